```python
import math
import jax, jax.numpy as jnp
from jax import lax
import numpy as np

D_MODEL = 2048
BATCH = 32
SEQ = 256
DEPTH = 4
DEC_BATCH = 2
DEC_SEQ = 1024
PAST_LEN = 256

GRID_W = 64
MIX_W = D_MODEL
MLA_HEADS = 8
MLA_NOPE = 128
MLA_ROPE = 64
MLA_QK = MLA_NOPE + MLA_ROPE
MLA_V = 128
MLA_Q_RANK = 768
MLA_KV_RANK = 512
ROPE_F = MLA_ROPE // 4
ROPE_BASE = 10000.0
Q_BLOCK = 128
GDN_HEADS = 4
GDN_DK = 128
GDN_DV = 128
GDN_CHUNK = 64
SSD_HEADS = 8
SSD_P = 64
SSD_GROUPS = 2
SSD_N = 128
SSD_CHUNK = 128
SSD_INNER = SSD_HEADS * SSD_P
CONV_W = 5
FF = -(-8 * D_MODEL // (3 * 256)) * 256
MLA_IN = MLA_Q_RANK + MLA_KV_RANK + MLA_ROPE
GDN_QK = GDN_HEADS * GDN_DK
GDN_VW = GDN_HEADS * GDN_DV
GDN_IN = 2 * GDN_QK + 2 * GDN_VW + 4 * GDN_HEADS
SSD_XBC = SSD_INNER + 2 * SSD_GROUPS * SSD_N
SSD_IN = SSD_INNER + SSD_XBC + 2 * SSD_HEADS
IN_W = MLA_IN + GDN_IN + SSD_IN
EPS = 1e-6

kernel_name = "hybrid_mla_gdn_ssd_diffusion_step"

f32 = jnp.float32


def rmsnorm(x, g):
    xf = x.astype(f32)
    y = xf * lax.rsqrt(jnp.mean(xf * xf, -1, keepdims=True) + EPS)
    return (y * g.astype(f32)).astype(x.dtype)


def l2norm(x):
    xf = x.astype(f32)
    return (xf * lax.rsqrt(jnp.sum(xf * xf, -1, keepdims=True) + EPS)).astype(x.dtype)


def dwconv_centred(x, w):
    pad = (w.shape[0] - 1) // 2
    return lax.conv_general_dilated(x, w[:, None, :].astype(x.dtype), (1,), [(pad, pad)],
                                    dimension_numbers=('NWC', 'WIO', 'NWC'),
                                    feature_group_count=x.shape[-1])


def axial_rope_tables(n_tokens):
    rows = n_tokens // GRID_W
    row = jnp.repeat(jnp.arange(rows, dtype=f32), GRID_W)
    col = jnp.tile(jnp.arange(GRID_W, dtype=f32), rows)
    inv = ROPE_BASE ** (-jnp.arange(ROPE_F, dtype=f32) / ROPE_F)
    ang = jnp.stack([row[:, None] * inv, col[:, None] * inv], axis=1)
    return jnp.cos(ang), jnp.sin(ang)


def rope2d(u, cos, sin):
    sh = u.shape
    uf = u.astype(f32).reshape(sh[:-1] + (2, 2, ROPE_F))
    c, s = cos[:, None], sin[:, None]
    u1, u2 = uf[..., 0, :], uf[..., 1, :]
    out = jnp.stack([u1 * c - u2 * s, u2 * c + u1 * s], axis=-2)
    return out.reshape(sh).astype(u.dtype)


def rope_tail(t, rope):
    cos, sin = rope
    return jnp.concatenate([t[..., :MLA_NOPE], rope2d(t[..., MLA_NOPE:], cos, sin)], -1)


def softmax_attention(q, k, v):
    B, Tq, H, D = q.shape
    nb = Tq // Q_BLOCK
    scale = D ** -0.5
    qb = q.reshape(B, nb, Q_BLOCK, H, D).swapaxes(0, 1)

    def block(qi):
        s = jnp.einsum('bqhd,bkhd->bhqk', qi, k).astype(f32) * scale
        pr = jax.nn.softmax(s, axis=-1).astype(v.dtype)
        return jnp.einsum('bhqk,bkhd->bqhd', pr, v)

    o = lax.map(block, qb)
    return o.swapaxes(0, 1).reshape(B, Tq, H, v.shape[-1])


def mla_keys_values(ckv, krope, p, l):
    B, T, _ = ckv.shape
    kv = (ckv @ p['mla_w_ukv'][l]).reshape(B, T, MLA_HEADS, MLA_NOPE + MLA_V)
    k_nope, v = kv[..., :MLA_NOPE], kv[..., MLA_NOPE:]
    k_pe = jnp.broadcast_to(krope[:, :, None, :], (B, T, MLA_HEADS, MLA_ROPE)).astype(k_nope.dtype)
    k = jnp.concatenate([k_nope, k_pe], -1)
    return rmsnorm(k, p['mla_k_g'][l]), v


def mla(u, p, l, rope, ctx_ckv, ctx_krope):
    B, T, _ = u.shape
    cq, ckv, krope = jnp.split(u, [MLA_Q_RANK, MLA_Q_RANK + MLA_KV_RANK], -1)
    ckv = rmsnorm(ckv, p['mla_kvnorm_g'][l])
    q = (rmsnorm(cq, p['mla_qnorm_g'][l]) @ p['mla_w_uq'][l]).reshape(B, T, MLA_HEADS, MLA_QK)
    q = rmsnorm(q, p['mla_q_g'][l])
    k, v = mla_keys_values(ckv, krope, p, l)
    if rope is not None:
        q = rope_tail(q, rope)
        k = rope_tail(k, rope)
    if ctx_ckv is not None:
        kc, vc = mla_keys_values(ctx_ckv.astype(u.dtype), ctx_krope.astype(u.dtype), p, l)
        k = jnp.concatenate([kc, k], 1)
        v = jnp.concatenate([vc, v], 1)
    o = softmax_attention(q, k, v)
    return o.reshape(B, T, MLA_HEADS * MLA_V), ckv, krope


def delta_rule_chunked(q, k, v, g, beta, s0):
    B, T, H, DK = q.shape
    DV = v.shape[-1]
    C = GDN_CHUNK
    N = T // C
    ch4 = lambda t: t.astype(f32).reshape(B, N, C, H, -1).transpose(1, 0, 3, 2, 4)
    ch3 = lambda t: t.astype(f32).reshape(B, N, C, H).transpose(1, 0, 3, 2)
    q = ch4(q) * (DK ** -0.5)
    k = ch4(k)
    v = ch4(v)
    beta = ch3(beta)
    gc = jnp.cumsum(ch3(g), -1)
    tri = jnp.tril(jnp.ones((C, C), bool))
    strict = jnp.tril(jnp.ones((C, C), bool), -1)
    decay = jnp.exp(jnp.where(tri, gc[..., :, None] - gc[..., None, :], -jnp.inf))
    kb = k * beta[..., None]
    A = jnp.where(strict, jnp.einsum('nbhid,nbhjd->nbhij', kb, k) * decay, 0.0)
    eye = jnp.eye(C, dtype=f32)
    Tm = lax.linalg.triangular_solve(A + eye, jnp.broadcast_to(eye, A.shape), left_side=True,
                                     lower=True, unit_diagonal=True)
    w = Tm @ (kb * jnp.exp(gc)[..., None])
    uu = Tm @ (v * beta[..., None])
    qk = jnp.where(tri, jnp.einsum('nbhid,nbhjd->nbhij', q, k) * decay, 0.0)

    def step(S, inp):
        qi, ki, ui, wi, gi, ai = inp
        v_new = ui - wi @ S
        o = (qi * jnp.exp(gi)[..., None]) @ S + ai @ v_new
        gl = gi[..., -1]
        S = S * jnp.exp(gl)[..., None, None] + jnp.einsum(
            'bhcd,bhce->bhde', ki * jnp.exp(gl[..., None] - gi)[..., None], v_new)
        return S, o

    s_fin, o = lax.scan(step, s0.astype(f32), (q, k, uu, w, gc, qk))
    return o.transpose(1, 0, 3, 2, 4).reshape(B, T, H, DV), s_fin


def gated_deltanet(u, p, l, s0):
    B, T, _ = u.shape
    qkv, z, b, a = jnp.split(u, [2 * GDN_QK + GDN_VW, 2 * GDN_QK + 2 * GDN_VW,
                                 2 * GDN_QK + 2 * GDN_VW + 2 * GDN_HEADS], -1)
    qkv = jax.nn.silu(dwconv_centred(qkv, p['gdn_conv_w'][l]))
    q, k, v = jnp.split(qkv, [GDN_QK, 2 * GDN_QK], -1)
    q = l2norm(q.reshape(B, T, GDN_HEADS, GDN_DK))
    k = l2norm(k.reshape(B, T, GDN_HEADS, GDN_DK))
    v = v.reshape(B, T, GDN_HEADS, GDN_DV)
    beta = jax.nn.sigmoid(b.astype(f32)).reshape(B, T, 2, GDN_HEADS)
    g = -jnp.exp(p['gdn_a_log'][l].astype(f32)) * jax.nn.softplus(
        a.astype(f32).reshape(B, T, 2, GDN_HEADS) + p['gdn_dt_bias'][l].astype(f32))
    if s0 is None:
        s0 = jnp.zeros((B, 2, GDN_HEADS, GDN_DK, GDN_DV), f32)
    flip = lambda t: jnp.flip(t, 1)
    o_f, s_f = delta_rule_chunked(q, k, v, g[:, :, 0], beta[:, :, 0], s0[:, 0])
    o_b, s_b = delta_rule_chunked(flip(q), flip(k), flip(v), flip(g[:, :, 1]), flip(beta[:, :, 1]), s0[:, 1])
    o = o_f + flip(o_b)
    o = rmsnorm(o, p['gdn_norm_g'][l]) * jax.nn.silu(z.astype(f32).reshape(B, T, GDN_HEADS, GDN_DV))
    return o.reshape(B, T, GDN_VW).astype(u.dtype), jnp.stack([s_f, s_b], 1).astype(u.dtype)


def ssd_chunked(x, dt, A, Bm, Cm, s0):
    B, T, H, P = x.shape
    Q = SSD_CHUNK
    nc = T // Q
    rep = H // Bm.shape[2]
    Bh = jnp.repeat(Bm.astype(f32), rep, 2).reshape(B, nc, Q, H, -1)
    Ch = jnp.repeat(Cm.astype(f32), rep, 2).reshape(B, nc, Q, H, -1)
    xdt = (x.astype(f32) * dt[..., None]).reshape(B, nc, Q, H, P)
    acum = jnp.cumsum((dt * A).reshape(B, nc, Q, H).transpose(0, 3, 1, 2), -1)
    tri = jnp.tril(jnp.ones((Q, Q), bool))
    Lmat = jnp.exp(jnp.where(tri, acum[..., :, None] - acum[..., None, :], -jnp.inf))
    scores = jnp.einsum('bclhn,bcshn->bhcls', Ch, Bh) * Lmat
    y = jnp.einsum('bhcls,bcshp->bclhp', scores, xdt)
    decay_to_end = jnp.exp(acum[..., -1:] - acum)
    chunk_states = jnp.einsum('bcshn,bhcs,bcshp->cbhpn', Bh, decay_to_end, xdt)
    chunk_decay = jnp.exp(acum[..., -1]).transpose(2, 0, 1)

    def step(S, inp):
        st, dec = inp
        return S * dec[..., None, None] + st, S

    s_fin, s_start = lax.scan(step, s0.astype(f32), (chunk_states, chunk_decay))
    y = y + jnp.einsum('bclhn,cbhpn,bhcl->bclhp', Ch, s_start, jnp.exp(acum))
    return y.reshape(B, T, H, P), s_fin


def mamba2_ssd(u, p, l, s0):
    B, T, _ = u.shape
    z, xbc, dt2 = jnp.split(u, [SSD_INNER, SSD_INNER + SSD_XBC], -1)
    xbc = jax.nn.silu(dwconv_centred(xbc, p['ssd_conv_w'][l]) + p['ssd_conv_b'][l])
    x, Bm, Cm = jnp.split(xbc, [SSD_INNER, SSD_INNER + SSD_GROUPS * SSD_N], -1)
    x = x.reshape(B, T, SSD_HEADS, SSD_P)
    Bm = Bm.reshape(B, T, SSD_GROUPS, SSD_N)
    Cm = Cm.reshape(B, T, SSD_GROUPS, SSD_N)
    dt = jax.nn.softplus(dt2.astype(f32).reshape(B, T, 2, SSD_HEADS) + p['ssd_dt_bias'][l].astype(f32))
    A = -jnp.exp(p['ssd_a_log'][l].astype(f32))
    if s0 is None:
        s0 = jnp.zeros((B, 2, SSD_HEADS, SSD_P, SSD_N), f32)
    flip = lambda t: jnp.flip(t, 1)
    y_f, s_f = ssd_chunked(x, dt[:, :, 0], A[0], Bm, Cm, s0[:, 0])
    y_b, s_b = ssd_chunked(flip(x), flip(dt[:, :, 1]), A[1], flip(Bm), flip(Cm), s0[:, 1])
    y = y_f + flip(y_b) + x.astype(f32) * p['ssd_d'][l].astype(f32)[:, None]
    y = y.reshape(B, T, SSD_INNER) * jax.nn.silu(z.astype(f32))
    y = rmsnorm(y.reshape(B, T, SSD_GROUPS, SSD_INNER // SSD_GROUPS),
                p['ssd_norm_g'][l].reshape(SSD_GROUPS, SSD_INNER // SSD_GROUPS))
    return y.reshape(B, T, SSD_INNER).astype(u.dtype), jnp.stack([s_f, s_b], 1).astype(u.dtype)


def trunk_layer(x, cvec, p, l, rope, ctx):
    mods = (jax.nn.silu(cvec) @ p['ada_w'][l] + p['ada_b'][l]).reshape(cvec.shape[0], 6, 1, D_MODEL)
    shift1, scale1, gate1, shift2, scale2, gate2 = mods.transpose(1, 0, 2, 3)
    h = rmsnorm(x, p['norm1_g'][l]) * (1 + scale1) + shift1
    u = h @ p['w_in'][l]
    u_mla, u_gdn, u_ssd = jnp.split(u, [MLA_IN, MLA_IN + GDN_IN], -1)
    ctx_ckv, ctx_kr, s_gdn0, s_ssd0 = ctx if ctx is not None else (None, None, None, None)
    o_mla, ckv, kr = mla(u_mla, p, l, rope, ctx_ckv, ctx_kr)
    o_gdn, s_gdn = gated_deltanet(u_gdn, p, l, s_gdn0)
    o_ssd, s_ssd = mamba2_ssd(u_ssd, p, l, s_ssd0)
    mixed = jnp.concatenate([o_mla, o_gdn, o_ssd], -1) @ p['w_out'][l]
    x = x + gate1 * mixed
    h = rmsnorm(x, p['norm2_g'][l]) * (1 + scale2) + shift2
    gt, up = jnp.split(h @ p['ffn_w_gu'][l], 2, -1)
    x = x + gate2 * ((jax.nn.silu(gt) * up) @ p['ffn_w_down'][l])
    return x, (ckv, kr, s_gdn, s_ssd)


def setup_inputs(seed: int = 0) -> dict:
    key = jax.random.key(seed)
    ks = iter(jax.random.split(key, 40))
    nrm = lambda shape, s: jax.random.normal(next(ks), shape, f32) * s
    gain = lambda shape: 1.0 + 0.02 * jax.random.normal(next(ks), shape, f32)

    def dt_bias(shape):
        dt = jnp.exp(jax.random.uniform(next(ks), shape, f32, math.log(1e-3), math.log(1e-1)))
        return dt + jnp.log(-jnp.expm1(-dt))

    a_log = lambda shape: jnp.log(jax.random.uniform(next(ks), shape, f32, 1.0, 16.0))
    L = DEPTH
    return {
        'x_prompt': nrm((BATCH, SEQ, D_MODEL), 1.0),
        'x_sample': nrm((DEC_BATCH, DEC_SEQ, D_MODEL), 1.0),
        'cache_mla_ckv': nrm((DEC_BATCH, L, PAST_LEN, MLA_KV_RANK), 1.0),
        'cache_mla_krope': nrm((DEC_BATCH, L, PAST_LEN, MLA_ROPE), 1.0),
        'state_gdn': nrm((DEC_BATCH, L, 2, GDN_HEADS, GDN_DK, GDN_DV), 0.1),
        'state_ssd': nrm((DEC_BATCH, L, 2, SSD_HEADS, SSD_P, SSD_N), 0.1),
        'c': nrm((DEC_BATCH, D_MODEL), 1.0),
        'c_ctx': nrm((D_MODEL,), 1.0),
        'norm1_g': gain((L, D_MODEL)),
        'norm2_g': gain((L, D_MODEL)),
        'ada_w': nrm((L, D_MODEL, 6 * D_MODEL), 0.5 * D_MODEL ** -0.5),
        'ada_b': nrm((L, 6 * D_MODEL), 0.01),
        'w_in': nrm((L, D_MODEL, IN_W), D_MODEL ** -0.5),
        'w_out': nrm((L, MIX_W, D_MODEL), MIX_W ** -0.5),
        'mla_qnorm_g': gain((L, MLA_Q_RANK)),
        'mla_w_uq': nrm((L, MLA_Q_RANK, MLA_HEADS * MLA_QK), MLA_Q_RANK ** -0.5),
        'mla_kvnorm_g': gain((L, MLA_KV_RANK)),
        'mla_w_ukv': nrm((L, MLA_KV_RANK, MLA_HEADS * (MLA_NOPE + MLA_V)), MLA_KV_RANK ** -0.5),
        'mla_q_g': gain((L, MLA_QK)),
        'mla_k_g': gain((L, MLA_QK)),
        'gdn_conv_w': nrm((L, CONV_W, 2 * GDN_QK + GDN_VW), CONV_W ** -0.5),
        'gdn_a_log': a_log((L, 2, GDN_HEADS)),
        'gdn_dt_bias': dt_bias((L, 2, GDN_HEADS)),
        'gdn_norm_g': gain((L, GDN_DV)),
        'ssd_conv_w': nrm((L, CONV_W, SSD_XBC), CONV_W ** -0.5),
        'ssd_conv_b': nrm((L, SSD_XBC), 0.01),
        'ssd_a_log': a_log((L, 2, SSD_HEADS)),
        'ssd_dt_bias': dt_bias((L, 2, SSD_HEADS)),
        'ssd_d': gain((L, SSD_HEADS)),
        'ssd_norm_g': gain((L, SSD_INNER)),
        'ffn_w_gu': nrm((L, D_MODEL, 2 * FF), D_MODEL ** -0.5),
        'ffn_w_down': nrm((L, FF, D_MODEL), FF ** -0.5),
    }


def reference(x_prompt, x_sample, cache_mla_ckv, cache_mla_krope, state_gdn, state_ssd, c, c_ctx,
              norm1_g, norm2_g, ada_w, ada_b, w_in, w_out, mla_qnorm_g, mla_w_uq, mla_kvnorm_g,
              mla_w_ukv, mla_q_g, mla_k_g, gdn_conv_w, gdn_a_log, gdn_dt_bias, gdn_norm_g,
              ssd_conv_w, ssd_conv_b, ssd_a_log, ssd_dt_bias, ssd_d, ssd_norm_g, ffn_w_gu, ffn_w_down):
    p = dict(norm1_g=norm1_g, norm2_g=norm2_g, ada_w=ada_w, ada_b=ada_b, w_in=w_in, w_out=w_out,
             mla_qnorm_g=mla_qnorm_g, mla_w_uq=mla_w_uq, mla_kvnorm_g=mla_kvnorm_g,
             mla_w_ukv=mla_w_ukv, mla_q_g=mla_q_g, mla_k_g=mla_k_g, gdn_conv_w=gdn_conv_w,
             gdn_a_log=gdn_a_log, gdn_dt_bias=gdn_dt_bias, gdn_norm_g=gdn_norm_g,
             ssd_conv_w=ssd_conv_w, ssd_conv_b=ssd_conv_b, ssd_a_log=ssd_a_log,
             ssd_dt_bias=ssd_dt_bias, ssd_d=ssd_d, ssd_norm_g=ssd_norm_g,
             ffn_w_gu=ffn_w_gu, ffn_w_down=ffn_w_down)
    xp = x_prompt
    cc = c_ctx[None, :]
    ckvs, krs, sgs, sss = [], [], [], []
    for l in range(DEPTH):
        xp, (ckv, kr, sg, ss) = trunk_layer(xp, cc, p, l, None, None)
        ckvs.append(ckv)
        krs.append(kr)
        sgs.append(sg)
        sss.append(ss)
    rope = axial_rope_tables(x_sample.shape[1])
    xs = x_sample
    for l in range(DEPTH):
        xs, _ = trunk_layer(xs, c, p, l, rope,
                            (cache_mla_ckv[:, l], cache_mla_krope[:, l], state_gdn[:, l], state_ssd[:, l]))
    return (xp, xs, jnp.stack(ckvs, 1), jnp.stack(krs, 1), jnp.stack(sgs, 1), jnp.stack(sss, 1))
```

```python
import functools

import numpy as np
import jax
import jax.numpy as jnp
from jax import lax
from jax.experimental import pallas as pl
from jax.experimental.pallas import tpu as pltpu

f32 = jnp.float32
bf16 = jnp.bfloat16
HIGHEST = lax.Precision.HIGHEST

D_MODEL = 2048
BATCH = 32
SEQ = 256
DEPTH = 4
DEC_BATCH = 2
DEC_SEQ = 1024
PAST_LEN = 256
GRID_W = 64
MLA_HEADS = 8
MLA_NOPE = 128
MLA_ROPE = 64
MLA_QK = MLA_NOPE + MLA_ROPE
MLA_V = 128
MLA_Q_RANK = 768
MLA_KV_RANK = 512
ROPE_F = MLA_ROPE // 4
ROPE_BASE = 10000.0
GDN_HEADS = 4
GDN_DK = 128
GDN_DV = 128
GDN_CHUNK = 64
SSD_HEADS = 8
SSD_P = 64
SSD_GROUPS = 2
SSD_N = 128
SSD_CHUNK = 128
SSD_INNER = SSD_HEADS * SSD_P
CONV_W = 5
FF = -(-8 * D_MODEL // (3 * 256)) * 256
EPS = 1e-6

LANES = 128
MLA_QK_PAD = 256
HEAD_PER_GROUP = SSD_HEADS // SSD_GROUPS

C_CQ = 0
C_SM = 768
C_XBC = 1024
C_CKV = 2048
C_GZ = 2560
C_GQKV = 3072
C_SZ = 4608
N_IN = 5120
SM_BETA = 64
SM_A = 72
SM_DT = 80

VMEM_MB = 1024 * 1024


def _cparams(sem, vmem_mb):
    return pltpu.CompilerParams(dimension_semantics=sem, vmem_limit_bytes=vmem_mb * VMEM_MB)


def _blk(off, width):
    assert off % width == 0
    return off // width


def _sigmoid(x):
    return 1.0 / (1.0 + jnp.exp(-x))


def _silu(x):
    return x * _sigmoid(x)


def _softplus(x):
    return jnp.maximum(x, 0.0) + jnp.log(1.0 + jnp.exp(-jnp.abs(x)))


def _rms_scale(x, n):
    return lax.rsqrt(jnp.sum(x * x, axis=-1, keepdims=True) / n + EPS)


def _lane_col(a, lane_idx):
    lane = lax.broadcasted_iota(jnp.int32, a.shape, 1)
    return jnp.sum(jnp.where(lane == lane_idx, a, 0.0), axis=1, keepdims=True)


def _dot(a, b):
    return jnp.dot(a, b, preferred_element_type=f32)


def _dot_nt(a, b):
    return lax.dot_general(a, b, (((1,), (1,)), ((), ())), preferred_element_type=f32)


def _dot_tn(a, b):
    return lax.dot_general(a, b, (((0,), (0,)), ((), ())), preferred_element_type=f32)


def _dot_hi(a, b):
    return jnp.dot(a, b, precision=HIGHEST, preferred_element_type=f32)


def _mods_kernel(c_ref, w_ref, b_ref, o_ref):
    s = _silu(c_ref[...]).astype(bf16)
    o_ref[0] = _dot(s, w_ref[0].astype(bf16)) + b_ref[0]


def _mods(cvec8, ada_w, ada_b):
    tn = 1024
    n = 6 * D_MODEL
    return pl.pallas_call(
        _mods_kernel,
        grid=(DEPTH, n // tn),
        in_specs=[pl.BlockSpec((8, D_MODEL), lambda l, j: (0, 0)),
                  pl.BlockSpec((1, D_MODEL, tn), lambda l, j: (l, 0, j)),
                  pl.BlockSpec((1, 1, tn), lambda l, j: (l, 0, j))],
        out_specs=pl.BlockSpec((1, 8, tn), lambda l, j: (l, 0, j)),
        out_shape=jax.ShapeDtypeStruct((DEPTH, 8, n), f32),
        compiler_params=_cparams(("parallel", "parallel"), 40),
        name="adaln_mods",
    )(cvec8, ada_w, ada_b.reshape(DEPTH, 1, n))


NORM_ROWS = 256


def _norm_mm_kernel(x_ref, m_ref, g_ref, w_ref, o_ref, h_ref, *, shift_row, scale_row, swiglu):
    tm = x_ref.shape[0]

    @pl.when(pl.program_id(1) == 0)
    def _():
        shift = m_ref[0, shift_row:shift_row + 1, :]
        scale1p = 1.0 + m_ref[0, scale_row:scale_row + 1, :]
        g = g_ref[...]

        def body(r, carry):
            sl = pl.ds(pl.multiple_of(r * NORM_ROWS, NORM_ROWS), NORM_ROWS)
            x = x_ref[sl, :]
            y = x * lax.rsqrt(jnp.mean(x * x, axis=-1, keepdims=True) + EPS) * g
            h_ref[sl, :] = (y * scale1p + shift).astype(bf16)
            return carry

        lax.fori_loop(0, tm // NORM_ROWS, body, 0)

    r = _dot(h_ref[...], w_ref[...])
    if swiglu:
        tf = r.shape[1] // 2
        o_ref[...] = (_silu(r[:, :tf]) * r[:, tf:]).astype(o_ref.dtype)
    else:
        o_ref[...] = r.astype(o_ref.dtype)


def _norm_mm(x, mods, g, w, *, shift_row, scale_row, swiglu, tm, tn, out_dtype, name):
    m_rows, k = x.shape
    wn = 2 * tn if swiglu else tn
    n_tiles = w.shape[1] // wn
    rows_per_cond = m_rows // mods.shape[0]
    kern = functools.partial(_norm_mm_kernel, shift_row=shift_row, scale_row=scale_row, swiglu=swiglu)
    return pl.pallas_call(
        kern,
        grid=(m_rows // tm, n_tiles),
        in_specs=[pl.BlockSpec((tm, k), lambda i, j: (i, 0)),
                  pl.BlockSpec((1, 6, k), lambda i, j: ((i * tm) // rows_per_cond, 0, 0)),
                  pl.BlockSpec((1, k), lambda i, j: (0, 0)),
                  pl.BlockSpec((k, wn), lambda i, j: (0, j))],
        out_specs=pl.BlockSpec((tm, tn), lambda i, j: (i, j)),
        out_shape=jax.ShapeDtypeStruct((m_rows, n_tiles * tn), out_dtype),
        scratch_shapes=[pltpu.VMEM((tm, k), bf16)],
        compiler_params=_cparams(("parallel", "arbitrary"), 48),
        name=name,
    )(x, mods, g, w)


def _mm_res_kernel(*refs, n_in, gate_row):
    a_refs, w_refs = refs[:n_in], refs[n_in:2 * n_in]
    x_ref, m_ref, o_ref = refs[2 * n_in:]
    acc = _dot(a_refs[0][...], w_refs[0][...])
    for t in range(1, n_in):
        acc = acc + _dot(a_refs[t][...], w_refs[t][...])
    o_ref[...] = x_ref[...] + m_ref[0, gate_row:gate_row + 1, :] * acc


def _mm_res(acts, ws, x, mods, *, gate_row, tm, tn, name):
    m_rows, n = x.shape
    n_in = len(acts)
    rows_per_cond = m_rows // mods.shape[0]
    in_specs = [pl.BlockSpec((tm, a.shape[1]), lambda i, j: (i, 0)) for a in acts]
    in_specs += [pl.BlockSpec((w.shape[0], tn), lambda i, j: (0, j)) for w in ws]
    in_specs += [pl.BlockSpec((tm, tn), lambda i, j: (i, j)),
                 pl.BlockSpec((1, 6, tn), lambda i, j: ((i * tm) // rows_per_cond, 0, j))]
    return pl.pallas_call(
        functools.partial(_mm_res_kernel, n_in=n_in, gate_row=gate_row),
        grid=(m_rows // tm, n // tn),
        in_specs=in_specs,
        out_specs=pl.BlockSpec((tm, tn), lambda i, j: (i, j)),
        out_shape=jax.ShapeDtypeStruct((m_rows, n), f32),
        compiler_params=_cparams(("parallel", "parallel"), 48),
        name=name,
    )(*acts, *ws, x, mods)


def _rope(x, cos, s_up, s_dn):
    return x * cos + pltpu.roll(x, LANES - ROPE_F, 1) * s_up + pltpu.roll(x, ROPE_F, 1) * s_dn


def _mla_kv_rows(ckvn_bf, kr, wuk, wuv, kg, rope, k_scr, v_scr, r0):
    n = kr.shape[0]
    kn = _dot(ckvn_bf, wuk)
    v = _dot(ckvn_bf, wuv)
    krg = kr * kg[:, MLA_NOPE:]
    if rope is not None:
        krg = _rope(krg, *rope)
    kr_ss = jnp.sum(kr * kr, axis=-1, keepdims=True)
    for h in range(MLA_HEADS):
        knh = kn[:, h * MLA_NOPE:(h + 1) * MLA_NOPE]
        r = lax.rsqrt((jnp.sum(knh * knh, axis=-1, keepdims=True) + kr_ss) / MLA_QK + EPS)
        kh = jnp.concatenate([knh * r * kg[:, :MLA_NOPE], krg * r], axis=-1)
        k_scr[h, r0:r0 + n, :] = kh.astype(bf16)
        v_scr[h, r0:r0 + n, :] = v[:, h * MLA_V:(h + 1) * MLA_V].astype(bf16)


def _mla_attend(cq, wuq, qn_g, qg, rope, k_scr, v_scr, o_ref):
    cqn = (cq * _rms_scale(cq, MLA_Q_RANK) * qn_g).astype(bf16)
    q_all = _dot(cqn, wuq)
    scale = MLA_QK ** -0.5
    for h in range(MLA_HEADS):
        qh = q_all[:, h * MLA_QK_PAD:(h + 1) * MLA_QK_PAD]
        qh = qh * _rms_scale(qh, MLA_QK) * qg
        if rope is not None:
            qh = jnp.concatenate([qh[:, :MLA_NOPE], _rope(qh[:, MLA_NOPE:], *rope)], axis=-1)
        s = _dot_nt(qh.astype(bf16), k_scr[h]) * scale
        p = jnp.exp(s - jnp.max(s, axis=-1, keepdims=True))
        l = jnp.sum(p, axis=-1, keepdims=True)
        oh = _dot(p.astype(bf16), v_scr[h]) / l
        o_ref[:, h * MLA_V:(h + 1) * MLA_V] = oh.astype(o_ref.dtype)


def _krope_lanes(sm):
    lane = lax.broadcasted_iota(jnp.int32, sm.shape, 1)
    return jnp.where(lane < MLA_ROPE, sm, 0.0)


def _mla_ctx_kernel(cq_ref, ckv_ref, sm_ref, wuq_ref, wuk_ref, wuv_ref, qn_ref, kvn_ref, qg_ref, kg_ref,
                    o_ref, ckvn_ref, k_scr, v_scr):
    ckv = ckv_ref[...]
    ckvn = ckv * _rms_scale(ckv, MLA_KV_RANK) * kvn_ref[...]
    ckvn_ref[...] = ckvn
    _mla_kv_rows(ckvn.astype(bf16), _krope_lanes(sm_ref[...]), wuk_ref[...], wuv_ref[...], kg_ref[...],
                 None, k_scr, v_scr, 0)
    _mla_attend(cq_ref[...], wuq_ref[...], qn_ref[...], qg_ref[...], None, k_scr, v_scr, o_ref)


MLA_QB = 256


def _mla_lat_kernel(cq_ref, ckv_ref, sm_ref, cckv_ref, ckr_ref, rq_ref, rk_ref,
                    wuq_ref, wuk_ref, wuv_ref, qn_ref, kvn_ref, qg_ref, kg_ref, o_ref, k_scr, v_scr):
    @pl.when(pl.program_id(1) == 0)
    def _():
        kg = kg_ref[...]
        _mla_kv_rows(cckv_ref[0, 0].astype(bf16), ckr_ref[0, 0], wuk_ref[...], wuv_ref[...], kg,
                     None, k_scr, v_scr, 0)
        for c in range(DEC_SEQ // MLA_QB):
            rows = slice(c * MLA_QB, (c + 1) * MLA_QB)
            ckv = ckv_ref[rows, :]
            ckvn = ckv * _rms_scale(ckv, MLA_KV_RANK) * kvn_ref[...]
            rope = (rk_ref[0, rows, :], rk_ref[1, rows, :], rk_ref[2, rows, :])
            _mla_kv_rows(ckvn.astype(bf16), _krope_lanes(sm_ref[rows, :]), wuk_ref[...], wuv_ref[...], kg,
                         rope, k_scr, v_scr, PAST_LEN + c * MLA_QB)

    rope_q = (rq_ref[0], rq_ref[1], rq_ref[2])
    _mla_attend(cq_ref[...], wuq_ref[...], qn_ref[...], qg_ref[...], rope_q, k_scr, v_scr, o_ref)


def _mla_weight_specs(nd):
    zero = (lambda *a: (0, 0))
    del nd
    return [pl.BlockSpec((MLA_Q_RANK, MLA_HEADS * MLA_QK_PAD), zero),
            pl.BlockSpec((MLA_KV_RANK, MLA_HEADS * MLA_NOPE), zero),
            pl.BlockSpec((MLA_KV_RANK, MLA_HEADS * MLA_V), zero),
            pl.BlockSpec((1, MLA_Q_RANK), zero),
            pl.BlockSpec((1, MLA_KV_RANK), zero),
            pl.BlockSpec((1, MLA_QK_PAD), zero),
            pl.BlockSpec((1, MLA_QK_PAD), zero)]


def _mla_ctx(u, wl):
    t = SEQ
    return pl.pallas_call(
        _mla_ctx_kernel,
        grid=(BATCH,),
        in_specs=[pl.BlockSpec((t, MLA_Q_RANK), lambda i: (i, _blk(C_CQ, MLA_Q_RANK))),
                  pl.BlockSpec((t, MLA_KV_RANK), lambda i: (i, _blk(C_CKV, MLA_KV_RANK))),
                  pl.BlockSpec((t, LANES), lambda i: (i, _blk(C_SM, LANES)))] + _mla_weight_specs(1),
        out_specs=[pl.BlockSpec((t, MLA_HEADS * MLA_V), lambda i: (i, 0)),
                   pl.BlockSpec((t, MLA_KV_RANK), lambda i: (i, 0))],
        out_shape=[jax.ShapeDtypeStruct((BATCH * t, MLA_HEADS * MLA_V), bf16),
                   jax.ShapeDtypeStruct((BATCH * t, MLA_KV_RANK), f32)],
        scratch_shapes=[pltpu.VMEM((MLA_HEADS, t, MLA_QK_PAD), bf16),
                        pltpu.VMEM((MLA_HEADS, t, MLA_V), bf16)],
        compiler_params=_cparams(("parallel",), 40),
        name="mla_ctx",
    )(u, u, u, wl["w_uq"], wl["w_uk"], wl["w_uv"], wl["qn_g"], wl["kvn_g"], wl["q_g"], wl["k_g"])


def _mla_lat(u, cache_ckv, cache_kr, rope_tab, wl, layer):
    t = DEC_SEQ
    nq = t // MLA_QB
    tk = PAST_LEN + t
    return pl.pallas_call(
        _mla_lat_kernel,
        grid=(DEC_BATCH, nq),
        in_specs=[pl.BlockSpec((MLA_QB, MLA_Q_RANK), lambda s, q: (s * nq + q, _blk(C_CQ, MLA_Q_RANK))),
                  pl.BlockSpec((t, MLA_KV_RANK), lambda s, q: (s, _blk(C_CKV, MLA_KV_RANK))),
                  pl.BlockSpec((t, LANES), lambda s, q: (s, _blk(C_SM, LANES))),
                  pl.BlockSpec((1, 1, PAST_LEN, MLA_KV_RANK), lambda s, q: (s, layer, 0, 0)),
                  pl.BlockSpec((1, 1, PAST_LEN, LANES), lambda s, q: (s, layer, 0, 0)),
                  pl.BlockSpec((3, MLA_QB, LANES), lambda s, q: (0, q, 0)),
                  pl.BlockSpec((3, t, LANES), lambda s, q: (0, 0, 0))] + _mla_weight_specs(2),
        out_specs=pl.BlockSpec((MLA_QB, MLA_HEADS * MLA_V), lambda s, q: (s * nq + q, 0)),
        out_shape=jax.ShapeDtypeStruct((DEC_BATCH * t, MLA_HEADS * MLA_V), bf16),
        scratch_shapes=[pltpu.VMEM((MLA_HEADS, tk, MLA_QK_PAD), bf16),
                        pltpu.VMEM((MLA_HEADS, tk, MLA_V), bf16)],
        compiler_params=_cparams(("parallel", "arbitrary"), 48),
        name="mla_lat",
    )(u, u, u, cache_ckv, cache_kr, rope_tab, rope_tab,
      wl["w_uq"], wl["w_uk"], wl["w_uv"], wl["qn_g"], wl["kvn_g"], wl["q_g"], wl["k_g"])


CONV_PAD = 8
CONV_ROWS = 256


def _conv_silu(x_ref, w_ref, b_ref, xp_ref, out_ref, t):
    c = x_ref.shape[1]
    xp_ref[0:CONV_PAD, :] = jnp.zeros((CONV_PAD, c), f32)
    xp_ref[CONV_PAD:CONV_PAD + t, :] = x_ref[...]
    xp_ref[CONV_PAD + t:2 * CONV_PAD + t, :] = jnp.zeros((CONV_PAD, c), f32)
    half = (CONV_W - 1) // 2
    for r0 in range(0, t, CONV_ROWS):
        acc = None
        for j in range(CONV_W):
            start = CONV_PAD - half + j + r0
            term = w_ref[j:j + 1, :] * xp_ref[start:start + CONV_ROWS, :]
            acc = term if acc is None else acc + term
        if b_ref is not None:
            acc = acc + b_ref[...]
        out_ref[r0:r0 + CONV_ROWS, :] = _silu(acc)


def _mm_split(a, b, low):
    n = a.shape[0]
    a_hi = a.astype(bf16)
    a_lo = (a - a_hi.astype(f32)).astype(bf16)
    b_hi = b.astype(bf16).astype(f32)
    b_cat = (b_hi + pltpu.roll(b - b_hi, n, 1)).astype(bf16)
    r = _dot(jnp.concatenate([a_hi, a_lo], axis=0)[:, :n], b_cat)
    r = r[:n] + r[n:]
    return jnp.where(low, r + pltpu.roll(r, LANES - n, 1), 0.0)


def _unit_tri_inverse(x):
    n = x.shape[0]
    ii = lax.broadcasted_iota(jnp.int32, x.shape, 0)
    jj = lax.broadcasted_iota(jnp.int32, x.shape, 1)
    low = jj < n
    p = jnp.where(ii == jj, 1.0, 0.0) + x
    xp = x
    for _ in range(n.bit_length() - 2):
        xp = _mm_split(xp, xp, low)
        p = p + _mm_split(p, xp, low)
    return p


def _gdn_kernel(*refs, t, hb, has_s0, want_state):
    it = iter(refs)
    q_ref, k_ref, v_ref, z_ref, sm_ref = (next(it) for _ in range(5))
    cwq_ref, cwk_ref, cwv_ref, gp_ref, ng_ref = (next(it) for _ in range(5))
    s0_ref = next(it) if has_s0 else None
    o_ref = next(it)
    sfin_ref = next(it) if want_state else None
    xp_s, qn_s, kn_s, vn_s, beta_s, cum_s, cumt_s, wq_s, uu_s, kt_s, qk_s, eg_s, od_s, st_s = it

    ck = GDN_CHUNK
    nc = t // ck
    head0 = pl.program_id(1) * hb

    _conv_silu(q_ref, cwq_ref, None, xp_s, qn_s, t)
    _conv_silu(k_ref, cwk_ref, None, xp_s, kn_s, t)
    _conv_silu(v_ref, cwv_ref, None, xp_s, vn_s, t)
    for hh in range(hb):
        cols = slice(hh * GDN_DK, (hh + 1) * GDN_DK)
        qh = qn_s[:, cols]
        qn_s[:, cols] = qh * lax.rsqrt(jnp.sum(qh * qh, axis=-1, keepdims=True) + EPS) * (GDN_DK ** -0.5)
        kh = kn_s[:, cols]
        kn_s[:, cols] = kh * lax.rsqrt(jnp.sum(kh * kh, axis=-1, keepdims=True) + EPS)

    sm = sm_ref[...]
    beta_s[...] = _sigmoid(sm)
    g_all = -jnp.exp(gp_ref[0:1, :]) * _softplus(sm + gp_ref[1:2, :])

    ii = lax.broadcasted_iota(jnp.int32, (ck, LANES), 0)
    jj = lax.broadcasted_iota(jnp.int32, (ck, LANES), 1)
    incl = (ii >= jj, (ii <= jj) & (jj < ck))
    strict = (ii > jj, (ii < jj) & (jj < ck))
    tri = (incl[0][:, :ck].astype(f32), incl[1][:, :ck].astype(f32))

    for c in range(nc):
        rows = slice(c * ck, (c + 1) * ck)
        for d in range(2):
            cum = _dot_hi(tri[d], g_all[rows, :])
            cum_s[d, rows, :] = cum
            cumt_s[d, c] = jnp.concatenate([cum, jnp.zeros_like(cum)], axis=0).T

    if has_s0:
        for d in range(2):
            for hh in range(hb):
                st_s[d * hb + hh] = s0_ref[0, 0, d, hh]
    else:
        st_s[...] = jnp.zeros(st_s.shape, f32)

    def phase1(c, carry):
        r = pl.multiple_of(c * ck, ck)
        rows = pl.ds(r, ck)
        beta_c = beta_s[rows, :]
        for hh in range(hb):
            cols = slice(hh * GDN_DK, (hh + 1) * GDN_DK)
            kc = kn_s[rows, cols]
            qc = qn_s[rows, cols]
            vc = vn_s[rows, cols]
            kcb = kc.astype(bf16)
            kc_pad = jnp.concatenate([kcb, jnp.zeros_like(kcb)], axis=0)
            g_kk = _dot_nt(kcb, kc_pad)
            g_qk = _dot_nt(qc.astype(bf16), kc_pad)
            for d in range(2):
                idx = d * hb + hh
                lane_g = SM_A + d * GDN_HEADS + head0 + hh
                lane_b = SM_BETA + d * GDN_HEADS + head0 + hh
                cum_c = cum_s[d, rows, :]
                col = _lane_col(cum_c, lane_g)
                bcol = _lane_col(beta_c, lane_b)
                row = cumt_s[d, c, pl.ds(lane_g, 1), :]
                dec = jnp.where(incl[d], jnp.exp(col - row), 0.0)
                a = jnp.where(strict[d], bcol * g_kk * dec, 0.0)
                tm = _unit_tri_inverse(-a)
                ecol = jnp.exp(col)
                rhs = jnp.concatenate([kc * (bcol * ecol), vc * bcol], axis=-1)
                wu = _dot(tm[:, :ck].astype(bf16), rhs.astype(bf16))
                end = cum_c[ck - 1:ck, :] if d == 0 else cum_c[0:1, :]
                gl = _lane_col(end, lane_g)
                r2 = pl.multiple_of(c * 2 * ck, 2 * ck)
                wq_s[idx, pl.ds(r2, ck), :] = wu[:, :GDN_DK].astype(bf16)
                wq_s[idx, pl.ds(r2 + ck, ck), :] = (qc * ecol).astype(bf16)
                uu_s[idx, rows, :] = wu[:, GDN_DK:]
                kt_s[idx, rows, :] = (kc * jnp.exp(gl - col)).astype(bf16)
                qk_s[idx, rows, :] = jnp.where(incl[d], g_qk * dec, 0.0)[:, :ck].astype(bf16)
                eg_s[idx, pl.ds(c, 1), :] = jnp.broadcast_to(jnp.exp(gl), (1, LANES))
        return carry

    lax.fori_loop(0, nc, phase1, 0)

    def phase2(i, carry):
        for d in range(2):
            c = i if d == 0 else nc - 1 - i
            rows = pl.ds(pl.multiple_of(c * ck, ck), ck)
            rows2 = pl.ds(pl.multiple_of(c * 2 * ck, 2 * ck), 2 * ck)
            for hh in range(hb):
                idx = d * hb + hh
                s = st_s[idx]
                ws = _dot(wq_s[idx, rows2, :], s.astype(bf16))
                v_new = uu_s[idx, rows, :] - ws[:ck]
                vb = v_new.astype(bf16)
                o = ws[ck:] + _dot(qk_s[idx, rows, :], vb)
                st_s[idx] = s * eg_s[idx, pl.ds(c, 1), :] + _dot_tn(kt_s[idx, rows, :], vb)
                od_s[d, rows, hh * GDN_DV:(hh + 1) * GDN_DV] = o
        return carry

    lax.fori_loop(0, nc, phase2, 0)

    for hh in range(hb):
        cols = slice(hh * GDN_DV, (hh + 1) * GDN_DV)
        o = od_s[0, :, cols] + od_s[1, :, cols]
        on = o * lax.rsqrt(jnp.mean(o * o, axis=-1, keepdims=True) + EPS) * ng_ref[...]
        o_ref[:, cols] = (on * _silu(z_ref[:, cols])).astype(o_ref.dtype)
    if want_state:
        for d in range(2):
            for hh in range(hb):
                sfin_ref[0, d, hh] = st_s[d * hb + hh]


def _gdn(u, wl, *, nseq, t, hb, s0, layer, want_state):
    w = hb * GDN_DK
    nhb = GDN_HEADS // hb
    nc = t // GDN_CHUNK
    gq, gk, gv = C_GQKV, C_GQKV + GDN_HEADS * GDN_DK, C_GQKV + 2 * GDN_HEADS * GDN_DK
    in_specs = [pl.BlockSpec((t, w), lambda i, j: (i, _blk(gq, w) + j)),
                pl.BlockSpec((t, w), lambda i, j: (i, _blk(gk, w) + j)),
                pl.BlockSpec((t, w), lambda i, j: (i, _blk(gv, w) + j)),
                pl.BlockSpec((t, w), lambda i, j: (i, _blk(C_GZ, w) + j)),
                pl.BlockSpec((t, LANES), lambda i, j: (i, _blk(C_SM, LANES))),
                pl.BlockSpec((8, w), lambda i, j: (0, j)),
                pl.BlockSpec((8, w), lambda i, j: (0, nhb + j)),
                pl.BlockSpec((8, w), lambda i, j: (0, 2 * nhb + j)),
                pl.BlockSpec((8, LANES), lambda i, j: (0, 0)),
                pl.BlockSpec((1, GDN_DV), lambda i, j: (0, 0))]
    args = [u, u, u, u, u, wl["gdn_cw"], wl["gdn_cw"], wl["gdn_cw"], wl["gdn_gp"], wl["gdn_ng"]]
    if s0 is not None:
        in_specs.append(pl.BlockSpec((1, 1, 2, hb, GDN_DK, GDN_DV), lambda i, j: (i, layer, 0, j, 0, 0)))
        args.append(s0)
    out_specs = [pl.BlockSpec((t, w), lambda i, j: (i, j))]
    out_shape = [jax.ShapeDtypeStruct((nseq * t, GDN_HEADS * GDN_DV), bf16)]
    if want_state:
        out_specs.append(pl.BlockSpec((1, 2, hb, GDN_DK, GDN_DV), lambda i, j: (i, 0, j, 0, 0)))
        out_shape.append(jax.ShapeDtypeStruct((nseq, 2, GDN_HEADS, GDN_DK, GDN_DV), f32))
    scratch = [pltpu.VMEM((t + 2 * CONV_PAD, w), f32),
               pltpu.VMEM((t, w), f32), pltpu.VMEM((t, w), f32), pltpu.VMEM((t, w), f32),
               pltpu.VMEM((t, LANES), f32),
               pltpu.VMEM((2, t, LANES), f32),
               pltpu.VMEM((2, nc, LANES, LANES), f32),
               pltpu.VMEM((2 * hb, 2 * t, GDN_DK), bf16),
               pltpu.VMEM((2 * hb, t, GDN_DV), f32),
               pltpu.VMEM((2 * hb, t, GDN_DK), bf16),
               pltpu.VMEM((2 * hb, t, GDN_CHUNK), bf16),
               pltpu.VMEM((2 * hb, max(nc, 8), LANES), f32),
               pltpu.VMEM((2, t, w), f32),
               pltpu.VMEM((2 * hb, GDN_DK, GDN_DV), f32)]
    kern = functools.partial(_gdn_kernel, t=t, hb=hb, has_s0=s0 is not None, want_state=want_state)
    return pl.pallas_call(
        kern,
        grid=(nseq, nhb),
        in_specs=in_specs,
        out_specs=out_specs,
        out_shape=out_shape,
        scratch_shapes=scratch,
        compiler_params=_cparams(("parallel", "parallel"), 48),
        name="gdn_" + ("lat" if s0 is not None else "ctx"),
    )(*args)


def _ssd_kernel(*refs, t, has_s0, want_state):
    it = iter(refs)
    x_ref, b_ref, c_ref, z_ref, sm_ref = (next(it) for _ in range(5))
    cwx_ref, cwb_ref, cwc_ref, cbx_ref, cbb_ref, cbc_ref = (next(it) for _ in range(6))
    gp_ref, ex_ref, dv_ref, ng_ref = (next(it) for _ in range(4))
    s0_ref = next(it) if has_s0 else None
    o_ref = next(it)
    sfin_ref = next(it) if want_state else None
    xpx_s, xpb_s, xs_s, bs_s, cs_s, dt_s, da_s, y_s, st_s, cumt_s, dtt_s = it

    ck = SSD_CHUNK
    nc = t // ck
    wx = HEAD_PER_GROUP * SSD_P
    group = pl.program_id(1)

    _conv_silu(x_ref, cwx_ref, cbx_ref, xpx_s, xs_s, t)
    _conv_silu(b_ref, cwb_ref, cbb_ref, xpb_s, bs_s, t)
    _conv_silu(c_ref, cwc_ref, cbc_ref, xpb_s, cs_s, t)

    dt_all = _softplus(sm_ref[...] + gp_ref[1:2, :])
    dt_s[...] = dt_all
    da_s[...] = dt_all * (-jnp.exp(gp_ref[0:1, :]))

    if has_s0:
        for d in range(2):
            s0 = jnp.concatenate([s0_ref[0, 0, d, hh] for hh in range(HEAD_PER_GROUP)], axis=0)
            st_s[d] = s0.T
    else:
        st_s[...] = jnp.zeros(st_s.shape, f32)

    ii = lax.broadcasted_iota(jnp.int32, (ck, ck), 0)
    jj = lax.broadcasted_iota(jnp.int32, (ck, ck), 1)
    incl = (ii >= jj, ii <= jj)
    tri = (incl[0].astype(f32), incl[1].astype(f32))
    lane = lax.broadcasted_iota(jnp.int32, (ck, LANES), 1)

    def step(i, carry):
        for d in range(2):
            c = i if d == 0 else nc - 1 - i
            rows = pl.ds(pl.multiple_of(c * ck, ck), ck)
            cum = _dot_hi(tri[d], da_s[rows, :])
            dt_c = dt_s[rows, :]
            cumt_s[d] = cum.T
            dtt_s[d] = dt_c.T
            cc = cs_s[rows, :].astype(bf16)
            bc = bs_s[rows, :].astype(bf16)
            xc = xs_s[rows, :]
            cb = _dot_nt(cc, bc)
            parts = []
            for pr in range(HEAD_PER_GROUP // 2):
                xpair = xc[:, pr * LANES:(pr + 1) * LANES].astype(bf16)
                ys = []
                for hh in (2 * pr, 2 * pr + 1):
                    ln = SM_DT + d * SSD_HEADS + group * HEAD_PER_GROUP + hh
                    col = _lane_col(cum, ln)
                    row = cumt_s[d, pl.ds(ln, 1), :]
                    dtrow = dtt_s[d, pl.ds(ln, 1), :]
                    lm = jnp.where(incl[d], jnp.exp(col - row), 0.0)
                    ys.append(_dot((cb * lm * dtrow).astype(bf16), xpair))
                parts.append(jnp.where(lane < SSD_P, ys[0], ys[1]))
            y_diag = jnp.concatenate(parts, axis=-1)
            end = cum[ck - 1:ck, :] if d == 0 else cum[0:1, :]
            ex = ex_ref[d, 0]
            x_scale = _dot_hi(jnp.exp(end - cum) * dt_c, ex)
            st_c = _dot_tn(bc, (xc * x_scale).astype(bf16))
            st = st_s[d]
            y_off = _dot(cc, st.astype(bf16)) * _dot_hi(jnp.exp(cum), ex)
            cd = _dot_hi(jnp.broadcast_to(jnp.exp(end), (8, LANES)), ex)[0:1, :]
            st_s[d] = st * cd + st_c
            y_s[d, rows, :] = y_diag + y_off
        return carry

    lax.fori_loop(0, nc, step, 0)

    y = y_s[0] + y_s[1] + xs_s[...] * dv_ref[...]
    y = y * _silu(z_ref[...])
    o_ref[...] = (y * lax.rsqrt(jnp.mean(y * y, axis=-1, keepdims=True) + EPS) * ng_ref[...]).astype(o_ref.dtype)
    if want_state:
        for d in range(2):
            stt = st_s[d].T
            for hh in range(HEAD_PER_GROUP):
                sfin_ref[0, d, hh] = stt[hh * SSD_P:(hh + 1) * SSD_P, :]


def _ssd(u, wl, *, nseq, t, s0, layer, want_state):
    wx = HEAD_PER_GROUP * SSD_P
    cx, cb, cc = C_XBC, C_XBC + SSD_INNER, C_XBC + SSD_INNER + SSD_GROUPS * SSD_N
    nxb = SSD_INNER // wx
    in_specs = [pl.BlockSpec((t, wx), lambda i, g: (i, _blk(cx, wx) + g)),
                pl.BlockSpec((t, SSD_N), lambda i, g: (i, _blk(cb, SSD_N) + g)),
                pl.BlockSpec((t, SSD_N), lambda i, g: (i, _blk(cc, SSD_N) + g)),
                pl.BlockSpec((t, wx), lambda i, g: (i, _blk(C_SZ, wx) + g)),
                pl.BlockSpec((t, LANES), lambda i, g: (i, _blk(C_SM, LANES))),
                pl.BlockSpec((8, wx), lambda i, g: (0, g)),
                pl.BlockSpec((8, SSD_N), lambda i, g: (0, _blk(SSD_INNER, SSD_N) + g)),
                pl.BlockSpec((8, SSD_N), lambda i, g: (0, _blk(SSD_INNER, SSD_N) + SSD_GROUPS + g)),
                pl.BlockSpec((1, wx), lambda i, g: (0, g)),
                pl.BlockSpec((1, SSD_N), lambda i, g: (0, _blk(SSD_INNER, SSD_N) + g)),
                pl.BlockSpec((1, SSD_N), lambda i, g: (0, _blk(SSD_INNER, SSD_N) + SSD_GROUPS + g)),
                pl.BlockSpec((8, LANES), lambda i, g: (0, 0)),
                pl.BlockSpec((2, 1, LANES, wx), lambda i, g: (0, g, 0, 0)),
                pl.BlockSpec((1, wx), lambda i, g: (0, g)),
                pl.BlockSpec((1, wx), lambda i, g: (0, g))]
    del nxb
    args = [u, u, u, u, u, wl["ssd_cw"], wl["ssd_cw"], wl["ssd_cw"], wl["ssd_cb"], wl["ssd_cb"], wl["ssd_cb"],
            wl["ssd_gp"], wl["ssd_ex"], wl["ssd_dv"], wl["ssd_ng"]]
    if s0 is not None:
        in_specs.append(pl.BlockSpec((1, 1, 2, HEAD_PER_GROUP, SSD_P, SSD_N), lambda i, g: (i, layer, 0, g, 0, 0)))
        args.append(s0)
    out_specs = [pl.BlockSpec((t, wx), lambda i, g: (i, g))]
    out_shape = [jax.ShapeDtypeStruct((nseq * t, SSD_INNER), bf16)]
    if want_state:
        out_specs.append(pl.BlockSpec((1, 2, HEAD_PER_GROUP, SSD_P, SSD_N), lambda i, g: (i, 0, g, 0, 0)))
        out_shape.append(jax.ShapeDtypeStruct((nseq, 2, SSD_HEADS, SSD_P, SSD_N), f32))
    scratch = [pltpu.VMEM((t + 2 * CONV_PAD, wx), f32), pltpu.VMEM((t + 2 * CONV_PAD, SSD_N), f32),
               pltpu.VMEM((t, wx), f32), pltpu.VMEM((t, SSD_N), f32), pltpu.VMEM((t, SSD_N), f32),
               pltpu.VMEM((t, LANES), f32), pltpu.VMEM((t, LANES), f32),
               pltpu.VMEM((2, t, wx), f32),
               pltpu.VMEM((2, SSD_N, wx), f32),
               pltpu.VMEM((2, LANES, SSD_CHUNK), f32), pltpu.VMEM((2, LANES, SSD_CHUNK), f32)]
    kern = functools.partial(_ssd_kernel, t=t, has_s0=s0 is not None, want_state=want_state)
    return pl.pallas_call(
        kern,
        grid=(nseq, SSD_GROUPS),
        in_specs=in_specs,
        out_specs=out_specs,
        out_shape=out_shape,
        scratch_shapes=scratch,
        compiler_params=_cparams(("parallel", "parallel"), 40),
        name="ssd_" + ("lat" if s0 is not None else "ctx"),
    )(*args)


def _prep_weights(p):
    w_in = p["w_in"]
    o_g = MLA_Q_RANK + MLA_KV_RANK + MLA_ROPE
    o_s = o_g + 2 * GDN_HEADS * GDN_DK + 2 * GDN_HEADS * GDN_DV + 4 * GDN_HEADS
    n_qkv = 2 * GDN_HEADS * GDN_DK + GDN_HEADS * GDN_DV
    n_gz = GDN_HEADS * GDN_DV
    sl = lambda a, b: w_in[:, :, a:b]
    zeros = lambda n: jnp.zeros(w_in.shape[:2] + (n,), w_in.dtype)
    n_xbc = SSD_INNER + 2 * SSD_GROUPS * SSD_N
    parts = [sl(0, MLA_Q_RANK),
             sl(MLA_Q_RANK + MLA_KV_RANK, o_g),
             sl(o_g + n_qkv + n_gz, o_g + n_qkv + n_gz + 4 * GDN_HEADS),
             sl(o_s + SSD_INNER + n_xbc, o_s + SSD_INNER + n_xbc + 2 * SSD_HEADS),
             zeros(C_XBC - (C_SM + MLA_ROPE + 4 * GDN_HEADS + 2 * SSD_HEADS)),
             sl(o_s + SSD_INNER, o_s + SSD_INNER + n_xbc),
             sl(MLA_Q_RANK, MLA_Q_RANK + MLA_KV_RANK),
             sl(o_g + n_qkv, o_g + n_qkv + n_gz),
             sl(o_g, o_g + n_qkv),
             sl(o_s, o_s + SSD_INNER)]
    w_in_p = jnp.concatenate(parts, axis=-1).astype(bf16)
    assert w_in_p.shape[-1] == N_IN

    w_uq = p["mla_w_uq"].reshape(DEPTH, MLA_Q_RANK, MLA_HEADS, MLA_QK)
    w_uq = jnp.pad(w_uq, ((0, 0), (0, 0), (0, 0), (0, MLA_QK_PAD - MLA_QK)))
    w_uq = w_uq.reshape(DEPTH, MLA_Q_RANK, MLA_HEADS * MLA_QK_PAD).astype(bf16)
    w_ukv = p["mla_w_ukv"].reshape(DEPTH, MLA_KV_RANK, MLA_HEADS, MLA_NOPE + MLA_V)
    w_uk = w_ukv[..., :MLA_NOPE].reshape(DEPTH, MLA_KV_RANK, MLA_HEADS * MLA_NOPE).astype(bf16)
    w_uv = w_ukv[..., MLA_NOPE:].reshape(DEPTH, MLA_KV_RANK, MLA_HEADS * MLA_V).astype(bf16)
    pad_g = lambda g: jnp.pad(g, ((0, 0), (0, MLA_QK_PAD - MLA_QK)))[:, None, :]

    w_out = p["w_out"].astype(bf16)
    n_mla, n_gdn = MLA_HEADS * MLA_V, GDN_HEADS * GDN_DV

    tf = FFN_TF
    w_gu = p["ffn_w_gu"]
    wg = w_gu[:, :, :FF].reshape(DEPTH, D_MODEL, FF // tf, 1, tf)
    wu = w_gu[:, :, FF:].reshape(DEPTH, D_MODEL, FF // tf, 1, tf)
    w_gu_p = jnp.concatenate([wg, wu], axis=3).reshape(DEPTH, D_MODEL, 2 * FF).astype(bf16)

    def lane_rows(a_log, dt_bias, lane0):
        n = a_log.shape[1] * a_log.shape[2]
        rows = jnp.stack([a_log.reshape(DEPTH, n), dt_bias.reshape(DEPTH, n)], axis=1)
        return jnp.pad(rows.astype(f32), ((0, 0), (0, 6), (lane0, LANES - lane0 - n)))

    ex = np.zeros((2, SSD_GROUPS, LANES, HEAD_PER_GROUP * SSD_P), np.float32)
    for d in range(2):
        for g in range(SSD_GROUPS):
            for hh in range(HEAD_PER_GROUP):
                ex[d, g, SM_DT + d * SSD_HEADS + g * HEAD_PER_GROUP + hh, hh * SSD_P:(hh + 1) * SSD_P] = 1.0

    pad_rows = lambda w: jnp.pad(w.astype(f32), ((0, 0), (0, 8 - CONV_W), (0, 0)))
    return dict(
        w_in=w_in_p, w_uq=w_uq, w_uk=w_uk, w_uv=w_uv,
        qn_g=p["mla_qnorm_g"][:, None, :], kvn_g=p["mla_kvnorm_g"][:, None, :],
        q_g=pad_g(p["mla_q_g"]), k_g=pad_g(p["mla_k_g"]),
        w_o_mla=w_out[:, :n_mla], w_o_gdn=w_out[:, n_mla:n_mla + n_gdn], w_o_ssd=w_out[:, n_mla + n_gdn:],
        w_gu=w_gu_p, w_down=p["ffn_w_down"].astype(bf16),
        norm1_g=p["norm1_g"][:, None, :], norm2_g=p["norm2_g"][:, None, :],
        gdn_cw=pad_rows(p["gdn_conv_w"]), gdn_gp=lane_rows(p["gdn_a_log"], p["gdn_dt_bias"], SM_A),
        gdn_ng=p["gdn_norm_g"][:, None, :],
        ssd_cw=pad_rows(p["ssd_conv_w"]), ssd_cb=p["ssd_conv_b"][:, None, :],
        ssd_gp=lane_rows(p["ssd_a_log"], p["ssd_dt_bias"], SM_DT), ssd_ex=jnp.asarray(np.broadcast_to(ex, (DEPTH,) + ex.shape)),
        ssd_dv=jnp.repeat(p["ssd_d"], SSD_P, axis=1)[:, None, :], ssd_ng=p["ssd_norm_g"][:, None, :],
    )


def _rope_tables(n_tokens):
    rows = n_tokens // GRID_W
    row = jnp.repeat(jnp.arange(rows, dtype=f32), GRID_W)
    col = jnp.tile(jnp.arange(GRID_W, dtype=f32), rows)
    inv = ROPE_BASE ** (-jnp.arange(ROPE_F, dtype=f32) / ROPE_F)
    ar, ac = row[:, None] * inv, col[:, None] * inv
    zero = jnp.zeros_like(ar)
    tail = jnp.zeros((n_tokens, LANES - MLA_ROPE), f32)
    cos = jnp.concatenate([jnp.cos(ar), jnp.cos(ar), jnp.cos(ac), jnp.cos(ac), tail], axis=-1)
    s_up = jnp.concatenate([-jnp.sin(ar), zero, -jnp.sin(ac), zero, tail], axis=-1)
    s_dn = jnp.concatenate([zero, jnp.sin(ar), zero, jnp.sin(ac), tail], axis=-1)
    return jnp.stack([cos, s_up, s_dn], axis=0)


FFN_TF = 512
DENSE_TM = 1024
DOWN_TM = 512
DENSE_TN = 512


def _trunk_layer(x, mods, wl, l, *, nseq, t, latent, cache=None):
    u = _norm_mm(x, mods, wl["norm1_g"], wl["w_in"], shift_row=0, scale_row=1, swiglu=False,
                 tm=DENSE_TM, tn=DENSE_TN, out_dtype=f32, name="in_proj")
    if latent:
        o_mla = _mla_lat(u, cache["ckv"], cache["krope"], cache["rope"], wl, l)
        ckvn = None
        o_gdn, = _gdn(u, wl, nseq=nseq, t=t, hb=1, s0=cache["gdn"], layer=l, want_state=False)
        o_ssd, = _ssd(u, wl, nseq=nseq, t=t, s0=cache["ssd"], layer=l, want_state=False)
        s_gdn = s_ssd = None
    else:
        o_mla, ckvn = _mla_ctx(u, wl)
        o_gdn, s_gdn = _gdn(u, wl, nseq=nseq, t=t, hb=GDN_HEADS, s0=None, layer=l, want_state=True)
        o_ssd, s_ssd = _ssd(u, wl, nseq=nseq, t=t, s0=None, layer=l, want_state=True)
    x = _mm_res([o_mla, o_gdn, o_ssd], [wl["w_o_mla"], wl["w_o_gdn"], wl["w_o_ssd"]], x, mods,
                gate_row=2, tm=DENSE_TM, tn=DENSE_TN, name="out_proj")
    act = _norm_mm(x, mods, wl["norm2_g"], wl["w_gu"], shift_row=3, scale_row=4, swiglu=True,
                   tm=DENSE_TM, tn=FFN_TF, out_dtype=bf16, name="ffn_gu")
    x = _mm_res([act], [wl["w_down"]], x, mods, gate_row=5, tm=DOWN_TM, tn=DENSE_TN, name="ffn_down")
    return x, (u, ckvn, s_gdn, s_ssd)


def kernel(x_prompt, x_sample, cache_mla_ckv, cache_mla_krope, state_gdn, state_ssd, c, c_ctx, norm1_g, norm2_g, ada_w, ada_b, w_in, w_out, mla_qnorm_g, mla_w_uq, mla_kvnorm_g, mla_w_ukv, mla_q_g, mla_k_g, gdn_conv_w, gdn_a_log, gdn_dt_bias, gdn_norm_g, ssd_conv_w, ssd_conv_b, ssd_a_log, ssd_dt_bias, ssd_d, ssd_norm_g, ffn_w_gu, ffn_w_down):
    p = dict(norm1_g=norm1_g, norm2_g=norm2_g, w_in=w_in, w_out=w_out,
             mla_qnorm_g=mla_qnorm_g, mla_w_uq=mla_w_uq, mla_kvnorm_g=mla_kvnorm_g,
             mla_w_ukv=mla_w_ukv, mla_q_g=mla_q_g, mla_k_g=mla_k_g, gdn_conv_w=gdn_conv_w,
             gdn_a_log=gdn_a_log, gdn_dt_bias=gdn_dt_bias, gdn_norm_g=gdn_norm_g,
             ssd_conv_w=ssd_conv_w, ssd_conv_b=ssd_conv_b, ssd_a_log=ssd_a_log,
             ssd_dt_bias=ssd_dt_bias, ssd_d=ssd_d, ssd_norm_g=ssd_norm_g,
             ffn_w_gu=ffn_w_gu, ffn_w_down=ffn_w_down)
    w = _prep_weights(p)
    layer_w = lambda l: {k: v[l] for k, v in w.items()}

    cvec = jnp.concatenate([c_ctx[None, :], c, jnp.zeros((8 - 1 - DEC_BATCH, D_MODEL), f32)], axis=0)
    mods = _mods(cvec, ada_w, ada_b).reshape(DEPTH, 8, 6, D_MODEL)

    cache = dict(ckv=cache_mla_ckv,
                 krope=jnp.pad(cache_mla_krope, ((0, 0), (0, 0), (0, 0), (0, LANES - MLA_ROPE))),
                 rope=_rope_tables(DEC_SEQ), gdn=state_gdn, ssd=state_ssd)

    xp = x_prompt.reshape(BATCH * SEQ, D_MODEL)
    xs = x_sample.reshape(DEC_BATCH * DEC_SEQ, D_MODEL)
    ckvs, krs, sgs, sss = [], [], [], []
    for l in range(DEPTH):
        wl = layer_w(l)
        xp, (u, ckvn, sg, ss) = _trunk_layer(xp, mods[l, 0:1], wl, l, nseq=BATCH, t=SEQ, latent=False)
        ckvs.append(ckvn.reshape(BATCH, SEQ, MLA_KV_RANK))
        krs.append(u[:, C_SM:C_SM + MLA_ROPE].reshape(BATCH, SEQ, MLA_ROPE))
        sgs.append(sg)
        sss.append(ss)
        xs, _ = _trunk_layer(xs, mods[l, 1:1 + DEC_BATCH], wl, l, nseq=DEC_BATCH, t=DEC_SEQ, latent=True,
                             cache=cache)
    return (xp.reshape(BATCH, SEQ, D_MODEL), xs.reshape(DEC_BATCH, DEC_SEQ, D_MODEL),
            jnp.stack(ckvs, 1), jnp.stack(krs, 1), jnp.stack(sgs, 1), jnp.stack(sss, 1))
```

```python
import functools

import numpy as np
import jax
import jax.numpy as jnp
from jax import lax
from jax.experimental import pallas as pl
from jax.experimental.pallas import tpu as pltpu

f32 = jnp.float32
bf16 = jnp.bfloat16

D_MODEL = 2048
BATCH = 32
SEQ = 256
DEPTH = 4
DEC_BATCH = 2
DEC_SEQ = 1024
PAST_LEN = 256
GRID_W = 64
MLA_HEADS = 8
MLA_NOPE = 128
MLA_ROPE = 64
MLA_QK = MLA_NOPE + MLA_ROPE
MLA_V = 128
MLA_Q_RANK = 768
MLA_KV_RANK = 512
ROPE_F = MLA_ROPE // 4
ROPE_BASE = 10000.0
GDN_HEADS = 4
GDN_DK = 128
GDN_DV = 128
GDN_CHUNK = 64
SSD_HEADS = 8
SSD_P = 64
SSD_GROUPS = 2
SSD_N = 128
SSD_CHUNK = 128
SSD_INNER = SSD_HEADS * SSD_P
CONV_W = 5
FF = -(-8 * D_MODEL // (3 * 256)) * 256
EPS = 1e-6

LANES = 128
MLA_QK_PAD = 256
HEAD_PER_GROUP = SSD_HEADS // SSD_GROUPS

C_CQ = 0
C_SM = 768
C_XBC = 1024
C_CKV = 2048
C_GZ = 2560
C_GQKV = 3072
C_SZ = 4608
N_IN = 5120
SM_BETA = 64
SM_A = 72
SM_DT = 80

VMEM_MB = 1024 * 1024


def _cparams(sem, vmem_mb):
    return pltpu.CompilerParams(dimension_semantics=sem, vmem_limit_bytes=vmem_mb * VMEM_MB)


def _blk(off, width):
    assert off % width == 0
    return off // width


def _sigmoid(x):
    return 1.0 / (1.0 + jnp.exp(-x))


def _silu(x):
    return x * _sigmoid(x)


def _softplus(x):
    return jnp.maximum(x, 0.0) + jnp.log(1.0 + jnp.exp(-jnp.abs(x)))


def _rms_scale(x, n):
    return lax.rsqrt(jnp.sum(x * x, axis=-1, keepdims=True) / n + EPS)


def _lane_col(a, lane_idx):
    lane = lax.broadcasted_iota(jnp.int32, a.shape, 1)
    return jnp.sum(jnp.where(lane == lane_idx, a, 0.0), axis=1, keepdims=True)


def _dot(a, b):
    return jnp.dot(a, b, preferred_element_type=f32)


def _dot_nt(a, b):
    return lax.dot_general(a, b, (((1,), (1,)), ((), ())), preferred_element_type=f32)


def _dot_tn(a, b):
    return lax.dot_general(a, b, (((0,), (0,)), ((), ())), preferred_element_type=f32)


def _tri_cumsum(tri_bf, a):
    hi = a.astype(bf16)
    r1 = a - hi.astype(f32)
    mid = r1.astype(bf16)
    lo = (r1 - mid.astype(f32)).astype(bf16)
    r = _dot(tri_bf, jnp.concatenate([hi, mid, lo], axis=1))
    return (r[:, 2 * LANES:] + r[:, LANES:2 * LANES]) + r[:, :LANES]


def _select_dot(a, sel_bf):
    n = a.shape[0]
    hi = a.astype(bf16)
    lo = (a - hi.astype(f32)).astype(bf16)
    r = _dot(jnp.concatenate([hi, lo], axis=0), sel_bf)
    return r[:n] + r[n:]


def _mods_kernel(c_ref, w_ref, b_ref, o_ref):
    s = _silu(c_ref[...]).astype(bf16)
    o_ref[0] = _dot(s, w_ref[0].astype(bf16)) + b_ref[0]


def _mods(cvec8, ada_w, ada_b):
    tn = 1024
    n = 6 * D_MODEL
    return pl.pallas_call(
        _mods_kernel,
        grid=(DEPTH, n // tn),
        in_specs=[pl.BlockSpec((8, D_MODEL), lambda l, j: (0, 0)),
                  pl.BlockSpec((1, D_MODEL, tn), lambda l, j: (l, 0, j)),
                  pl.BlockSpec((1, 1, tn), lambda l, j: (l, 0, j))],
        out_specs=pl.BlockSpec((1, 8, tn), lambda l, j: (l, 0, j)),
        out_shape=jax.ShapeDtypeStruct((DEPTH, 8, n), f32),
        compiler_params=_cparams(("parallel", "parallel"), 40),
        name="adaln_mods",
    )(cvec8, ada_w, ada_b.reshape(DEPTH, 1, n))


NORM_ROWS = 256


def _norm_mm_kernel(x_ref, m_ref, g_ref, *rest, shift_row, scale_row, swiglu):
    w_refs, (o_ref, h_ref) = rest[:-2], rest[-2:]
    tm = x_ref.shape[0]

    @pl.when(pl.program_id(1) == 0)
    def _():
        shift = m_ref[shift_row:shift_row + 1, :]
        scale1p = 1.0 + m_ref[scale_row:scale_row + 1, :]
        g = g_ref[...]

        def body(r, carry):
            sl = pl.ds(pl.multiple_of(r * NORM_ROWS, NORM_ROWS), NORM_ROWS)
            x = x_ref[sl, :]
            y = x * lax.rsqrt(jnp.mean(x * x, axis=-1, keepdims=True) + EPS) * g
            h_ref[sl, :] = (y * scale1p + shift).astype(bf16)
            return carry

        lax.fori_loop(0, tm // NORM_ROWS, body, 0)

    h = h_ref[...]
    if swiglu:
        gate, up = _dot(h, w_refs[0][...]), _dot(h, w_refs[1][...])
        o_ref[...] = (_silu(gate) * up).astype(o_ref.dtype)
    else:
        o_ref[...] = _dot(h, w_refs[0][...]).astype(o_ref.dtype)


def _lspec(block, layer, idx):
    return pl.BlockSpec((None,) + tuple(block), lambda *g: (layer,) + tuple(idx(*g)))


def _mods_spec(width, layer, cond0, rows_per_cond, tm, col):
    return pl.BlockSpec((None, None, 6, width),
                        lambda i, j: (layer, cond0 + (i * tm) // rows_per_cond, 0, j if col else 0))


def _norm_mm(x, mods, g, w, *, layer, cond0, ncond, shift_row, scale_row, swiglu, tm, tn, out_dtype, name):
    m_rows, k = x.shape
    n_out = w.shape[2] // (2 if swiglu else 1)
    n_tiles = n_out // tn
    w_specs = [_lspec((k, tn), layer, lambda i, j: (0, j))]
    if swiglu:
        w_specs.append(_lspec((k, tn), layer, lambda i, j: (0, n_tiles + j)))
    kern = functools.partial(_norm_mm_kernel, shift_row=shift_row, scale_row=scale_row, swiglu=swiglu)
    return pl.pallas_call(
        kern,
        grid=(m_rows // tm, n_tiles),
        in_specs=[pl.BlockSpec((tm, k), lambda i, j: (i, 0)),
                  _mods_spec(k, layer, cond0, m_rows // ncond, tm, False),
                  _lspec((1, k), layer, lambda i, j: (0, 0))] + w_specs,
        out_specs=pl.BlockSpec((tm, tn), lambda i, j: (i, j)),
        out_shape=jax.ShapeDtypeStruct((m_rows, n_out), out_dtype),
        scratch_shapes=[pltpu.VMEM((tm, k), bf16)],
        compiler_params=_cparams(("parallel", "arbitrary"), 48),
        name=name,
    )(x, mods, g, *([w] * len(w_specs)))


def _mm_res_kernel(*refs, n_in, gate_row):
    a_refs, w_refs = refs[:n_in], refs[n_in:2 * n_in]
    x_ref, m_ref, o_ref = refs[2 * n_in:]
    acc = _dot(a_refs[0][...], w_refs[0][...])
    for t in range(1, n_in):
        acc = acc + _dot(a_refs[t][...], w_refs[t][...])
    o_ref[...] = x_ref[...] + m_ref[gate_row:gate_row + 1, :] * acc


def _mm_res(acts, w, x, mods, *, layer, cond0, ncond, gate_row, tm, tn, name):
    m_rows, n = x.shape
    n_in = len(acts)
    widths = [a.shape[1] for a in acts]
    offs = [sum(widths[:t]) for t in range(n_in)]
    in_specs = [pl.BlockSpec((tm, wd), lambda i, j: (i, 0)) for wd in widths]
    in_specs += [_lspec((wd, tn), layer, functools.partial(lambda i, j, rb: (rb, j), rb=_blk(off, wd)))
                 for wd, off in zip(widths, offs)]
    in_specs += [pl.BlockSpec((tm, tn), lambda i, j: (i, j)),
                 _mods_spec(tn, layer, cond0, m_rows // ncond, tm, True)]
    return pl.pallas_call(
        functools.partial(_mm_res_kernel, n_in=n_in, gate_row=gate_row),
        grid=(m_rows // tm, n // tn),
        in_specs=in_specs,
        out_specs=pl.BlockSpec((tm, tn), lambda i, j: (i, j)),
        out_shape=jax.ShapeDtypeStruct((m_rows, n), f32),
        compiler_params=_cparams(("parallel", "parallel"), 48),
        name=name,
    )(*acts, *([w] * n_in), x, mods)


def _rope(x, cos, s_up, s_dn):
    return x * cos + pltpu.roll(x, LANES - ROPE_F, 1) * s_up + pltpu.roll(x, ROPE_F, 1) * s_dn


def _mla_kv_rows(ckvn_bf, kr, wuk, wuv, kg, rope, k_scr, v_scr, r0):
    n = kr.shape[0]
    kn = _dot(ckvn_bf, wuk)
    v = _dot(ckvn_bf, wuv)
    krg = kr * kg[:, MLA_NOPE:]
    if rope is not None:
        krg = _rope(krg, *rope)
    kr_ss = jnp.sum(kr * kr, axis=-1, keepdims=True)
    for h in range(MLA_HEADS):
        knh = kn[:, h * MLA_NOPE:(h + 1) * MLA_NOPE]
        r = lax.rsqrt((jnp.sum(knh * knh, axis=-1, keepdims=True) + kr_ss) / MLA_QK + EPS)
        kh = jnp.concatenate([knh * r * kg[:, :MLA_NOPE], krg * r], axis=-1)
        k_scr[h, r0:r0 + n, :] = kh.astype(bf16)
        v_scr[h, r0:r0 + n, :] = v[:, h * MLA_V:(h + 1) * MLA_V].astype(bf16)


def _mla_attend(cq, wuq, qn_g, qg, rope, k_scr, v_scr, o_ref):
    cqn = (cq * _rms_scale(cq, MLA_Q_RANK) * qn_g).astype(bf16)
    q_all = _dot(cqn, wuq)
    scale = MLA_QK ** -0.5
    for h in range(MLA_HEADS):
        qh = q_all[:, h * MLA_QK_PAD:(h + 1) * MLA_QK_PAD]
        qh = qh * _rms_scale(qh, MLA_QK) * qg
        if rope is not None:
            qh = jnp.concatenate([qh[:, :MLA_NOPE], _rope(qh[:, MLA_NOPE:], *rope)], axis=-1)
        s = _dot_nt(qh.astype(bf16), k_scr[h]) * scale
        p = jnp.exp(s - jnp.max(s, axis=-1, keepdims=True))
        l = jnp.sum(p, axis=-1, keepdims=True)
        oh = _dot(p.astype(bf16), v_scr[h]) / l
        o_ref[:, h * MLA_V:(h + 1) * MLA_V] = oh.astype(o_ref.dtype)


def _krope_lanes(sm):
    lane = lax.broadcasted_iota(jnp.int32, sm.shape, 1)
    return jnp.where(lane < MLA_ROPE, sm, 0.0)


def _mla_ctx_kernel(cq_ref, ckv_ref, sm_ref, wuq_ref, wuk_ref, wuv_ref, qn_ref, kvn_ref, qg_ref, kg_ref,
                    o_ref, ckvn_ref, k_scr, v_scr):
    ckv = ckv_ref[...]
    ckvn = ckv * _rms_scale(ckv, MLA_KV_RANK) * kvn_ref[...]
    ckvn_ref[...] = ckvn
    _mla_kv_rows(ckvn.astype(bf16), _krope_lanes(sm_ref[...]), wuk_ref[...], wuv_ref[...], kg_ref[...],
                 None, k_scr, v_scr, 0)
    _mla_attend(cq_ref[...], wuq_ref[...], qn_ref[...], qg_ref[...], None, k_scr, v_scr, o_ref)


MLA_QB = 256


def _mla_lat_kernel(cq_ref, ckv_ref, sm_ref, cckv_ref, ckr_ref, rq_ref, rk_ref,
                    wuq_ref, wuk_ref, wuv_ref, qn_ref, kvn_ref, qg_ref, kg_ref, o_ref, k_scr, v_scr):
    @pl.when(pl.program_id(1) == 0)
    def _():
        kg = kg_ref[...]
        _mla_kv_rows(cckv_ref[0, 0].astype(bf16), ckr_ref[0, 0], wuk_ref[...], wuv_ref[...], kg,
                     None, k_scr, v_scr, 0)
        for c in range(DEC_SEQ // MLA_QB):
            rows = slice(c * MLA_QB, (c + 1) * MLA_QB)
            ckv = ckv_ref[rows, :]
            ckvn = ckv * _rms_scale(ckv, MLA_KV_RANK) * kvn_ref[...]
            rope = (rk_ref[0, rows, :], rk_ref[1, rows, :], rk_ref[2, rows, :])
            _mla_kv_rows(ckvn.astype(bf16), _krope_lanes(sm_ref[rows, :]), wuk_ref[...], wuv_ref[...], kg,
                         rope, k_scr, v_scr, PAST_LEN + c * MLA_QB)

    rope_q = (rq_ref[0], rq_ref[1], rq_ref[2])
    _mla_attend(cq_ref[...], wuq_ref[...], qn_ref[...], qg_ref[...], rope_q, k_scr, v_scr, o_ref)


def _mla_weight_specs(layer):
    zero = (lambda *a: (0, 0))
    return [_lspec((MLA_Q_RANK, MLA_HEADS * MLA_QK_PAD), layer, zero),
            _lspec((MLA_KV_RANK, MLA_HEADS * MLA_NOPE), layer, zero),
            _lspec((MLA_KV_RANK, MLA_HEADS * MLA_V), layer, zero),
            _lspec((1, MLA_Q_RANK), layer, zero),
            _lspec((1, MLA_KV_RANK), layer, zero),
            _lspec((1, MLA_QK_PAD), layer, zero),
            _lspec((1, MLA_QK_PAD), layer, zero)]


def _mla_weights(w):
    return [w["w_uq"], w["w_uk"], w["w_uv"], w["qn_g"], w["kvn_g"], w["q_g"], w["k_g"]]


def _mla_ctx(u, w, layer):
    t = SEQ
    return pl.pallas_call(
        _mla_ctx_kernel,
        grid=(BATCH,),
        in_specs=[pl.BlockSpec((t, MLA_Q_RANK), lambda i: (i, _blk(C_CQ, MLA_Q_RANK))),
                  pl.BlockSpec((t, MLA_KV_RANK), lambda i: (i, _blk(C_CKV, MLA_KV_RANK))),
                  pl.BlockSpec((t, LANES), lambda i: (i, _blk(C_SM, LANES)))] + _mla_weight_specs(layer),
        out_specs=[pl.BlockSpec((t, MLA_HEADS * MLA_V), lambda i: (i, 0)),
                   pl.BlockSpec((t, MLA_KV_RANK), lambda i: (i, 0))],
        out_shape=[jax.ShapeDtypeStruct((BATCH * t, MLA_HEADS * MLA_V), bf16),
                   jax.ShapeDtypeStruct((BATCH * t, MLA_KV_RANK), f32)],
        scratch_shapes=[pltpu.VMEM((MLA_HEADS, t, MLA_QK_PAD), bf16),
                        pltpu.VMEM((MLA_HEADS, t, MLA_V), bf16)],
        compiler_params=_cparams(("parallel",), 40),
        name="mla_ctx",
    )(u, u, u, *_mla_weights(w))


def _mla_lat(u, cache_ckv, cache_kr, rope_tab, w, layer):
    t = DEC_SEQ
    nq = t // MLA_QB
    tk = PAST_LEN + t
    return pl.pallas_call(
        _mla_lat_kernel,
        grid=(DEC_BATCH, nq),
        in_specs=[pl.BlockSpec((MLA_QB, MLA_Q_RANK), lambda s, q: (s * nq + q, _blk(C_CQ, MLA_Q_RANK))),
                  pl.BlockSpec((t, MLA_KV_RANK), lambda s, q: (s, _blk(C_CKV, MLA_KV_RANK))),
                  pl.BlockSpec((t, LANES), lambda s, q: (s, _blk(C_SM, LANES))),
                  pl.BlockSpec((1, 1, PAST_LEN, MLA_KV_RANK), lambda s, q: (s, layer, 0, 0)),
                  pl.BlockSpec((1, 1, PAST_LEN, LANES), lambda s, q: (s, layer, 0, 0)),
                  pl.BlockSpec((3, MLA_QB, LANES), lambda s, q: (0, q, 0)),
                  pl.BlockSpec((3, t, LANES), lambda s, q: (0, 0, 0))] + _mla_weight_specs(layer),
        out_specs=pl.BlockSpec((MLA_QB, MLA_HEADS * MLA_V), lambda s, q: (s * nq + q, 0)),
        out_shape=jax.ShapeDtypeStruct((DEC_BATCH * t, MLA_HEADS * MLA_V), bf16),
        scratch_shapes=[pltpu.VMEM((MLA_HEADS, tk, MLA_QK_PAD), bf16),
                        pltpu.VMEM((MLA_HEADS, tk, MLA_V), bf16)],
        compiler_params=_cparams(("parallel", "arbitrary"), 48),
        name="mla_lat",
    )(u, u, u, cache_ckv, cache_kr, rope_tab, rope_tab, *_mla_weights(w))


CONV_PAD = 8
CONV_ROWS = 256


def _conv_silu(x_ref, w_ref, b_ref, xp_ref, out_ref, t):
    c = x_ref.shape[1]
    xp_ref[0:CONV_PAD, :] = jnp.zeros((CONV_PAD, c), f32)
    xp_ref[CONV_PAD:CONV_PAD + t, :] = x_ref[...]
    xp_ref[CONV_PAD + t:2 * CONV_PAD + t, :] = jnp.zeros((CONV_PAD, c), f32)
    half = (CONV_W - 1) // 2
    for r0 in range(0, t, CONV_ROWS):
        acc = None
        for j in range(CONV_W):
            start = CONV_PAD - half + j + r0
            term = w_ref[j:j + 1, :] * xp_ref[start:start + CONV_ROWS, :]
            acc = term if acc is None else acc + term
        if b_ref is not None:
            acc = acc + b_ref[...]
        out_ref[r0:r0 + CONV_ROWS, :] = _silu(acc)


def _unit_tri_inverses(xs):
    n = xs[0].shape[0]
    shape = xs[0].shape
    ii = lax.broadcasted_iota(jnp.int32, shape, 0)
    jj = lax.broadcasted_iota(jnp.int32, shape, 1)
    first_half = jj < n
    eye = jnp.where((jj == ii) | (jj == ii + n), 1.0, 0.0)

    def hi_lo(a):
        hi_f = a.astype(bf16).astype(f32)
        return hi_f, a - hi_f

    def left(hi_f, lo_f):
        return jnp.where(first_half, hi_f, lo_f).astype(bf16)

    levels = n.bit_length() - 2
    ps = [eye + x for x in xs]
    for j in range(levels + 1):
        first, last = j == 0, j == levels
        x_parts = [hi_lo(x) for x in xs]
        rhs = [jnp.concatenate([h.astype(bf16)] * 2 + [l.astype(bf16)] * 2, axis=0) for h, l in x_parts]
        lhs = []
        for p, xp in zip(ps, x_parts):
            blocks = []
            if not first:
                blocks.append(left(*hi_lo(p)))
            if not last:
                blocks.append(left(*xp))
            rows = blocks[0] if len(blocks) == 1 else jnp.concatenate(blocks, axis=0)
            lhs.append(jnp.concatenate([rows, rows], axis=1))
        rs = [_dot(l, r) for l, r in zip(lhs, rhs)]
        if not first:
            ps = [p + r[:n] for p, r in zip(ps, rs)]
        if not last:
            xs = [r[-n:] for r in rs]
    return ps


def _gdn_kernel(*refs, t, hb, cpi, has_s0, want_state):
    it = iter(refs)
    q_ref, k_ref, v_ref, z_ref, sm_ref = (next(it) for _ in range(5))
    cwq_ref, cwk_ref, cwv_ref, gp_ref, ng_ref = (next(it) for _ in range(5))
    s0_ref = next(it) if has_s0 else None
    o_ref = next(it)
    sfin_ref = next(it) if want_state else None
    xp_s, qn_s, kn_s, vn_s, beta_s, cum_s, cumt_s, wq_s, uu_s, kt_s, qk_s, eg_s, od_s, st_s = it

    ck = GDN_CHUNK
    nc = t // ck
    head0 = pl.program_id(1) * hb

    _conv_silu(q_ref, cwq_ref, None, xp_s, qn_s, t)
    _conv_silu(k_ref, cwk_ref, None, xp_s, kn_s, t)
    _conv_silu(v_ref, cwv_ref, None, xp_s, vn_s, t)
    for hh in range(hb):
        cols = slice(hh * GDN_DK, (hh + 1) * GDN_DK)
        qh = qn_s[:, cols]
        qn_s[:, cols] = qh * lax.rsqrt(jnp.sum(qh * qh, axis=-1, keepdims=True) + EPS) * (GDN_DK ** -0.5)
        kh = kn_s[:, cols]
        kn_s[:, cols] = kh * lax.rsqrt(jnp.sum(kh * kh, axis=-1, keepdims=True) + EPS)

    sm = sm_ref[...]
    beta_s[...] = _sigmoid(sm)
    g_all = -jnp.exp(gp_ref[0:1, :]) * _softplus(sm + gp_ref[1:2, :])

    ii = lax.broadcasted_iota(jnp.int32, (ck, LANES), 0)
    jj = lax.broadcasted_iota(jnp.int32, (ck, LANES), 1)
    jj = jnp.where(jj < ck, jj, jj - ck)
    incl = (ii >= jj, ii <= jj)
    strict = (ii > jj, ii < jj)
    tri = (incl[0][:, :ck].astype(bf16), incl[1][:, :ck].astype(bf16))

    for c in range(nc):
        rows = slice(c * ck, (c + 1) * ck)
        for d in range(2):
            cum = _tri_cumsum(tri[d], g_all[rows, :])
            cum_s[d, rows, :] = cum
            cumt_s[d, c] = jnp.concatenate([cum, cum], axis=0).T

    if has_s0:
        for d in range(2):
            for hh in range(hb):
                st_s[d * hb + hh] = s0_ref[0, 0, d, hh]
    else:
        st_s[...] = jnp.zeros(st_s.shape, f32)

    def phase1(it_idx, carry):
        chunks = [it_idx * cpi + e for e in range(cpi)]
        rows = [pl.ds(pl.multiple_of(c * ck, ck), ck) for c in chunks]
        beta_c = [beta_s[r, :] for r in rows]
        heads = [(e, hh) for e in range(cpi) for hh in range(hb)]
        kc, qc, vc, g_kk, g_qk = {}, {}, {}, {}, {}
        for e, hh in heads:
            cols = slice(hh * GDN_DK, (hh + 1) * GDN_DK)
            kc[e, hh] = kn_s[rows[e], cols]
            qc[e, hh] = qn_s[rows[e], cols]
            vc[e, hh] = vn_s[rows[e], cols]
        for key in heads:
            kcb = kc[key].astype(bf16)
            kc_dup = jnp.concatenate([kcb, kcb], axis=0)
            g_kk[key] = _dot_nt(kcb, kc_dup)
            g_qk[key] = _dot_nt(qc[key].astype(bf16), kc_dup)
        probs = [(e, hh, d) for e, hh in heads for d in range(2)]
        col, bcol, dec, gl, neg_a = {}, {}, {}, {}, []
        for e, hh, d in probs:
            key = (e, hh, d)
            lane_g = SM_A + d * GDN_HEADS + head0 + hh
            lane_b = SM_BETA + d * GDN_HEADS + head0 + hh
            cum_c = cum_s[d, rows[e], :]
            col[key] = _lane_col(cum_c, lane_g)
            bcol[key] = _lane_col(beta_c[e], lane_b)
            row = cumt_s[d, chunks[e], pl.ds(lane_g, 1), :]
            dec[key] = jnp.where(incl[d], jnp.exp(col[key] - row), 0.0)
            end = cum_c[ck - 1:ck, :] if d == 0 else cum_c[0:1, :]
            gl[key] = _lane_col(end, lane_g)
            neg_a.append(jnp.where(strict[d], -(bcol[key] * g_kk[e, hh] * dec[key]), 0.0))
        tms = _unit_tri_inverses(neg_a)
        ecol, rhs = {}, []
        for key in probs:
            e, hh, d = key
            ecol[key] = jnp.exp(col[key])
            rhs.append(jnp.concatenate([kc[e, hh] * (bcol[key] * ecol[key]), vc[e, hh] * bcol[key]],
                                       axis=-1).astype(bf16))
        wus = [_dot(tm[:, :ck].astype(bf16), r) for tm, r in zip(tms, rhs)]
        for key, wu in zip(probs, wus):
            e, hh, d = key
            idx = d * hb + hh
            r2 = pl.multiple_of(chunks[e] * 2 * ck, 2 * ck)
            wq_s[idx, pl.ds(r2, ck), :] = wu[:, :GDN_DK].astype(bf16)
            wq_s[idx, pl.ds(r2 + ck, ck), :] = (qc[e, hh] * ecol[key]).astype(bf16)
            uu_s[idx, rows[e], :] = wu[:, GDN_DK:]
            kt_s[idx, rows[e], :] = (kc[e, hh] * jnp.exp(gl[key] - col[key])).astype(bf16)
            qk_s[idx, rows[e], :] = jnp.where(incl[d], g_qk[e, hh] * dec[key], 0.0)[:, :ck].astype(bf16)
            eg_s[idx, pl.ds(chunks[e], 1), :] = jnp.broadcast_to(jnp.exp(gl[key]), (1, LANES))
        return carry

    lax.fori_loop(0, nc // cpi, phase1, 0)

    def phase2(i, carry):
        probs = [(d, hh) for d in range(2) for hh in range(hb)]
        chunk = {0: i, 1: nc - 1 - i}
        rows = {d: pl.ds(pl.multiple_of(chunk[d] * ck, ck), ck) for d in range(2)}
        rows2 = {d: pl.ds(pl.multiple_of(chunk[d] * 2 * ck, 2 * ck), 2 * ck) for d in range(2)}
        s = [st_s[d * hb + hh] for d, hh in probs]
        ws = [_dot(wq_s[d * hb + hh, rows2[d], :], sv.astype(bf16)) for (d, hh), sv in zip(probs, s)]
        vb = [(uu_s[d * hb + hh, rows[d], :] - w[:ck]).astype(bf16) for (d, hh), w in zip(probs, ws)]
        o = [w[ck:] + _dot(qk_s[d * hb + hh, rows[d], :], v) for (d, hh), w, v in zip(probs, ws, vb)]
        ds = [_dot_tn(kt_s[d * hb + hh, rows[d], :], v) for (d, hh), v in zip(probs, vb)]
        for (d, hh), sv, dv, ov in zip(probs, s, ds, o):
            idx = d * hb + hh
            st_s[idx] = sv * eg_s[idx, pl.ds(chunk[d], 1), :] + dv
            od_s[d, rows[d], hh * GDN_DV:(hh + 1) * GDN_DV] = ov
        return carry

    lax.fori_loop(0, nc, phase2, 0)

    for hh in range(hb):
        cols = slice(hh * GDN_DV, (hh + 1) * GDN_DV)
        o = od_s[0, :, cols] + od_s[1, :, cols]
        on = o * lax.rsqrt(jnp.mean(o * o, axis=-1, keepdims=True) + EPS) * ng_ref[...]
        o_ref[:, cols] = (on * _silu(z_ref[:, cols])).astype(o_ref.dtype)
    if want_state:
        for d in range(2):
            for hh in range(hb):
                sfin_ref[0, d, hh] = st_s[d * hb + hh]


def _gdn(u, wl, *, nseq, t, hb, cpi, s0, layer, want_state):
    w = hb * GDN_DK
    nhb = GDN_HEADS // hb
    nc = t // GDN_CHUNK
    assert nc % cpi == 0
    gq, gk, gv = C_GQKV, C_GQKV + GDN_HEADS * GDN_DK, C_GQKV + 2 * GDN_HEADS * GDN_DK
    in_specs = [pl.BlockSpec((t, w), lambda i, j: (i, _blk(gq, w) + j)),
                pl.BlockSpec((t, w), lambda i, j: (i, _blk(gk, w) + j)),
                pl.BlockSpec((t, w), lambda i, j: (i, _blk(gv, w) + j)),
                pl.BlockSpec((t, w), lambda i, j: (i, _blk(C_GZ, w) + j)),
                pl.BlockSpec((t, LANES), lambda i, j: (i, _blk(C_SM, LANES))),
                _lspec((8, w), layer, lambda i, j: (0, j)),
                _lspec((8, w), layer, lambda i, j: (0, nhb + j)),
                _lspec((8, w), layer, lambda i, j: (0, 2 * nhb + j)),
                _lspec((8, LANES), layer, lambda i, j: (0, 0)),
                _lspec((1, GDN_DV), layer, lambda i, j: (0, 0))]
    args = [u, u, u, u, u, wl["gdn_cw"], wl["gdn_cw"], wl["gdn_cw"], wl["gdn_gp"], wl["gdn_ng"]]
    if s0 is not None:
        in_specs.append(pl.BlockSpec((1, 1, 2, hb, GDN_DK, GDN_DV), lambda i, j: (i, layer, 0, j, 0, 0)))
        args.append(s0)
    out_specs = [pl.BlockSpec((t, w), lambda i, j: (i, j))]
    out_shape = [jax.ShapeDtypeStruct((nseq * t, GDN_HEADS * GDN_DV), bf16)]
    if want_state:
        out_specs.append(pl.BlockSpec((1, 2, hb, GDN_DK, GDN_DV), lambda i, j: (i, 0, j, 0, 0)))
        out_shape.append(jax.ShapeDtypeStruct((nseq, 2, GDN_HEADS, GDN_DK, GDN_DV), f32))
    scratch = [pltpu.VMEM((t + 2 * CONV_PAD, w), f32),
               pltpu.VMEM((t, w), f32), pltpu.VMEM((t, w), f32), pltpu.VMEM((t, w), f32),
               pltpu.VMEM((t, LANES), f32),
               pltpu.VMEM((2, t, LANES), f32),
               pltpu.VMEM((2, nc, LANES, LANES), f32),
               pltpu.VMEM((2 * hb, 2 * t, GDN_DK), bf16),
               pltpu.VMEM((2 * hb, t, GDN_DV), f32),
               pltpu.VMEM((2 * hb, t, GDN_DK), bf16),
               pltpu.VMEM((2 * hb, t, GDN_CHUNK), bf16),
               pltpu.VMEM((2 * hb, max(nc, 8), LANES), f32),
               pltpu.VMEM((2, t, w), f32),
               pltpu.VMEM((2 * hb, GDN_DK, GDN_DV), f32)]
    kern = functools.partial(_gdn_kernel, t=t, hb=hb, cpi=cpi, has_s0=s0 is not None, want_state=want_state)
    return pl.pallas_call(
        kern,
        grid=(nseq, nhb),
        in_specs=in_specs,
        out_specs=out_specs,
        out_shape=out_shape,
        scratch_shapes=scratch,
        compiler_params=_cparams(("parallel", "parallel"), 48),
        name="gdn_" + ("lat" if s0 is not None else "ctx"),
    )(*args)


def _ssd_kernel(*refs, t, has_s0, want_state):
    it = iter(refs)
    x_ref, b_ref, c_ref, z_ref, sm_ref = (next(it) for _ in range(5))
    cwx_ref, cwb_ref, cwc_ref, cbx_ref, cbb_ref, cbc_ref = (next(it) for _ in range(6))
    gp_ref, ex_ref, dv_ref, ng_ref = (next(it) for _ in range(4))
    s0_ref = next(it) if has_s0 else None
    o_ref = next(it)
    sfin_ref = next(it) if want_state else None
    xpx_s, xpb_s, xs_s, bs_s, cs_s, dt_s, da_s, y_s, st_s, cumt_s, dtt_s = it

    ck = SSD_CHUNK
    nc = t // ck
    wx = HEAD_PER_GROUP * SSD_P
    group = pl.program_id(1)

    _conv_silu(x_ref, cwx_ref, cbx_ref, xpx_s, xs_s, t)
    _conv_silu(b_ref, cwb_ref, cbb_ref, xpb_s, bs_s, t)
    _conv_silu(c_ref, cwc_ref, cbc_ref, xpb_s, cs_s, t)

    dt_all = _softplus(sm_ref[...] + gp_ref[1:2, :])
    dt_s[...] = dt_all
    da_s[...] = dt_all * (-jnp.exp(gp_ref[0:1, :]))

    if has_s0:
        for d in range(2):
            s0 = jnp.concatenate([s0_ref[0, 0, d, hh] for hh in range(HEAD_PER_GROUP)], axis=0)
            st_s[d] = s0.T
    else:
        st_s[...] = jnp.zeros(st_s.shape, f32)

    ii = lax.broadcasted_iota(jnp.int32, (ck, ck), 0)
    jj = lax.broadcasted_iota(jnp.int32, (ck, ck), 1)
    incl = (ii >= jj, ii <= jj)
    tri = (incl[0].astype(bf16), incl[1].astype(bf16))
    lane = lax.broadcasted_iota(jnp.int32, (ck, LANES), 1)
    end_rows = 16

    def step(i, carry):
        dirs = (0, 1)
        chunk = (i, nc - 1 - i)
        rows = [pl.ds(pl.multiple_of(c * ck, ck), ck) for c in chunk]
        dt_c = [dt_s[rows[d], :] for d in dirs]
        cum = [_tri_cumsum(tri[d], da_s[rows[d], :]) for d in dirs]
        for d in dirs:
            cumt_s[d] = cum[d].T
            dtt_s[d] = dt_c[d].T
        cc = [cs_s[rows[d], :].astype(bf16) for d in dirs]
        bc = [bs_s[rows[d], :].astype(bf16) for d in dirs]
        xc = [xs_s[rows[d], :] for d in dirs]
        cb = [_dot_nt(cc[d], bc[d]) for d in dirs]
        heads = [(d, hh) for d in dirs for hh in range(HEAD_PER_GROUP)]
        ms = []
        for d, hh in heads:
            ln = SM_DT + d * SSD_HEADS + group * HEAD_PER_GROUP + hh
            col = _lane_col(cum[d], ln)
            row = cumt_s[d, pl.ds(ln, 1), :]
            dtrow = dtt_s[d, pl.ds(ln, 1), :]
            lm = jnp.where(incl[d], jnp.exp(col - row), 0.0)
            ms.append((cb[d] * lm * dtrow).astype(bf16))
        ys = [_dot(m, xc[d][:, (hh // 2) * LANES:(hh // 2 + 1) * LANES].astype(bf16))
              for (d, hh), m in zip(heads, ms)]
        end = [cum[0][ck - 1:ck, :], cum[1][0:1, :]]
        spread = [_select_dot(jnp.concatenate([jnp.exp(end[d] - cum[d]) * dt_c[d], jnp.exp(cum[d]),
                                               jnp.broadcast_to(jnp.exp(end[d]), (end_rows, LANES))], axis=0),
                              ex_ref[d, 0]) for d in dirs]
        st = [st_s[d] for d in dirs]
        st_c = [_dot_tn(bc[d], (xc[d] * spread[d][:ck]).astype(bf16)) for d in dirs]
        y_off = [_dot(cc[d], st[d].astype(bf16)) * spread[d][ck:2 * ck] for d in dirs]
        for d in dirs:
            st_s[d] = st[d] * spread[d][2 * ck:2 * ck + 1] + st_c[d]
            parts = [jnp.where(lane < SSD_P, ys[d * HEAD_PER_GROUP + 2 * pr], ys[d * HEAD_PER_GROUP + 2 * pr + 1])
                     for pr in range(HEAD_PER_GROUP // 2)]
            y_s[d, rows[d], :] = jnp.concatenate(parts, axis=-1) + y_off[d]
        return carry

    lax.fori_loop(0, nc, step, 0)

    y = y_s[0] + y_s[1] + xs_s[...] * dv_ref[...]
    y = y * _silu(z_ref[...])
    o_ref[...] = (y * lax.rsqrt(jnp.mean(y * y, axis=-1, keepdims=True) + EPS) * ng_ref[...]).astype(o_ref.dtype)
    if want_state:
        for d in range(2):
            stt = st_s[d].T
            for hh in range(HEAD_PER_GROUP):
                sfin_ref[0, d, hh] = stt[hh * SSD_P:(hh + 1) * SSD_P, :]


def _ssd(u, wl, *, nseq, t, s0, layer, want_state):
    wx = HEAD_PER_GROUP * SSD_P
    cx, cb, cc = C_XBC, C_XBC + SSD_INNER, C_XBC + SSD_INNER + SSD_GROUPS * SSD_N
    nxb = SSD_INNER // wx
    in_specs = [pl.BlockSpec((t, wx), lambda i, g: (i, _blk(cx, wx) + g)),
                pl.BlockSpec((t, SSD_N), lambda i, g: (i, _blk(cb, SSD_N) + g)),
                pl.BlockSpec((t, SSD_N), lambda i, g: (i, _blk(cc, SSD_N) + g)),
                pl.BlockSpec((t, wx), lambda i, g: (i, _blk(C_SZ, wx) + g)),
                pl.BlockSpec((t, LANES), lambda i, g: (i, _blk(C_SM, LANES))),
                _lspec((8, wx), layer, lambda i, g: (0, g)),
                _lspec((8, SSD_N), layer, lambda i, g: (0, _blk(SSD_INNER, SSD_N) + g)),
                _lspec((8, SSD_N), layer, lambda i, g: (0, _blk(SSD_INNER, SSD_N) + SSD_GROUPS + g)),
                _lspec((1, wx), layer, lambda i, g: (0, g)),
                _lspec((1, SSD_N), layer, lambda i, g: (0, _blk(SSD_INNER, SSD_N) + g)),
                _lspec((1, SSD_N), layer, lambda i, g: (0, _blk(SSD_INNER, SSD_N) + SSD_GROUPS + g)),
                _lspec((8, LANES), layer, lambda i, g: (0, 0)),
                pl.BlockSpec((2, 1, LANES, wx), lambda i, g: (0, g, 0, 0)),
                _lspec((1, wx), layer, lambda i, g: (0, g)),
                _lspec((1, wx), layer, lambda i, g: (0, g))]
    del nxb
    args = [u, u, u, u, u, wl["ssd_cw"], wl["ssd_cw"], wl["ssd_cw"], wl["ssd_cb"], wl["ssd_cb"], wl["ssd_cb"],
            wl["ssd_gp"], wl["ssd_ex"], wl["ssd_dv"], wl["ssd_ng"]]
    if s0 is not None:
        in_specs.append(pl.BlockSpec((1, 1, 2, HEAD_PER_GROUP, SSD_P, SSD_N), lambda i, g: (i, layer, 0, g, 0, 0)))
        args.append(s0)
    out_specs = [pl.BlockSpec((t, wx), lambda i, g: (i, g))]
    out_shape = [jax.ShapeDtypeStruct((nseq * t, SSD_INNER), bf16)]
    if want_state:
        out_specs.append(pl.BlockSpec((1, 2, HEAD_PER_GROUP, SSD_P, SSD_N), lambda i, g: (i, 0, g, 0, 0)))
        out_shape.append(jax.ShapeDtypeStruct((nseq, 2, SSD_HEADS, SSD_P, SSD_N), f32))
    scratch = [pltpu.VMEM((t + 2 * CONV_PAD, wx), f32), pltpu.VMEM((t + 2 * CONV_PAD, SSD_N), f32),
               pltpu.VMEM((t, wx), f32), pltpu.VMEM((t, SSD_N), f32), pltpu.VMEM((t, SSD_N), f32),
               pltpu.VMEM((t, LANES), f32), pltpu.VMEM((t, LANES), f32),
               pltpu.VMEM((2, t, wx), f32),
               pltpu.VMEM((2, SSD_N, wx), f32),
               pltpu.VMEM((2, LANES, SSD_CHUNK), f32), pltpu.VMEM((2, LANES, SSD_CHUNK), f32)]
    kern = functools.partial(_ssd_kernel, t=t, has_s0=s0 is not None, want_state=want_state)
    return pl.pallas_call(
        kern,
        grid=(nseq, SSD_GROUPS),
        in_specs=in_specs,
        out_specs=out_specs,
        out_shape=out_shape,
        scratch_shapes=scratch,
        compiler_params=_cparams(("parallel", "parallel"), 40),
        name="ssd_" + ("lat" if s0 is not None else "ctx"),
    )(*args)


def _prep_weights(p):
    w_in = p["w_in"]
    o_g = MLA_Q_RANK + MLA_KV_RANK + MLA_ROPE
    o_s = o_g + 2 * GDN_HEADS * GDN_DK + 2 * GDN_HEADS * GDN_DV + 4 * GDN_HEADS
    n_qkv = 2 * GDN_HEADS * GDN_DK + GDN_HEADS * GDN_DV
    n_gz = GDN_HEADS * GDN_DV
    sl = lambda a, b: w_in[:, :, a:b]
    zeros = lambda n: jnp.zeros(w_in.shape[:2] + (n,), w_in.dtype)
    n_xbc = SSD_INNER + 2 * SSD_GROUPS * SSD_N
    parts = [sl(0, MLA_Q_RANK),
             sl(MLA_Q_RANK + MLA_KV_RANK, o_g),
             sl(o_g + n_qkv + n_gz, o_g + n_qkv + n_gz + 4 * GDN_HEADS),
             sl(o_s + SSD_INNER + n_xbc, o_s + SSD_INNER + n_xbc + 2 * SSD_HEADS),
             zeros(C_XBC - (C_SM + MLA_ROPE + 4 * GDN_HEADS + 2 * SSD_HEADS)),
             sl(o_s + SSD_INNER, o_s + SSD_INNER + n_xbc),
             sl(MLA_Q_RANK, MLA_Q_RANK + MLA_KV_RANK),
             sl(o_g + n_qkv, o_g + n_qkv + n_gz),
             sl(o_g, o_g + n_qkv),
             sl(o_s, o_s + SSD_INNER)]
    w_in_p = jnp.concatenate(parts, axis=-1).astype(bf16)
    assert w_in_p.shape[-1] == N_IN

    w_uq = p["mla_w_uq"].reshape(DEPTH, MLA_Q_RANK, MLA_HEADS, MLA_QK)
    w_uq = jnp.pad(w_uq, ((0, 0), (0, 0), (0, 0), (0, MLA_QK_PAD - MLA_QK)))
    w_uq = w_uq.reshape(DEPTH, MLA_Q_RANK, MLA_HEADS * MLA_QK_PAD).astype(bf16)
    w_ukv = p["mla_w_ukv"].reshape(DEPTH, MLA_KV_RANK, MLA_HEADS, MLA_NOPE + MLA_V)
    w_uk = w_ukv[..., :MLA_NOPE].reshape(DEPTH, MLA_KV_RANK, MLA_HEADS * MLA_NOPE).astype(bf16)
    w_uv = w_ukv[..., MLA_NOPE:].reshape(DEPTH, MLA_KV_RANK, MLA_HEADS * MLA_V).astype(bf16)
    pad_g = lambda g: jnp.pad(g, ((0, 0), (0, MLA_QK_PAD - MLA_QK)))[:, None, :]

    def lane_rows(a_log, dt_bias, lane0):
        n = a_log.shape[1] * a_log.shape[2]
        rows = jnp.stack([a_log.reshape(DEPTH, n), dt_bias.reshape(DEPTH, n)], axis=1)
        return jnp.pad(rows.astype(f32), ((0, 0), (0, 6), (lane0, LANES - lane0 - n)))

    ex = np.zeros((2, SSD_GROUPS, LANES, HEAD_PER_GROUP * SSD_P), np.float32)
    for d in range(2):
        for g in range(SSD_GROUPS):
            for hh in range(HEAD_PER_GROUP):
                ex[d, g, SM_DT + d * SSD_HEADS + g * HEAD_PER_GROUP + hh, hh * SSD_P:(hh + 1) * SSD_P] = 1.0

    pad_rows = lambda w: jnp.pad(w.astype(f32), ((0, 0), (0, 8 - CONV_W), (0, 0)))
    return dict(
        w_in=w_in_p, w_uq=w_uq, w_uk=w_uk, w_uv=w_uv,
        qn_g=p["mla_qnorm_g"][:, None, :], kvn_g=p["mla_kvnorm_g"][:, None, :],
        q_g=pad_g(p["mla_q_g"]), k_g=pad_g(p["mla_k_g"]),
        w_out=p["w_out"].astype(bf16), w_gu=p["ffn_w_gu"].astype(bf16), w_down=p["ffn_w_down"].astype(bf16),
        norm1_g=p["norm1_g"][:, None, :], norm2_g=p["norm2_g"][:, None, :],
        gdn_cw=pad_rows(p["gdn_conv_w"]), gdn_gp=lane_rows(p["gdn_a_log"], p["gdn_dt_bias"], SM_A),
        gdn_ng=p["gdn_norm_g"][:, None, :],
        ssd_cw=pad_rows(p["ssd_conv_w"]), ssd_cb=p["ssd_conv_b"][:, None, :],
        ssd_gp=lane_rows(p["ssd_a_log"], p["ssd_dt_bias"], SM_DT), ssd_ex=jnp.asarray(ex, dtype=bf16),
        ssd_dv=jnp.repeat(p["ssd_d"], SSD_P, axis=1)[:, None, :], ssd_ng=p["ssd_norm_g"][:, None, :],
    )


def _rope_tables(n_tokens):
    rows = n_tokens // GRID_W
    row = jnp.repeat(jnp.arange(rows, dtype=f32), GRID_W)
    col = jnp.tile(jnp.arange(GRID_W, dtype=f32), rows)
    inv = ROPE_BASE ** (-jnp.arange(ROPE_F, dtype=f32) / ROPE_F)
    ar, ac = row[:, None] * inv, col[:, None] * inv
    zero = jnp.zeros_like(ar)
    tail = jnp.zeros((n_tokens, LANES - MLA_ROPE), f32)
    cos = jnp.concatenate([jnp.cos(ar), jnp.cos(ar), jnp.cos(ac), jnp.cos(ac), tail], axis=-1)
    s_up = jnp.concatenate([-jnp.sin(ar), zero, -jnp.sin(ac), zero, tail], axis=-1)
    s_dn = jnp.concatenate([zero, jnp.sin(ar), zero, jnp.sin(ac), tail], axis=-1)
    return jnp.stack([cos, s_up, s_dn], axis=0)


FFN_TF = 512
DENSE_TM = 1024
DOWN_TM = 512
DENSE_TN = 512


def _trunk_layer(x, mods, wl, l, *, nseq, t, latent, cache=None):
    cond = dict(layer=l, cond0=1 if latent else 0, ncond=nseq if latent else 1)
    u = _norm_mm(x, mods, wl["norm1_g"], wl["w_in"], shift_row=0, scale_row=1, swiglu=False,
                 tm=DENSE_TM, tn=DENSE_TN, out_dtype=f32, name="in_proj", **cond)
    if latent:
        o_mla = _mla_lat(u, cache["ckv"], cache["krope"], cache["rope"], wl, l)
        ckvn = None
        o_gdn, = _gdn(u, wl, nseq=nseq, t=t, hb=2, cpi=2, s0=cache["gdn"], layer=l, want_state=False)
        o_ssd, = _ssd(u, wl, nseq=nseq, t=t, s0=cache["ssd"], layer=l, want_state=False)
        s_gdn = s_ssd = None
    else:
        o_mla, ckvn = _mla_ctx(u, wl, l)
        o_gdn, s_gdn = _gdn(u, wl, nseq=nseq, t=t, hb=GDN_HEADS, cpi=1, s0=None, layer=l, want_state=True)
        o_ssd, s_ssd = _ssd(u, wl, nseq=nseq, t=t, s0=None, layer=l, want_state=True)
    x = _mm_res([o_mla, o_gdn, o_ssd], wl["w_out"], x, mods, gate_row=2, tm=DENSE_TM, tn=DENSE_TN,
                name="out_proj", **cond)
    act = _norm_mm(x, mods, wl["norm2_g"], wl["w_gu"], shift_row=3, scale_row=4, swiglu=True,
                   tm=DENSE_TM, tn=FFN_TF, out_dtype=bf16, name="ffn_gu", **cond)
    x = _mm_res([act], wl["w_down"], x, mods, gate_row=5, tm=DOWN_TM, tn=DENSE_TN, name="ffn_down", **cond)
    return x, (u, ckvn, s_gdn, s_ssd)


def kernel(x_prompt, x_sample, cache_mla_ckv, cache_mla_krope, state_gdn, state_ssd, c, c_ctx, norm1_g, norm2_g, ada_w, ada_b, w_in, w_out, mla_qnorm_g, mla_w_uq, mla_kvnorm_g, mla_w_ukv, mla_q_g, mla_k_g, gdn_conv_w, gdn_a_log, gdn_dt_bias, gdn_norm_g, ssd_conv_w, ssd_conv_b, ssd_a_log, ssd_dt_bias, ssd_d, ssd_norm_g, ffn_w_gu, ffn_w_down):
    p = dict(norm1_g=norm1_g, norm2_g=norm2_g, w_in=w_in, w_out=w_out,
             mla_qnorm_g=mla_qnorm_g, mla_w_uq=mla_w_uq, mla_kvnorm_g=mla_kvnorm_g,
             mla_w_ukv=mla_w_ukv, mla_q_g=mla_q_g, mla_k_g=mla_k_g, gdn_conv_w=gdn_conv_w,
             gdn_a_log=gdn_a_log, gdn_dt_bias=gdn_dt_bias, gdn_norm_g=gdn_norm_g,
             ssd_conv_w=ssd_conv_w, ssd_conv_b=ssd_conv_b, ssd_a_log=ssd_a_log,
             ssd_dt_bias=ssd_dt_bias, ssd_d=ssd_d, ssd_norm_g=ssd_norm_g,
             ffn_w_gu=ffn_w_gu, ffn_w_down=ffn_w_down)
    w = _prep_weights(p)

    cvec = jnp.concatenate([c_ctx[None, :], c, jnp.zeros((8 - 1 - DEC_BATCH, D_MODEL), f32)], axis=0)
    mods = _mods(cvec, ada_w, ada_b).reshape(DEPTH, 8, 6, D_MODEL)

    cache = dict(ckv=cache_mla_ckv,
                 krope=jnp.pad(cache_mla_krope, ((0, 0), (0, 0), (0, 0), (0, LANES - MLA_ROPE))),
                 rope=_rope_tables(DEC_SEQ), gdn=state_gdn, ssd=state_ssd)

    xp = x_prompt.reshape(BATCH * SEQ, D_MODEL)
    xs = x_sample.reshape(DEC_BATCH * DEC_SEQ, D_MODEL)
    ckvs, krs, sgs, sss = [], [], [], []
    for l in range(DEPTH):
        xp, (u, ckvn, sg, ss) = _trunk_layer(xp, mods, w, l, nseq=BATCH, t=SEQ, latent=False)
        ckvs.append(ckvn.reshape(BATCH, SEQ, MLA_KV_RANK))
        krs.append(u[:, C_SM:C_SM + MLA_ROPE].reshape(BATCH, SEQ, MLA_ROPE))
        sgs.append(sg)
        sss.append(ss)
        xs, _ = _trunk_layer(xs, mods, w, l, nseq=DEC_BATCH, t=DEC_SEQ, latent=True, cache=cache)
    return (xp.reshape(BATCH, SEQ, D_MODEL), xs.reshape(DEC_BATCH, DEC_SEQ, D_MODEL),
            jnp.stack(ckvs, 1), jnp.stack(krs, 1), jnp.stack(sgs, 1), jnp.stack(sss, 1))
```

```python
import functools

import numpy as np
import jax
import jax.numpy as jnp
from jax import lax
from jax.experimental import pallas as pl
from jax.experimental.pallas import tpu as pltpu

f32 = jnp.float32
bf16 = jnp.bfloat16

D_MODEL = 2048
BATCH = 32
SEQ = 256
DEPTH = 4
DEC_BATCH = 2
DEC_SEQ = 1024
PAST_LEN = 256
GRID_W = 64
MLA_HEADS = 8
MLA_NOPE = 128
MLA_ROPE = 64
MLA_QK = MLA_NOPE + MLA_ROPE
MLA_V = 128
MLA_Q_RANK = 768
MLA_KV_RANK = 512
ROPE_F = MLA_ROPE // 4
ROPE_BASE = 10000.0
GDN_HEADS = 4
GDN_DK = 128
GDN_DV = 128
GDN_CHUNK = 64
SSD_HEADS = 8
SSD_P = 64
SSD_GROUPS = 2
SSD_N = 128
SSD_CHUNK = 128
SSD_INNER = SSD_HEADS * SSD_P
CONV_W = 5
FF = -(-8 * D_MODEL // (3 * 256)) * 256
EPS = 1e-6

LANES = 128
MLA_QK_PAD = 256
HEAD_PER_GROUP = SSD_HEADS // SSD_GROUPS

C_CQ = 0
C_SM = 768
C_XBC = 1024
C_CKV = 2048
C_GZ = 2560
C_GQKV = 3072
C_SZ = 4608
N_IN = 5120
SM_BETA = 64
SM_A = 72
SM_DT = 80

VMEM_MB = 1024 * 1024


def _cparams(sem, vmem_mb):
    return pltpu.CompilerParams(dimension_semantics=sem, vmem_limit_bytes=vmem_mb * VMEM_MB)


def _blk(off, width):
    assert off % width == 0
    return off // width


def _sigmoid(x):
    return 1.0 / (1.0 + jnp.exp(-x))


def _silu(x):
    return x * _sigmoid(x)


def _softplus(x):
    return jnp.maximum(x, 0.0) + jnp.log(1.0 + jnp.exp(-jnp.abs(x)))


def _rms_scale(x, n):
    return lax.rsqrt(jnp.sum(x * x, axis=-1, keepdims=True) / n + EPS)


def _lane_col(a, lane_idx):
    lane = lax.broadcasted_iota(jnp.int32, a.shape, 1)
    return jnp.sum(jnp.where(lane == lane_idx, a, 0.0), axis=1, keepdims=True)


def _dot(a, b):
    return jnp.dot(a, b, preferred_element_type=f32)


def _dot_nt(a, b):
    return lax.dot_general(a, b, (((1,), (1,)), ((), ())), preferred_element_type=f32)


def _dot_tn(a, b):
    return lax.dot_general(a, b, (((0,), (0,)), ((), ())), preferred_element_type=f32)


def _tri_cumsum(tri_bf, a):
    hi = a.astype(bf16)
    r1 = a - hi.astype(f32)
    mid = r1.astype(bf16)
    lo = (r1 - mid.astype(f32)).astype(bf16)
    r = _dot(tri_bf, jnp.concatenate([hi, mid, lo], axis=1))
    return (r[:, 2 * LANES:] + r[:, LANES:2 * LANES]) + r[:, :LANES]


def _select_dot(a, sel_bf):
    n = a.shape[0]
    hi = a.astype(bf16)
    lo = (a - hi.astype(f32)).astype(bf16)
    r = _dot(jnp.concatenate([hi, lo], axis=0), sel_bf)
    return r[:n] + r[n:]


def _mods_kernel(c_ref, w_ref, b_ref, o_ref):
    s = _silu(c_ref[...]).astype(bf16)
    o_ref[0] = _dot(s, w_ref[0].astype(bf16)) + b_ref[0]


def _mods(cvec8, ada_w, ada_b):
    tn = 1024
    n = 6 * D_MODEL
    return pl.pallas_call(
        _mods_kernel,
        grid=(DEPTH, n // tn),
        in_specs=[pl.BlockSpec((8, D_MODEL), lambda l, j: (0, 0)),
                  pl.BlockSpec((1, D_MODEL, tn), lambda l, j: (l, 0, j)),
                  pl.BlockSpec((1, 1, tn), lambda l, j: (l, 0, j))],
        out_specs=pl.BlockSpec((1, 8, tn), lambda l, j: (l, 0, j)),
        out_shape=jax.ShapeDtypeStruct((DEPTH, 8, n), f32),
        compiler_params=_cparams(("parallel", "parallel"), 40),
        name="adaln_mods",
    )(cvec8, ada_w, ada_b.reshape(DEPTH, 1, n))


NORM_ROWS = 256


def _norm_mm_kernel(x_ref, m_ref, g_ref, *rest, shift_row, scale_row, swiglu):
    w_refs, (o_ref, h_ref) = rest[:-2], rest[-2:]
    tm = x_ref.shape[0]

    @pl.when(pl.program_id(1) == 0)
    def _():
        shift = m_ref[shift_row:shift_row + 1, :]
        scale1p = 1.0 + m_ref[scale_row:scale_row + 1, :]
        g = g_ref[...]

        def body(r, carry):
            sl = pl.ds(pl.multiple_of(r * NORM_ROWS, NORM_ROWS), NORM_ROWS)
            x = x_ref[sl, :]
            y = x * lax.rsqrt(jnp.mean(x * x, axis=-1, keepdims=True) + EPS) * g
            h_ref[sl, :] = (y * scale1p + shift).astype(bf16)
            return carry

        lax.fori_loop(0, tm // NORM_ROWS, body, 0)

    h = h_ref[...]
    if swiglu:
        gate, up = _dot(h, w_refs[0][...]), _dot(h, w_refs[1][...])
        o_ref[...] = (_silu(gate) * up).astype(o_ref.dtype)
    else:
        o_ref[...] = _dot(h, w_refs[0][...]).astype(o_ref.dtype)


def _lspec(block, layer, idx):
    return pl.BlockSpec((None,) + tuple(block), lambda *g: (layer,) + tuple(idx(*g)))


def _mods_spec(width, layer, cond0, rows_per_cond, tm, col):
    return pl.BlockSpec((None, None, 6, width),
                        lambda i, j: (layer, cond0 + (i * tm) // rows_per_cond, 0, j if col else 0))


def _norm_mm(x, mods, g, w, *, layer, cond0, ncond, shift_row, scale_row, swiglu, tm, tn, out_dtype, name):
    m_rows, k = x.shape
    n_out = w.shape[2] // (2 if swiglu else 1)
    n_tiles = n_out // tn
    w_specs = [_lspec((k, tn), layer, lambda i, j: (0, j))]
    if swiglu:
        w_specs.append(_lspec((k, tn), layer, lambda i, j: (0, n_tiles + j)))
    kern = functools.partial(_norm_mm_kernel, shift_row=shift_row, scale_row=scale_row, swiglu=swiglu)
    return pl.pallas_call(
        kern,
        grid=(m_rows // tm, n_tiles),
        in_specs=[pl.BlockSpec((tm, k), lambda i, j: (i, 0)),
                  _mods_spec(k, layer, cond0, m_rows // ncond, tm, False),
                  _lspec((1, k), layer, lambda i, j: (0, 0))] + w_specs,
        out_specs=pl.BlockSpec((tm, tn), lambda i, j: (i, j)),
        out_shape=jax.ShapeDtypeStruct((m_rows, n_out), out_dtype),
        scratch_shapes=[pltpu.VMEM((tm, k), bf16)],
        compiler_params=_cparams(("parallel", "arbitrary"), 48),
        name=name,
    )(x, mods, g, *([w] * len(w_specs)))


def _mm_res_kernel(*refs, n_in, gate_row):
    a_refs, w_refs = refs[:n_in], refs[n_in:2 * n_in]
    x_ref, m_ref, o_ref = refs[2 * n_in:]
    acc = _dot(a_refs[0][...], w_refs[0][...])
    for t in range(1, n_in):
        acc = acc + _dot(a_refs[t][...], w_refs[t][...])
    o_ref[...] = x_ref[...] + m_ref[gate_row:gate_row + 1, :] * acc


def _mm_res(acts, w, x, mods, *, layer, cond0, ncond, gate_row, tm, tn, vmem_mb, name):
    m_rows, n = x.shape
    n_in = len(acts)
    widths = [a.shape[1] for a in acts]
    offs = [sum(widths[:t]) for t in range(n_in)]
    in_specs = [pl.BlockSpec((tm, wd), lambda i, j: (i, 0)) for wd in widths]
    in_specs += [_lspec((wd, tn), layer, functools.partial(lambda i, j, rb: (rb, j), rb=_blk(off, wd)))
                 for wd, off in zip(widths, offs)]
    in_specs += [pl.BlockSpec((tm, tn), lambda i, j: (i, j)),
                 _mods_spec(tn, layer, cond0, m_rows // ncond, tm, True)]
    return pl.pallas_call(
        functools.partial(_mm_res_kernel, n_in=n_in, gate_row=gate_row),
        grid=(m_rows // tm, n // tn),
        in_specs=in_specs,
        out_specs=pl.BlockSpec((tm, tn), lambda i, j: (i, j)),
        out_shape=jax.ShapeDtypeStruct((m_rows, n), f32),
        compiler_params=_cparams(("parallel", "parallel"), vmem_mb),
        name=name,
    )(*acts, *([w] * n_in), x, mods)


def _rope(x, cos, s_up, s_dn):
    return x * cos + pltpu.roll(x, LANES - ROPE_F, 1) * s_up + pltpu.roll(x, ROPE_F, 1) * s_dn


def _mla_kv_rows(ckvn_bf, kr, wuk, wuv, kg, rope, k_scr, v_scr, r0):
    n = kr.shape[0]
    kn = _dot(ckvn_bf, wuk)
    v = _dot(ckvn_bf, wuv)
    krg = kr * kg[:, MLA_NOPE:]
    if rope is not None:
        krg = _rope(krg, *rope)
    kr_ss = jnp.sum(kr * kr, axis=-1, keepdims=True)
    for h in range(MLA_HEADS):
        knh = kn[:, h * MLA_NOPE:(h + 1) * MLA_NOPE]
        r = lax.rsqrt((jnp.sum(knh * knh, axis=-1, keepdims=True) + kr_ss) / MLA_QK + EPS)
        kh = jnp.concatenate([knh * r * kg[:, :MLA_NOPE], krg * r], axis=-1)
        k_scr[h, r0:r0 + n, :] = kh.astype(bf16)
        v_scr[h, r0:r0 + n, :] = v[:, h * MLA_V:(h + 1) * MLA_V].astype(bf16)


def _mla_attend(cq, wuq, qn_g, qg, rope, k_scr, v_scr, o_ref):
    cqn = (cq * _rms_scale(cq, MLA_Q_RANK) * qn_g).astype(bf16)
    q_all = _dot(cqn, wuq)
    scale = MLA_QK ** -0.5
    for h in range(MLA_HEADS):
        qh = q_all[:, h * MLA_QK_PAD:(h + 1) * MLA_QK_PAD]
        qh = qh * _rms_scale(qh, MLA_QK) * qg
        if rope is not None:
            qh = jnp.concatenate([qh[:, :MLA_NOPE], _rope(qh[:, MLA_NOPE:], *rope)], axis=-1)
        s = _dot_nt(qh.astype(bf16), k_scr[h]) * scale
        p = jnp.exp(s - jnp.max(s, axis=-1, keepdims=True))
        l = jnp.sum(p, axis=-1, keepdims=True)
        oh = _dot(p.astype(bf16), v_scr[h]) / l
        o_ref[:, h * MLA_V:(h + 1) * MLA_V] = oh.astype(o_ref.dtype)


def _krope_lanes(sm):
    lane = lax.broadcasted_iota(jnp.int32, sm.shape, 1)
    return jnp.where(lane < MLA_ROPE, sm, 0.0)


def _mla_ctx_kernel(cq_ref, ckv_ref, sm_ref, wuq_ref, wuk_ref, wuv_ref, qn_ref, kvn_ref, qg_ref, kg_ref,
                    *rest):
    o_ref, ckvn_ref, kr_ref, k_scr, v_scr = rest[-5:]
    ckv = ckv_ref[...]
    ckvn = ckv * _rms_scale(ckv, MLA_KV_RANK) * kvn_ref[...]
    ckvn_ref[...] = ckvn
    kr_ref[...] = sm_ref[:, 0:MLA_ROPE]
    _mla_kv_rows(ckvn.astype(bf16), _krope_lanes(sm_ref[...]), wuk_ref[...], wuv_ref[...], kg_ref[...],
                 None, k_scr, v_scr, 0)
    _mla_attend(cq_ref[...], wuq_ref[...], qn_ref[...], qg_ref[...], None, k_scr, v_scr, o_ref)


MLA_QB = 256


def _mla_lat_kernel(cq_ref, ckv_ref, sm_ref, cckv_ref, ckr_ref, rq_ref, rk_ref,
                    wuq_ref, wuk_ref, wuv_ref, qn_ref, kvn_ref, qg_ref, kg_ref, o_ref, k_scr, v_scr):
    @pl.when(pl.program_id(1) == 0)
    def _():
        kg = kg_ref[...]
        _mla_kv_rows(cckv_ref[0, 0].astype(bf16), ckr_ref[0, 0], wuk_ref[...], wuv_ref[...], kg,
                     None, k_scr, v_scr, 0)
        for c in range(DEC_SEQ // MLA_QB):
            rows = slice(c * MLA_QB, (c + 1) * MLA_QB)
            ckv = ckv_ref[rows, :]
            ckvn = ckv * _rms_scale(ckv, MLA_KV_RANK) * kvn_ref[...]
            rope = (rk_ref[0, rows, :], rk_ref[1, rows, :], rk_ref[2, rows, :])
            _mla_kv_rows(ckvn.astype(bf16), _krope_lanes(sm_ref[rows, :]), wuk_ref[...], wuv_ref[...], kg,
                         rope, k_scr, v_scr, PAST_LEN + c * MLA_QB)

    rope_q = (rq_ref[0], rq_ref[1], rq_ref[2])
    _mla_attend(cq_ref[...], wuq_ref[...], qn_ref[...], qg_ref[...], rope_q, k_scr, v_scr, o_ref)


def _mla_weight_specs(layer):
    zero = (lambda *a: (0, 0))
    return [_lspec((MLA_Q_RANK, MLA_HEADS * MLA_QK_PAD), layer, zero),
            _lspec((MLA_KV_RANK, MLA_HEADS * MLA_NOPE), layer, zero),
            _lspec((MLA_KV_RANK, MLA_HEADS * MLA_V), layer, zero),
            _lspec((1, MLA_Q_RANK), layer, zero),
            _lspec((1, MLA_KV_RANK), layer, zero),
            _lspec((1, MLA_QK_PAD), layer, zero),
            _lspec((1, MLA_QK_PAD), layer, zero)]


def _mla_weights(w):
    return [w["w_uq"], w["w_uk"], w["w_uv"], w["qn_g"], w["kvn_g"], w["q_g"], w["k_g"]]


def _stacked(prev, n_regular):
    prev = list(prev or [])
    specs = [pl.BlockSpec(memory_space=pl.ANY)] * len(prev)
    return prev, specs, {n_regular + t: 1 + t for t in range(len(prev))}


def _mla_ctx(u, w, layer, prev):
    t = SEQ
    in_specs = [pl.BlockSpec((t, MLA_Q_RANK), lambda i: (i, _blk(C_CQ, MLA_Q_RANK))),
                pl.BlockSpec((t, MLA_KV_RANK), lambda i: (i, _blk(C_CKV, MLA_KV_RANK))),
                pl.BlockSpec((t, LANES), lambda i: (i, _blk(C_SM, LANES)))] + _mla_weight_specs(layer)
    prev, alias_specs, aliases = _stacked(prev, len(in_specs))
    return pl.pallas_call(
        _mla_ctx_kernel,
        grid=(BATCH,),
        in_specs=in_specs + alias_specs,
        out_specs=[pl.BlockSpec((t, MLA_HEADS * MLA_V), lambda i: (i, 0)),
                   pl.BlockSpec((None, t, MLA_KV_RANK), lambda i: (i, layer, 0)),
                   pl.BlockSpec((None, t, MLA_ROPE), lambda i: (i, layer, 0))],
        out_shape=[jax.ShapeDtypeStruct((BATCH * t, MLA_HEADS * MLA_V), bf16),
                   jax.ShapeDtypeStruct((BATCH, DEPTH * t, MLA_KV_RANK), f32),
                   jax.ShapeDtypeStruct((BATCH, DEPTH * t, MLA_ROPE), f32)],
        input_output_aliases=aliases,
        scratch_shapes=[pltpu.VMEM((MLA_HEADS, t, MLA_QK_PAD), bf16),
                        pltpu.VMEM((MLA_HEADS, t, MLA_V), bf16)],
        compiler_params=_cparams(("parallel",), 40),
        name="mla_ctx",
    )(u, u, u, *_mla_weights(w), *prev)


def _mla_lat(u, cache_ckv, cache_kr, rope_tab, w, layer):
    t = DEC_SEQ
    nq = t // MLA_QB
    tk = PAST_LEN + t
    return pl.pallas_call(
        _mla_lat_kernel,
        grid=(DEC_BATCH, nq),
        in_specs=[pl.BlockSpec((MLA_QB, MLA_Q_RANK), lambda s, q: (s * nq + q, _blk(C_CQ, MLA_Q_RANK))),
                  pl.BlockSpec((t, MLA_KV_RANK), lambda s, q: (s, _blk(C_CKV, MLA_KV_RANK))),
                  pl.BlockSpec((t, LANES), lambda s, q: (s, _blk(C_SM, LANES))),
                  pl.BlockSpec((1, 1, PAST_LEN, MLA_KV_RANK), lambda s, q: (s, layer, 0, 0)),
                  pl.BlockSpec((1, 1, PAST_LEN, LANES), lambda s, q: (s, layer, 0, 0)),
                  pl.BlockSpec((3, MLA_QB, LANES), lambda s, q: (0, q, 0)),
                  pl.BlockSpec((3, t, LANES), lambda s, q: (0, 0, 0))] + _mla_weight_specs(layer),
        out_specs=pl.BlockSpec((MLA_QB, MLA_HEADS * MLA_V), lambda s, q: (s * nq + q, 0)),
        out_shape=jax.ShapeDtypeStruct((DEC_BATCH * t, MLA_HEADS * MLA_V), bf16),
        scratch_shapes=[pltpu.VMEM((MLA_HEADS, tk, MLA_QK_PAD), bf16),
                        pltpu.VMEM((MLA_HEADS, tk, MLA_V), bf16)],
        compiler_params=_cparams(("parallel", "arbitrary"), 48),
        name="mla_lat",
    )(u, u, u, cache_ckv, cache_kr, rope_tab, rope_tab, *_mla_weights(w))


CONV_PAD = 8
CONV_ROWS = 256


def _conv_silu(x_ref, w_ref, b_ref, xp_ref, out_ref, t):
    c = x_ref.shape[1]
    xp_ref[0:CONV_PAD, :] = jnp.zeros((CONV_PAD, c), f32)
    xp_ref[CONV_PAD:CONV_PAD + t, :] = x_ref[...]
    xp_ref[CONV_PAD + t:2 * CONV_PAD + t, :] = jnp.zeros((CONV_PAD, c), f32)
    half = (CONV_W - 1) // 2
    for r0 in range(0, t, CONV_ROWS):
        acc = None
        for j in range(CONV_W):
            start = CONV_PAD - half + j + r0
            term = w_ref[j:j + 1, :] * xp_ref[start:start + CONV_ROWS, :]
            acc = term if acc is None else acc + term
        if b_ref is not None:
            acc = acc + b_ref[...]
        out_ref[r0:r0 + CONV_ROWS, :] = _silu(acc)


def _unit_tri_inverses(xs):
    n = xs[0].shape[0]
    shape = xs[0].shape
    ii = lax.broadcasted_iota(jnp.int32, shape, 0)
    jj = lax.broadcasted_iota(jnp.int32, shape, 1)
    first_half = jj < n
    eye = jnp.where((jj == ii) | (jj == ii + n), 1.0, 0.0)

    def hi_lo(a):
        hi_f = a.astype(bf16).astype(f32)
        return hi_f, a - hi_f

    def left(hi_f, lo_f):
        return jnp.where(first_half, hi_f, lo_f).astype(bf16)

    levels = n.bit_length() - 2
    ps = [eye + x for x in xs]
    for j in range(levels + 1):
        first, last = j == 0, j == levels
        x_parts = [hi_lo(x) for x in xs]
        rhs = [jnp.concatenate([h.astype(bf16)] * 2 + [l.astype(bf16)] * 2, axis=0) for h, l in x_parts]
        lhs = []
        for p, xp in zip(ps, x_parts):
            blocks = []
            if not first:
                blocks.append(left(*hi_lo(p)))
            if not last:
                blocks.append(left(*xp))
            rows = blocks[0] if len(blocks) == 1 else jnp.concatenate(blocks, axis=0)
            lhs.append(jnp.concatenate([rows, rows], axis=1))
        rs = [_dot(l, r) for l, r in zip(lhs, rhs)]
        if not first:
            ps = [p + r[:n] for p, r in zip(ps, rs)]
        if not last:
            xs = [r[-n:] for r in rs]
    return ps


def _gdn_kernel(*refs, t, hb, cpi, has_s0, want_state, n_alias):
    it = iter(refs)
    q_ref, k_ref, v_ref, z_ref, sm_ref = (next(it) for _ in range(5))
    cwq_ref, cwk_ref, cwv_ref, gp_ref, ng_ref = (next(it) for _ in range(5))
    s0_ref = next(it) if has_s0 else None
    for _ in range(n_alias):
        next(it)
    o_ref = next(it)
    sfin_ref = next(it) if want_state else None
    xp_s, qn_s, kn_s, vn_s, beta_s, cum_s, cumt_s, wq_s, uu_s, kt_s, qk_s, eg_s, od_s, st_s = it

    ck = GDN_CHUNK
    nc = t // ck
    head0 = pl.program_id(1) * hb

    _conv_silu(q_ref, cwq_ref, None, xp_s, qn_s, t)
    _conv_silu(k_ref, cwk_ref, None, xp_s, kn_s, t)
    _conv_silu(v_ref, cwv_ref, None, xp_s, vn_s, t)
    for hh in range(hb):
        cols = slice(hh * GDN_DK, (hh + 1) * GDN_DK)
        qh = qn_s[:, cols]
        qn_s[:, cols] = qh * lax.rsqrt(jnp.sum(qh * qh, axis=-1, keepdims=True) + EPS) * (GDN_DK ** -0.5)
        kh = kn_s[:, cols]
        kn_s[:, cols] = kh * lax.rsqrt(jnp.sum(kh * kh, axis=-1, keepdims=True) + EPS)

    sm = sm_ref[...]
    beta_s[...] = _sigmoid(sm)
    g_all = -jnp.exp(gp_ref[0:1, :]) * _softplus(sm + gp_ref[1:2, :])

    ii = lax.broadcasted_iota(jnp.int32, (ck, LANES), 0)
    jj = lax.broadcasted_iota(jnp.int32, (ck, LANES), 1)
    jj = jnp.where(jj < ck, jj, jj - ck)
    incl = (ii >= jj, ii <= jj)
    strict = (ii > jj, ii < jj)
    tri = (incl[0][:, :ck].astype(bf16), incl[1][:, :ck].astype(bf16))

    for c in range(nc):
        rows = slice(c * ck, (c + 1) * ck)
        for d in range(2):
            cum = _tri_cumsum(tri[d], g_all[rows, :])
            cum_s[d, rows, :] = cum
            cumt_s[d, c] = jnp.concatenate([cum, cum], axis=0).T

    if has_s0:
        for d in range(2):
            for hh in range(hb):
                st_s[d * hb + hh] = s0_ref[0, 0, d, hh]
    else:
        st_s[...] = jnp.zeros(st_s.shape, f32)

    def phase1(it_idx, carry):
        chunks = [it_idx * cpi + e for e in range(cpi)]
        rows = [pl.ds(pl.multiple_of(c * ck, ck), ck) for c in chunks]
        beta_c = [beta_s[r, :] for r in rows]
        heads = [(e, hh) for e in range(cpi) for hh in range(hb)]
        kc, qc, vc, g_kk, g_qk = {}, {}, {}, {}, {}
        for e, hh in heads:
            cols = slice(hh * GDN_DK, (hh + 1) * GDN_DK)
            kc[e, hh] = kn_s[rows[e], cols]
            qc[e, hh] = qn_s[rows[e], cols]
            vc[e, hh] = vn_s[rows[e], cols]
        for key in heads:
            kcb = kc[key].astype(bf16)
            kc_dup = jnp.concatenate([kcb, kcb], axis=0)
            g_kk[key] = _dot_nt(kcb, kc_dup)
            g_qk[key] = _dot_nt(qc[key].astype(bf16), kc_dup)
        probs = [(e, hh, d) for e, hh in heads for d in range(2)]
        col, bcol, dec, gl, neg_a = {}, {}, {}, {}, []
        for e, hh, d in probs:
            key = (e, hh, d)
            lane_g = SM_A + d * GDN_HEADS + head0 + hh
            lane_b = SM_BETA + d * GDN_HEADS + head0 + hh
            cum_c = cum_s[d, rows[e], :]
            col[key] = _lane_col(cum_c, lane_g)
            bcol[key] = _lane_col(beta_c[e], lane_b)
            row = cumt_s[d, chunks[e], pl.ds(lane_g, 1), :]
            dec[key] = jnp.where(incl[d], jnp.exp(col[key] - row), 0.0)
            end = cum_c[ck - 1:ck, :] if d == 0 else cum_c[0:1, :]
            gl[key] = _lane_col(end, lane_g)
            neg_a.append(jnp.where(strict[d], -(bcol[key] * g_kk[e, hh] * dec[key]), 0.0))
        tms = _unit_tri_inverses(neg_a)
        ecol, rhs = {}, []
        for key in probs:
            e, hh, d = key
            ecol[key] = jnp.exp(col[key])
            rhs.append(jnp.concatenate([kc[e, hh] * (bcol[key] * ecol[key]), vc[e, hh] * bcol[key]],
                                       axis=-1).astype(bf16))
        wus = [_dot(tm[:, :ck].astype(bf16), r) for tm, r in zip(tms, rhs)]
        for key, wu in zip(probs, wus):
            e, hh, d = key
            idx = d * hb + hh
            r2 = pl.multiple_of(chunks[e] * 2 * ck, 2 * ck)
            wq_s[idx, pl.ds(r2, ck), :] = wu[:, :GDN_DK].astype(bf16)
            wq_s[idx, pl.ds(r2 + ck, ck), :] = (qc[e, hh] * ecol[key]).astype(bf16)
            uu_s[idx, rows[e], :] = wu[:, GDN_DK:]
            kt_s[idx, rows[e], :] = (kc[e, hh] * jnp.exp(gl[key] - col[key])).astype(bf16)
            qk_s[idx, rows[e], :] = jnp.where(incl[d], g_qk[e, hh] * dec[key], 0.0)[:, :ck].astype(bf16)
            eg_s[idx, pl.ds(chunks[e], 1), :] = jnp.broadcast_to(jnp.exp(gl[key]), (1, LANES))
        return carry

    lax.fori_loop(0, nc // cpi, phase1, 0)

    def phase2(i, carry):
        probs = [(d, hh) for d in range(2) for hh in range(hb)]
        chunk = {0: i, 1: nc - 1 - i}
        rows = {d: pl.ds(pl.multiple_of(chunk[d] * ck, ck), ck) for d in range(2)}
        rows2 = {d: pl.ds(pl.multiple_of(chunk[d] * 2 * ck, 2 * ck), 2 * ck) for d in range(2)}
        s = [st_s[d * hb + hh] for d, hh in probs]
        ws = [_dot(wq_s[d * hb + hh, rows2[d], :], sv.astype(bf16)) for (d, hh), sv in zip(probs, s)]
        vb = [(uu_s[d * hb + hh, rows[d], :] - w[:ck]).astype(bf16) for (d, hh), w in zip(probs, ws)]
        o = [w[ck:] + _dot(qk_s[d * hb + hh, rows[d], :], v) for (d, hh), w, v in zip(probs, ws, vb)]
        ds = [_dot_tn(kt_s[d * hb + hh, rows[d], :], v) for (d, hh), v in zip(probs, vb)]
        for (d, hh), sv, dv, ov in zip(probs, s, ds, o):
            idx = d * hb + hh
            st_s[idx] = sv * eg_s[idx, pl.ds(chunk[d], 1), :] + dv
            od_s[d, rows[d], hh * GDN_DV:(hh + 1) * GDN_DV] = ov
        return carry

    lax.fori_loop(0, nc, phase2, 0)

    for hh in range(hb):
        cols = slice(hh * GDN_DV, (hh + 1) * GDN_DV)
        o = od_s[0, :, cols] + od_s[1, :, cols]
        on = o * lax.rsqrt(jnp.mean(o * o, axis=-1, keepdims=True) + EPS) * ng_ref[...]
        o_ref[:, cols] = (on * _silu(z_ref[:, cols])).astype(o_ref.dtype)
    if want_state:
        for d in range(2):
            for hh in range(hb):
                sfin_ref[0, d, hh] = st_s[d * hb + hh]


def _gdn(u, wl, *, nseq, t, hb, cpi, s0, layer, want_state, prev=None):
    w = hb * GDN_DK
    nhb = GDN_HEADS // hb
    nc = t // GDN_CHUNK
    assert nc % cpi == 0
    gq, gk, gv = C_GQKV, C_GQKV + GDN_HEADS * GDN_DK, C_GQKV + 2 * GDN_HEADS * GDN_DK
    in_specs = [pl.BlockSpec((t, w), lambda i, j: (i, _blk(gq, w) + j)),
                pl.BlockSpec((t, w), lambda i, j: (i, _blk(gk, w) + j)),
                pl.BlockSpec((t, w), lambda i, j: (i, _blk(gv, w) + j)),
                pl.BlockSpec((t, w), lambda i, j: (i, _blk(C_GZ, w) + j)),
                pl.BlockSpec((t, LANES), lambda i, j: (i, _blk(C_SM, LANES))),
                _lspec((8, w), layer, lambda i, j: (0, j)),
                _lspec((8, w), layer, lambda i, j: (0, nhb + j)),
                _lspec((8, w), layer, lambda i, j: (0, 2 * nhb + j)),
                _lspec((8, LANES), layer, lambda i, j: (0, 0)),
                _lspec((1, GDN_DV), layer, lambda i, j: (0, 0))]
    args = [u, u, u, u, u, wl["gdn_cw"], wl["gdn_cw"], wl["gdn_cw"], wl["gdn_gp"], wl["gdn_ng"]]
    if s0 is not None:
        in_specs.append(pl.BlockSpec((1, 1, 2, hb, GDN_DK, GDN_DV), lambda i, j: (i, layer, 0, j, 0, 0)))
        args.append(s0)
    prev, alias_specs, aliases = _stacked(prev, len(in_specs))
    out_specs = [pl.BlockSpec((t, w), lambda i, j: (i, j))]
    out_shape = [jax.ShapeDtypeStruct((nseq * t, GDN_HEADS * GDN_DV), bf16)]
    if want_state:
        out_specs.append(pl.BlockSpec((1, None, 2, hb, GDN_DK, GDN_DV), lambda i, j: (i, layer, 0, j, 0, 0)))
        out_shape.append(jax.ShapeDtypeStruct((nseq, DEPTH, 2, GDN_HEADS, GDN_DK, GDN_DV), f32))
    scratch = [pltpu.VMEM((t + 2 * CONV_PAD, w), f32),
               pltpu.VMEM((t, w), f32), pltpu.VMEM((t, w), f32), pltpu.VMEM((t, w), f32),
               pltpu.VMEM((t, LANES), f32),
               pltpu.VMEM((2, t, LANES), f32),
               pltpu.VMEM((2, nc, LANES, LANES), f32),
               pltpu.VMEM((2 * hb, 2 * t, GDN_DK), bf16),
               pltpu.VMEM((2 * hb, t, GDN_DV), f32),
               pltpu.VMEM((2 * hb, t, GDN_DK), bf16),
               pltpu.VMEM((2 * hb, t, GDN_CHUNK), bf16),
               pltpu.VMEM((2 * hb, max(nc, 8), LANES), f32),
               pltpu.VMEM((2, t, w), f32),
               pltpu.VMEM((2 * hb, GDN_DK, GDN_DV), f32)]
    kern = functools.partial(_gdn_kernel, t=t, hb=hb, cpi=cpi, has_s0=s0 is not None, want_state=want_state,
                             n_alias=len(prev))
    return pl.pallas_call(
        kern,
        grid=(nseq, nhb),
        in_specs=in_specs + alias_specs,
        out_specs=out_specs,
        out_shape=out_shape,
        input_output_aliases=aliases,
        scratch_shapes=scratch,
        compiler_params=_cparams(("parallel", "parallel"), 48),
        name="gdn_" + ("lat" if s0 is not None else "ctx"),
    )(*args, *prev)


def _ssd_kernel(*refs, t, has_s0, want_state, n_alias):
    it = iter(refs)
    x_ref, b_ref, c_ref, z_ref, sm_ref = (next(it) for _ in range(5))
    cwx_ref, cwb_ref, cwc_ref, cbx_ref, cbb_ref, cbc_ref = (next(it) for _ in range(6))
    gp_ref, ex_ref, dv_ref, ng_ref = (next(it) for _ in range(4))
    s0_ref = next(it) if has_s0 else None
    for _ in range(n_alias):
        next(it)
    o_ref = next(it)
    sfin_ref = next(it) if want_state else None
    xpx_s, xpb_s, xs_s, bs_s, cs_s, dt_s, da_s, y_s, st_s, cumt_s, dtt_s = it

    ck = SSD_CHUNK
    nc = t // ck
    wx = HEAD_PER_GROUP * SSD_P
    group = pl.program_id(1)

    _conv_silu(x_ref, cwx_ref, cbx_ref, xpx_s, xs_s, t)
    _conv_silu(b_ref, cwb_ref, cbb_ref, xpb_s, bs_s, t)
    _conv_silu(c_ref, cwc_ref, cbc_ref, xpb_s, cs_s, t)

    dt_all = _softplus(sm_ref[...] + gp_ref[1:2, :])
    dt_s[...] = dt_all
    da_s[...] = dt_all * (-jnp.exp(gp_ref[0:1, :]))

    if has_s0:
        for d in range(2):
            s0 = jnp.concatenate([s0_ref[0, 0, d, hh] for hh in range(HEAD_PER_GROUP)], axis=0)
            st_s[d] = s0.T
    else:
        st_s[...] = jnp.zeros(st_s.shape, f32)

    ii = lax.broadcasted_iota(jnp.int32, (ck, ck), 0)
    jj = lax.broadcasted_iota(jnp.int32, (ck, ck), 1)
    incl = (ii >= jj, ii <= jj)
    tri = (incl[0].astype(bf16), incl[1].astype(bf16))
    lane = lax.broadcasted_iota(jnp.int32, (ck, LANES), 1)
    end_rows = 16

    def step(i, carry):
        dirs = (0, 1)
        chunk = (i, nc - 1 - i)
        rows = [pl.ds(pl.multiple_of(c * ck, ck), ck) for c in chunk]
        dt_c = [dt_s[rows[d], :] for d in dirs]
        cum = [_tri_cumsum(tri[d], da_s[rows[d], :]) for d in dirs]
        for d in dirs:
            cumt_s[d] = cum[d].T
            dtt_s[d] = dt_c[d].T
        cc = [cs_s[rows[d], :].astype(bf16) for d in dirs]
        bc = [bs_s[rows[d], :].astype(bf16) for d in dirs]
        xc = [xs_s[rows[d], :] for d in dirs]
        cb = [_dot_nt(cc[d], bc[d]) for d in dirs]
        heads = [(d, hh) for d in dirs for hh in range(HEAD_PER_GROUP)]
        ms = []
        for d, hh in heads:
            ln = SM_DT + d * SSD_HEADS + group * HEAD_PER_GROUP + hh
            col = _lane_col(cum[d], ln)
            row = cumt_s[d, pl.ds(ln, 1), :]
            dtrow = dtt_s[d, pl.ds(ln, 1), :]
            lm = jnp.where(incl[d], jnp.exp(col - row), 0.0)
            ms.append((cb[d] * lm * dtrow).astype(bf16))
        ys = [_dot(m, xc[d][:, (hh // 2) * LANES:(hh // 2 + 1) * LANES].astype(bf16))
              for (d, hh), m in zip(heads, ms)]
        end = [cum[0][ck - 1:ck, :], cum[1][0:1, :]]
        spread = [_select_dot(jnp.concatenate([jnp.exp(end[d] - cum[d]) * dt_c[d], jnp.exp(cum[d]),
                                               jnp.broadcast_to(jnp.exp(end[d]), (end_rows, LANES))], axis=0),
                              ex_ref[d, 0]) for d in dirs]
        st = [st_s[d] for d in dirs]
        st_c = [_dot_tn(bc[d], (xc[d] * spread[d][:ck]).astype(bf16)) for d in dirs]
        y_off = [_dot(cc[d], st[d].astype(bf16)) * spread[d][ck:2 * ck] for d in dirs]
        for d in dirs:
            st_s[d] = st[d] * spread[d][2 * ck:2 * ck + 1] + st_c[d]
            parts = [jnp.where(lane < SSD_P, ys[d * HEAD_PER_GROUP + 2 * pr], ys[d * HEAD_PER_GROUP + 2 * pr + 1])
                     for pr in range(HEAD_PER_GROUP // 2)]
            y_s[d, rows[d], :] = jnp.concatenate(parts, axis=-1) + y_off[d]
        return carry

    lax.fori_loop(0, nc, step, 0)

    y = y_s[0] + y_s[1] + xs_s[...] * dv_ref[...]
    y = y * _silu(z_ref[...])
    o_ref[...] = (y * lax.rsqrt(jnp.mean(y * y, axis=-1, keepdims=True) + EPS) * ng_ref[...]).astype(o_ref.dtype)
    if want_state:
        for d in range(2):
            stt = st_s[d].T
            for hh in range(HEAD_PER_GROUP):
                sfin_ref[0, d, hh] = stt[hh * SSD_P:(hh + 1) * SSD_P, :]


def _ssd(u, wl, *, nseq, t, s0, layer, want_state, prev=None):
    wx = HEAD_PER_GROUP * SSD_P
    cx, cb, cc = C_XBC, C_XBC + SSD_INNER, C_XBC + SSD_INNER + SSD_GROUPS * SSD_N
    nxb = SSD_INNER // wx
    in_specs = [pl.BlockSpec((t, wx), lambda i, g: (i, _blk(cx, wx) + g)),
                pl.BlockSpec((t, SSD_N), lambda i, g: (i, _blk(cb, SSD_N) + g)),
                pl.BlockSpec((t, SSD_N), lambda i, g: (i, _blk(cc, SSD_N) + g)),
                pl.BlockSpec((t, wx), lambda i, g: (i, _blk(C_SZ, wx) + g)),
                pl.BlockSpec((t, LANES), lambda i, g: (i, _blk(C_SM, LANES))),
                _lspec((8, wx), layer, lambda i, g: (0, g)),
                _lspec((8, SSD_N), layer, lambda i, g: (0, _blk(SSD_INNER, SSD_N) + g)),
                _lspec((8, SSD_N), layer, lambda i, g: (0, _blk(SSD_INNER, SSD_N) + SSD_GROUPS + g)),
                _lspec((1, wx), layer, lambda i, g: (0, g)),
                _lspec((1, SSD_N), layer, lambda i, g: (0, _blk(SSD_INNER, SSD_N) + g)),
                _lspec((1, SSD_N), layer, lambda i, g: (0, _blk(SSD_INNER, SSD_N) + SSD_GROUPS + g)),
                _lspec((8, LANES), layer, lambda i, g: (0, 0)),
                pl.BlockSpec((2, 1, LANES, wx), lambda i, g: (0, g, 0, 0)),
                _lspec((1, wx), layer, lambda i, g: (0, g)),
                _lspec((1, wx), layer, lambda i, g: (0, g))]
    del nxb
    args = [u, u, u, u, u, wl["ssd_cw"], wl["ssd_cw"], wl["ssd_cw"], wl["ssd_cb"], wl["ssd_cb"], wl["ssd_cb"],
            wl["ssd_gp"], wl["ssd_ex"], wl["ssd_dv"], wl["ssd_ng"]]
    if s0 is not None:
        in_specs.append(pl.BlockSpec((1, 1, 2, HEAD_PER_GROUP, SSD_P, SSD_N), lambda i, g: (i, layer, 0, g, 0, 0)))
        args.append(s0)
    prev, alias_specs, aliases = _stacked(prev, len(in_specs))
    out_specs = [pl.BlockSpec((t, wx), lambda i, g: (i, g))]
    out_shape = [jax.ShapeDtypeStruct((nseq * t, SSD_INNER), bf16)]
    if want_state:
        out_specs.append(pl.BlockSpec((1, None, 2, HEAD_PER_GROUP, SSD_P, SSD_N),
                                      lambda i, g: (i, layer, 0, g, 0, 0)))
        out_shape.append(jax.ShapeDtypeStruct((nseq, DEPTH, 2, SSD_HEADS, SSD_P, SSD_N), f32))
    scratch = [pltpu.VMEM((t + 2 * CONV_PAD, wx), f32), pltpu.VMEM((t + 2 * CONV_PAD, SSD_N), f32),
               pltpu.VMEM((t, wx), f32), pltpu.VMEM((t, SSD_N), f32), pltpu.VMEM((t, SSD_N), f32),
               pltpu.VMEM((t, LANES), f32), pltpu.VMEM((t, LANES), f32),
               pltpu.VMEM((2, t, wx), f32),
               pltpu.VMEM((2, SSD_N, wx), f32),
               pltpu.VMEM((2, LANES, SSD_CHUNK), f32), pltpu.VMEM((2, LANES, SSD_CHUNK), f32)]
    kern = functools.partial(_ssd_kernel, t=t, has_s0=s0 is not None, want_state=want_state, n_alias=len(prev))
    return pl.pallas_call(
        kern,
        grid=(nseq, SSD_GROUPS),
        in_specs=in_specs + alias_specs,
        out_specs=out_specs,
        out_shape=out_shape,
        input_output_aliases=aliases,
        scratch_shapes=scratch,
        compiler_params=_cparams(("parallel", "parallel"), 40),
        name="ssd_" + ("lat" if s0 is not None else "ctx"),
    )(*args, *prev)


def _prep_weights(p):
    w_in = p["w_in"]
    o_g = MLA_Q_RANK + MLA_KV_RANK + MLA_ROPE
    o_s = o_g + 2 * GDN_HEADS * GDN_DK + 2 * GDN_HEADS * GDN_DV + 4 * GDN_HEADS
    n_qkv = 2 * GDN_HEADS * GDN_DK + GDN_HEADS * GDN_DV
    n_gz = GDN_HEADS * GDN_DV
    sl = lambda a, b: w_in[:, :, a:b]
    zeros = lambda n: jnp.zeros(w_in.shape[:2] + (n,), w_in.dtype)
    n_xbc = SSD_INNER + 2 * SSD_GROUPS * SSD_N
    parts = [sl(0, MLA_Q_RANK),
             sl(MLA_Q_RANK + MLA_KV_RANK, o_g),
             sl(o_g + n_qkv + n_gz, o_g + n_qkv + n_gz + 4 * GDN_HEADS),
             sl(o_s + SSD_INNER + n_xbc, o_s + SSD_INNER + n_xbc + 2 * SSD_HEADS),
             zeros(C_XBC - (C_SM + MLA_ROPE + 4 * GDN_HEADS + 2 * SSD_HEADS)),
             sl(o_s + SSD_INNER, o_s + SSD_INNER + n_xbc),
             sl(MLA_Q_RANK, MLA_Q_RANK + MLA_KV_RANK),
             sl(o_g + n_qkv, o_g + n_qkv + n_gz),
             sl(o_g, o_g + n_qkv),
             sl(o_s, o_s + SSD_INNER)]
    w_in_p = jnp.concatenate(parts, axis=-1).astype(bf16)
    assert w_in_p.shape[-1] == N_IN

    w_uq = p["mla_w_uq"].reshape(DEPTH, MLA_Q_RANK, MLA_HEADS, MLA_QK)
    w_uq = jnp.pad(w_uq, ((0, 0), (0, 0), (0, 0), (0, MLA_QK_PAD - MLA_QK)))
    w_uq = w_uq.reshape(DEPTH, MLA_Q_RANK, MLA_HEADS * MLA_QK_PAD).astype(bf16)
    w_ukv = p["mla_w_ukv"].reshape(DEPTH, MLA_KV_RANK, MLA_HEADS, MLA_NOPE + MLA_V)
    w_uk = w_ukv[..., :MLA_NOPE].reshape(DEPTH, MLA_KV_RANK, MLA_HEADS * MLA_NOPE).astype(bf16)
    w_uv = w_ukv[..., MLA_NOPE:].reshape(DEPTH, MLA_KV_RANK, MLA_HEADS * MLA_V).astype(bf16)
    pad_g = lambda g: jnp.pad(g, ((0, 0), (0, MLA_QK_PAD - MLA_QK)))[:, None, :]

    def lane_rows(a_log, dt_bias, lane0):
        n = a_log.shape[1] * a_log.shape[2]
        rows = jnp.stack([a_log.reshape(DEPTH, n), dt_bias.reshape(DEPTH, n)], axis=1)
        return jnp.pad(rows.astype(f32), ((0, 0), (0, 6), (lane0, LANES - lane0 - n)))

    ex = np.zeros((2, SSD_GROUPS, LANES, HEAD_PER_GROUP * SSD_P), np.float32)
    for d in range(2):
        for g in range(SSD_GROUPS):
            for hh in range(HEAD_PER_GROUP):
                ex[d, g, SM_DT + d * SSD_HEADS + g * HEAD_PER_GROUP + hh, hh * SSD_P:(hh + 1) * SSD_P] = 1.0

    pad_rows = lambda w: jnp.pad(w.astype(f32), ((0, 0), (0, 8 - CONV_W), (0, 0)))
    return dict(
        w_in=w_in_p, w_uq=w_uq, w_uk=w_uk, w_uv=w_uv,
        qn_g=p["mla_qnorm_g"][:, None, :], kvn_g=p["mla_kvnorm_g"][:, None, :],
        q_g=pad_g(p["mla_q_g"]), k_g=pad_g(p["mla_k_g"]),
        w_out=p["w_out"].astype(bf16), w_gu=p["ffn_w_gu"].astype(bf16), w_down=p["ffn_w_down"].astype(bf16),
        norm1_g=p["norm1_g"][:, None, :], norm2_g=p["norm2_g"][:, None, :],
        gdn_cw=pad_rows(p["gdn_conv_w"]), gdn_gp=lane_rows(p["gdn_a_log"], p["gdn_dt_bias"], SM_A),
        gdn_ng=p["gdn_norm_g"][:, None, :],
        ssd_cw=pad_rows(p["ssd_conv_w"]), ssd_cb=p["ssd_conv_b"][:, None, :],
        ssd_gp=lane_rows(p["ssd_a_log"], p["ssd_dt_bias"], SM_DT), ssd_ex=jnp.asarray(ex, dtype=bf16),
        ssd_dv=jnp.repeat(p["ssd_d"], SSD_P, axis=1)[:, None, :], ssd_ng=p["ssd_norm_g"][:, None, :],
    )


def _rope_tables(n_tokens):
    rows = n_tokens // GRID_W
    row = jnp.repeat(jnp.arange(rows, dtype=f32), GRID_W)
    col = jnp.tile(jnp.arange(GRID_W, dtype=f32), rows)
    inv = ROPE_BASE ** (-jnp.arange(ROPE_F, dtype=f32) / ROPE_F)
    ar, ac = row[:, None] * inv, col[:, None] * inv
    zero = jnp.zeros_like(ar)
    tail = jnp.zeros((n_tokens, LANES - MLA_ROPE), f32)
    cos = jnp.concatenate([jnp.cos(ar), jnp.cos(ar), jnp.cos(ac), jnp.cos(ac), tail], axis=-1)
    s_up = jnp.concatenate([-jnp.sin(ar), zero, -jnp.sin(ac), zero, tail], axis=-1)
    s_dn = jnp.concatenate([zero, jnp.sin(ar), zero, jnp.sin(ac), tail], axis=-1)
    return jnp.stack([cos, s_up, s_dn], axis=0)


FFN_TF = 512
DENSE_TM = 1024
IN_TN = 1024
OUT_TM = 512
DOWN_TN = 512


def _trunk_layer(x, mods, wl, l, *, nseq, t, latent, cache=None, stacked=None):
    cond = dict(layer=l, cond0=1 if latent else 0, ncond=nseq if latent else 1)
    u = _norm_mm(x, mods, wl["norm1_g"], wl["w_in"], shift_row=0, scale_row=1, swiglu=False,
                 tm=DENSE_TM, tn=IN_TN, out_dtype=f32, name="in_proj", **cond)
    if latent:
        o_mla = _mla_lat(u, cache["ckv"], cache["krope"], cache["rope"], wl, l)
        o_gdn, = _gdn(u, wl, nseq=nseq, t=t, hb=2, cpi=2, s0=cache["gdn"], layer=l, want_state=False)
        o_ssd, = _ssd(u, wl, nseq=nseq, t=t, s0=cache["ssd"], layer=l, want_state=False)
    else:
        prev = (lambda a, b: None) if stacked is None else (lambda a, b: stacked[a:b])
        o_mla, ckv_all, kr_all = _mla_ctx(u, wl, l, prev(0, 2))
        o_gdn, sg_all = _gdn(u, wl, nseq=nseq, t=t, hb=GDN_HEADS, cpi=2, s0=None, layer=l, want_state=True,
                             prev=prev(2, 3))
        o_ssd, ss_all = _ssd(u, wl, nseq=nseq, t=t, s0=None, layer=l, want_state=True, prev=prev(3, 4))
        stacked = (ckv_all, kr_all, sg_all, ss_all)
    x = _mm_res([o_mla, o_gdn, o_ssd], wl["w_out"], x, mods, gate_row=2, tm=OUT_TM, tn=D_MODEL,
                vmem_mb=48, name="out_proj", **cond)
    act = _norm_mm(x, mods, wl["norm2_g"], wl["w_gu"], shift_row=3, scale_row=4, swiglu=True,
                   tm=DENSE_TM, tn=FFN_TF, out_dtype=bf16, name="ffn_gu", **cond)
    x = _mm_res([act], wl["w_down"], x, mods, gate_row=5, tm=DENSE_TM, tn=DOWN_TN, vmem_mb=56,
                name="ffn_down", **cond)
    return x, stacked


def kernel(x_prompt, x_sample, cache_mla_ckv, cache_mla_krope, state_gdn, state_ssd, c, c_ctx, norm1_g, norm2_g, ada_w, ada_b, w_in, w_out, mla_qnorm_g, mla_w_uq, mla_kvnorm_g, mla_w_ukv, mla_q_g, mla_k_g, gdn_conv_w, gdn_a_log, gdn_dt_bias, gdn_norm_g, ssd_conv_w, ssd_conv_b, ssd_a_log, ssd_dt_bias, ssd_d, ssd_norm_g, ffn_w_gu, ffn_w_down):
    p = dict(norm1_g=norm1_g, norm2_g=norm2_g, w_in=w_in, w_out=w_out,
             mla_qnorm_g=mla_qnorm_g, mla_w_uq=mla_w_uq, mla_kvnorm_g=mla_kvnorm_g,
             mla_w_ukv=mla_w_ukv, mla_q_g=mla_q_g, mla_k_g=mla_k_g, gdn_conv_w=gdn_conv_w,
             gdn_a_log=gdn_a_log, gdn_dt_bias=gdn_dt_bias, gdn_norm_g=gdn_norm_g,
             ssd_conv_w=ssd_conv_w, ssd_conv_b=ssd_conv_b, ssd_a_log=ssd_a_log,
             ssd_dt_bias=ssd_dt_bias, ssd_d=ssd_d, ssd_norm_g=ssd_norm_g,
             ffn_w_gu=ffn_w_gu, ffn_w_down=ffn_w_down)
    w = _prep_weights(p)

    cvec = jnp.concatenate([c_ctx[None, :], c, jnp.zeros((8 - 1 - DEC_BATCH, D_MODEL), f32)], axis=0)
    mods = _mods(cvec, ada_w, ada_b).reshape(DEPTH, 8, 6, D_MODEL)

    cache = dict(ckv=cache_mla_ckv,
                 krope=jnp.pad(cache_mla_krope, ((0, 0), (0, 0), (0, 0), (0, LANES - MLA_ROPE))),
                 rope=_rope_tables(DEC_SEQ), gdn=state_gdn, ssd=state_ssd)

    xp = x_prompt.reshape(BATCH * SEQ, D_MODEL)
    xs = x_sample.reshape(DEC_BATCH * DEC_SEQ, D_MODEL)
    stacked = None
    for l in range(DEPTH):
        xp, stacked = _trunk_layer(xp, mods, w, l, nseq=BATCH, t=SEQ, latent=False, stacked=stacked)
        xs, _ = _trunk_layer(xs, mods, w, l, nseq=DEC_BATCH, t=DEC_SEQ, latent=True, cache=cache)
    ckv_all, kr_all, sg_all, ss_all = stacked
    return (xp.reshape(BATCH, SEQ, D_MODEL), xs.reshape(DEC_BATCH, DEC_SEQ, D_MODEL),
            ckv_all.reshape(BATCH, DEPTH, SEQ, MLA_KV_RANK), kr_all.reshape(BATCH, DEPTH, SEQ, MLA_ROPE),
            sg_all, ss_all)
```

```python
import functools

import numpy as np
import jax
import jax.numpy as jnp
from jax import lax
from jax.experimental import pallas as pl
from jax.experimental.pallas import tpu as pltpu

f32 = jnp.float32
bf16 = jnp.bfloat16

D_MODEL = 2048
BATCH = 32
SEQ = 256
DEPTH = 4
DEC_BATCH = 2
DEC_SEQ = 1024
PAST_LEN = 256
GRID_W = 64
MLA_HEADS = 8
MLA_NOPE = 128
MLA_ROPE = 64
MLA_QK = MLA_NOPE + MLA_ROPE
MLA_V = 128
MLA_Q_RANK = 768
MLA_KV_RANK = 512
ROPE_F = MLA_ROPE // 4
ROPE_BASE = 10000.0
GDN_HEADS = 4
GDN_DK = 128
GDN_DV = 128
GDN_CHUNK = 64
SSD_HEADS = 8
SSD_P = 64
SSD_GROUPS = 2
SSD_N = 128
SSD_CHUNK = 128
SSD_INNER = SSD_HEADS * SSD_P
CONV_W = 5
FF = -(-8 * D_MODEL // (3 * 256)) * 256
EPS = 1e-6

LANES = 128
MLA_QK_PAD = 256
HEAD_PER_GROUP = SSD_HEADS // SSD_GROUPS

C_CQ = 0
C_SM = 768
C_XBC = 1024
C_CKV = 2048
C_GZ = 2560
C_GQKV = 3072
C_SZ = 4608
N_IN = 5120
SM_BETA = 64
SM_A = 72
SM_DT = 80

VMEM_MB = 1024 * 1024


def _cparams(sem, vmem_mb):
    return pltpu.CompilerParams(dimension_semantics=sem, vmem_limit_bytes=vmem_mb * VMEM_MB)


def _blk(off, width):
    assert off % width == 0
    return off // width


def _sigmoid(x):
    return 1.0 / (1.0 + jnp.exp(-x))


def _silu(x):
    return x * _sigmoid(x)


def _softplus(x):
    return jnp.maximum(x, 0.0) + jnp.log(1.0 + jnp.exp(-jnp.abs(x)))


def _rms_scale(x, n):
    return lax.rsqrt(jnp.sum(x * x, axis=-1, keepdims=True) / n + EPS)


def _lane_col(a, lane_idx):
    lane = lax.broadcasted_iota(jnp.int32, a.shape, 1)
    return jnp.sum(jnp.where(lane == lane_idx, a, 0.0), axis=1, keepdims=True)


def _dot(a, b):
    return jnp.dot(a, b, preferred_element_type=f32)


def _dot_nt(a, b):
    return lax.dot_general(a, b, (((1,), (1,)), ((), ())), preferred_element_type=f32)


def _dot_tn(a, b):
    return lax.dot_general(a, b, (((0,), (0,)), ((), ())), preferred_element_type=f32)


def _tri_cumsum(tri_bf, a):
    hi = a.astype(bf16)
    r1 = a - hi.astype(f32)
    mid = r1.astype(bf16)
    lo = (r1 - mid.astype(f32)).astype(bf16)
    r = _dot(tri_bf, jnp.concatenate([hi, mid, lo], axis=1))
    return (r[:, 2 * LANES:] + r[:, LANES:2 * LANES]) + r[:, :LANES]


def _select_dot(a, sel_bf):
    n = a.shape[0]
    hi = a.astype(bf16)
    lo = (a - hi.astype(f32)).astype(bf16)
    r = _dot(jnp.concatenate([hi, lo], axis=0), sel_bf)
    return r[:n] + r[n:]


def _mods_kernel(c_ref, w_ref, b_ref, o_ref):
    s = _silu(c_ref[...]).astype(bf16)
    o_ref[0] = _dot(s, w_ref[0].astype(bf16)) + b_ref[0]


def _mods(cvec8, ada_w, ada_b):
    tn = 1024
    n = 6 * D_MODEL
    return pl.pallas_call(
        _mods_kernel,
        grid=(DEPTH, n // tn),
        in_specs=[pl.BlockSpec((8, D_MODEL), lambda l, j: (0, 0)),
                  pl.BlockSpec((1, D_MODEL, tn), lambda l, j: (l, 0, j)),
                  pl.BlockSpec((1, 1, tn), lambda l, j: (l, 0, j))],
        out_specs=pl.BlockSpec((1, 8, tn), lambda l, j: (l, 0, j)),
        out_shape=jax.ShapeDtypeStruct((DEPTH, 8, n), f32),
        compiler_params=_cparams(("parallel", "parallel"), 40),
        name="adaln_mods",
    )(cvec8, ada_w, ada_b.reshape(DEPTH, 1, n))


NORM_ROWS = 256


def _norm_mm_kernel(x_ref, m_ref, g_ref, *rest, shift_row, scale_row, swiglu):
    w_refs, (o_ref, h_ref) = rest[:-2], rest[-2:]
    tm = x_ref.shape[0]

    @pl.when(pl.program_id(1) == 0)
    def _():
        shift = m_ref[shift_row:shift_row + 1, :]
        scale1p = 1.0 + m_ref[scale_row:scale_row + 1, :]
        g = g_ref[...]

        def body(r, carry):
            sl = pl.ds(pl.multiple_of(r * NORM_ROWS, NORM_ROWS), NORM_ROWS)
            x = x_ref[sl, :]
            y = x * lax.rsqrt(jnp.mean(x * x, axis=-1, keepdims=True) + EPS) * g
            h_ref[sl, :] = (y * scale1p + shift).astype(bf16)
            return carry

        lax.fori_loop(0, tm // NORM_ROWS, body, 0)

    h = h_ref[...]
    if swiglu:
        gate, up = _dot(h, w_refs[0][...]), _dot(h, w_refs[1][...])
        o_ref[...] = (_silu(gate) * up).astype(o_ref.dtype)
    else:
        o_ref[...] = _dot(h, w_refs[0][...]).astype(o_ref.dtype)


def _lspec(block, layer, idx):
    return pl.BlockSpec((None,) + tuple(block), lambda *g: (layer,) + tuple(idx(*g)))


def _mods_spec(width, layer, cond0, rows_per_cond, tm, col):
    return pl.BlockSpec((None, None, 6, width),
                        lambda i, j: (layer, cond0 + (i * tm) // rows_per_cond, 0, j if col else 0))


def _norm_mm(x, mods, g, w, *, layer, cond0, ncond, shift_row, scale_row, swiglu, tm, tn, out_dtype, name):
    m_rows, k = x.shape
    n_out = w.shape[2] // (2 if swiglu else 1)
    n_tiles = n_out // tn
    w_specs = [_lspec((k, tn), layer, lambda i, j: (0, j))]
    if swiglu:
        w_specs.append(_lspec((k, tn), layer, lambda i, j: (0, n_tiles + j)))
    kern = functools.partial(_norm_mm_kernel, shift_row=shift_row, scale_row=scale_row, swiglu=swiglu)
    return pl.pallas_call(
        kern,
        grid=(m_rows // tm, n_tiles),
        in_specs=[pl.BlockSpec((tm, k), lambda i, j: (i, 0)),
                  _mods_spec(k, layer, cond0, m_rows // ncond, tm, False),
                  _lspec((1, k), layer, lambda i, j: (0, 0))] + w_specs,
        out_specs=pl.BlockSpec((tm, tn), lambda i, j: (i, j)),
        out_shape=jax.ShapeDtypeStruct((m_rows, n_out), out_dtype),
        scratch_shapes=[pltpu.VMEM((tm, k), bf16)],
        compiler_params=_cparams(("parallel", "arbitrary"), 48),
        name=name,
    )(x, mods, g, *([w] * len(w_specs)))


def _mm_res_kernel(*refs, n_in, gate_row):
    a_refs, w_refs = refs[:n_in], refs[n_in:2 * n_in]
    x_ref, m_ref, o_ref = refs[2 * n_in:]
    acc = _dot(a_refs[0][...], w_refs[0][...])
    for t in range(1, n_in):
        acc = acc + _dot(a_refs[t][...], w_refs[t][...])
    o_ref[...] = x_ref[...] + m_ref[gate_row:gate_row + 1, :] * acc


def _mm_res(acts, w, x, mods, *, layer, cond0, ncond, gate_row, tm, tn, vmem_mb, name):
    m_rows, n = x.shape
    n_in = len(acts)
    widths = [a.shape[1] for a in acts]
    offs = [sum(widths[:t]) for t in range(n_in)]
    in_specs = [pl.BlockSpec((tm, wd), lambda i, j: (i, 0)) for wd in widths]
    in_specs += [_lspec((wd, tn), layer, functools.partial(lambda i, j, rb: (rb, j), rb=_blk(off, wd)))
                 for wd, off in zip(widths, offs)]
    in_specs += [pl.BlockSpec((tm, tn), lambda i, j: (i, j)),
                 _mods_spec(tn, layer, cond0, m_rows // ncond, tm, True)]
    return pl.pallas_call(
        functools.partial(_mm_res_kernel, n_in=n_in, gate_row=gate_row),
        grid=(m_rows // tm, n // tn),
        in_specs=in_specs,
        out_specs=pl.BlockSpec((tm, tn), lambda i, j: (i, j)),
        out_shape=jax.ShapeDtypeStruct((m_rows, n), f32),
        compiler_params=_cparams(("parallel", "parallel"), vmem_mb),
        name=name,
    )(*acts, *([w] * n_in), x, mods)


def _rope(x, cos, s_up, s_dn):
    return x * cos + pltpu.roll(x, LANES - ROPE_F, 1) * s_up + pltpu.roll(x, ROPE_F, 1) * s_dn


def _mla_kv_rows(ckvn_bf, kr, wuk, wuv, kg, rope, k_scr, v_scr, r0):
    n = kr.shape[0]
    kn = _dot(ckvn_bf, wuk)
    v = _dot(ckvn_bf, wuv)
    krg = kr * kg[:, MLA_NOPE:]
    if rope is not None:
        krg = _rope(krg, *rope)
    kr_ss = jnp.sum(kr * kr, axis=-1, keepdims=True)
    for h in range(MLA_HEADS):
        knh = kn[:, h * MLA_NOPE:(h + 1) * MLA_NOPE]
        r = lax.rsqrt((jnp.sum(knh * knh, axis=-1, keepdims=True) + kr_ss) / MLA_QK + EPS)
        kh = jnp.concatenate([knh * r * kg[:, :MLA_NOPE], krg * r], axis=-1)
        k_scr[h, r0:r0 + n, :] = kh.astype(bf16)
        v_scr[h, r0:r0 + n, :] = v[:, h * MLA_V:(h + 1) * MLA_V].astype(bf16)


def _mla_attend(cq, wuq, qn_g, qg, rope, k_scr, v_scr, o_ref):
    cqn = (cq * _rms_scale(cq, MLA_Q_RANK) * qn_g).astype(bf16)
    q_all = _dot(cqn, wuq)
    scale = MLA_QK ** -0.5
    for h in range(MLA_HEADS):
        qh = q_all[:, h * MLA_QK_PAD:(h + 1) * MLA_QK_PAD]
        qh = qh * _rms_scale(qh, MLA_QK) * qg
        if rope is not None:
            qh = jnp.concatenate([qh[:, :MLA_NOPE], _rope(qh[:, MLA_NOPE:], *rope)], axis=-1)
        s = _dot_nt(qh.astype(bf16), k_scr[h]) * scale
        p = jnp.exp(s - jnp.max(s, axis=-1, keepdims=True))
        l = jnp.sum(p, axis=-1, keepdims=True)
        oh = _dot(p.astype(bf16), v_scr[h]) / l
        o_ref[:, h * MLA_V:(h + 1) * MLA_V] = oh.astype(o_ref.dtype)


def _krope_lanes(sm):
    lane = lax.broadcasted_iota(jnp.int32, sm.shape, 1)
    return jnp.where(lane < MLA_ROPE, sm, 0.0)


def _mla_ctx_kernel(cq_ref, ckv_ref, sm_ref, wuq_ref, wuk_ref, wuv_ref, qn_ref, kvn_ref, qg_ref, kg_ref,
                    *rest):
    o_ref, ckvn_ref, kr_ref, k_scr, v_scr = rest[-5:]
    ckv = ckv_ref[...]
    ckvn = ckv * _rms_scale(ckv, MLA_KV_RANK) * kvn_ref[...]
    ckvn_ref[...] = ckvn
    kr_ref[...] = sm_ref[:, 0:MLA_ROPE]
    _mla_kv_rows(ckvn.astype(bf16), _krope_lanes(sm_ref[...]), wuk_ref[...], wuv_ref[...], kg_ref[...],
                 None, k_scr, v_scr, 0)
    _mla_attend(cq_ref[...], wuq_ref[...], qn_ref[...], qg_ref[...], None, k_scr, v_scr, o_ref)


MLA_QB = 256


def _mla_lat_kernel(cq_ref, ckv_ref, sm_ref, cckv_ref, ckr_ref, rq_ref, rk_ref,
                    wuq_ref, wuk_ref, wuv_ref, qn_ref, kvn_ref, qg_ref, kg_ref, o_ref, k_scr, v_scr):
    @pl.when(pl.program_id(1) == 0)
    def _():
        kg = kg_ref[...]
        _mla_kv_rows(cckv_ref[0, 0].astype(bf16), ckr_ref[0, 0], wuk_ref[...], wuv_ref[...], kg,
                     None, k_scr, v_scr, 0)
        for c in range(DEC_SEQ // MLA_QB):
            rows = slice(c * MLA_QB, (c + 1) * MLA_QB)
            ckv = ckv_ref[rows, :]
            ckvn = ckv * _rms_scale(ckv, MLA_KV_RANK) * kvn_ref[...]
            rope = (rk_ref[0, rows, :], rk_ref[1, rows, :], rk_ref[2, rows, :])
            _mla_kv_rows(ckvn.astype(bf16), _krope_lanes(sm_ref[rows, :]), wuk_ref[...], wuv_ref[...], kg,
                         rope, k_scr, v_scr, PAST_LEN + c * MLA_QB)

    rope_q = (rq_ref[0], rq_ref[1], rq_ref[2])
    _mla_attend(cq_ref[...], wuq_ref[...], qn_ref[...], qg_ref[...], rope_q, k_scr, v_scr, o_ref)


def _mla_weight_specs(layer):
    zero = (lambda *a: (0, 0))
    return [_lspec((MLA_Q_RANK, MLA_HEADS * MLA_QK_PAD), layer, zero),
            _lspec((MLA_KV_RANK, MLA_HEADS * MLA_NOPE), layer, zero),
            _lspec((MLA_KV_RANK, MLA_HEADS * MLA_V), layer, zero),
            _lspec((1, MLA_Q_RANK), layer, zero),
            _lspec((1, MLA_KV_RANK), layer, zero),
            _lspec((1, MLA_QK_PAD), layer, zero),
            _lspec((1, MLA_QK_PAD), layer, zero)]


def _mla_weights(w):
    return [w["w_uq"], w["w_uk"], w["w_uv"], w["qn_g"], w["kvn_g"], w["q_g"], w["k_g"]]


def _stacked(prev, n_regular):
    prev = list(prev or [])
    specs = [pl.BlockSpec(memory_space=pl.ANY)] * len(prev)
    return prev, specs, {n_regular + t: 1 + t for t in range(len(prev))}


def _mla_ctx(u, w, layer, prev):
    t = SEQ
    in_specs = [pl.BlockSpec((t, MLA_Q_RANK), lambda i: (i, _blk(C_CQ, MLA_Q_RANK))),
                pl.BlockSpec((t, MLA_KV_RANK), lambda i: (i, _blk(C_CKV, MLA_KV_RANK))),
                pl.BlockSpec((t, LANES), lambda i: (i, _blk(C_SM, LANES)))] + _mla_weight_specs(layer)
    prev, alias_specs, aliases = _stacked(prev, len(in_specs))
    return pl.pallas_call(
        _mla_ctx_kernel,
        grid=(BATCH,),
        in_specs=in_specs + alias_specs,
        out_specs=[pl.BlockSpec((t, MLA_HEADS * MLA_V), lambda i: (i, 0)),
                   pl.BlockSpec((None, t, MLA_KV_RANK), lambda i: (i, layer, 0)),
                   pl.BlockSpec((None, t, MLA_ROPE), lambda i: (i, layer, 0))],
        out_shape=[jax.ShapeDtypeStruct((BATCH * t, MLA_HEADS * MLA_V), bf16),
                   jax.ShapeDtypeStruct((BATCH, DEPTH * t, MLA_KV_RANK), f32),
                   jax.ShapeDtypeStruct((BATCH, DEPTH * t, MLA_ROPE), f32)],
        input_output_aliases=aliases,
        scratch_shapes=[pltpu.VMEM((MLA_HEADS, t, MLA_QK_PAD), bf16),
                        pltpu.VMEM((MLA_HEADS, t, MLA_V), bf16)],
        compiler_params=_cparams(("parallel",), 40),
        name="mla_ctx",
    )(u, u, u, *_mla_weights(w), *prev)


def _mla_lat(u, cache_ckv, cache_kr, rope_tab, w, layer):
    t = DEC_SEQ
    nq = t // MLA_QB
    tk = PAST_LEN + t
    return pl.pallas_call(
        _mla_lat_kernel,
        grid=(DEC_BATCH, nq),
        in_specs=[pl.BlockSpec((MLA_QB, MLA_Q_RANK), lambda s, q: (s * nq + q, _blk(C_CQ, MLA_Q_RANK))),
                  pl.BlockSpec((t, MLA_KV_RANK), lambda s, q: (s, _blk(C_CKV, MLA_KV_RANK))),
                  pl.BlockSpec((t, LANES), lambda s, q: (s, _blk(C_SM, LANES))),
                  pl.BlockSpec((1, 1, PAST_LEN, MLA_KV_RANK), lambda s, q: (s, layer, 0, 0)),
                  pl.BlockSpec((1, 1, PAST_LEN, LANES), lambda s, q: (s, layer, 0, 0)),
                  pl.BlockSpec((3, MLA_QB, LANES), lambda s, q: (0, q, 0)),
                  pl.BlockSpec((3, t, LANES), lambda s, q: (0, 0, 0))] + _mla_weight_specs(layer),
        out_specs=pl.BlockSpec((MLA_QB, MLA_HEADS * MLA_V), lambda s, q: (s * nq + q, 0)),
        out_shape=jax.ShapeDtypeStruct((DEC_BATCH * t, MLA_HEADS * MLA_V), bf16),
        scratch_shapes=[pltpu.VMEM((MLA_HEADS, tk, MLA_QK_PAD), bf16),
                        pltpu.VMEM((MLA_HEADS, tk, MLA_V), bf16)],
        compiler_params=_cparams(("parallel", "arbitrary"), 48),
        name="mla_lat",
    )(u, u, u, cache_ckv, cache_kr, rope_tab, rope_tab, *_mla_weights(w))


CONV_PAD = 8
CONV_ROWS = 256


def _conv_silu(x_ref, w_ref, b_ref, xp_ref, out_ref, t):
    c = x_ref.shape[1]
    xp_ref[0:CONV_PAD, :] = jnp.zeros((CONV_PAD, c), f32)
    xp_ref[CONV_PAD:CONV_PAD + t, :] = x_ref[...]
    xp_ref[CONV_PAD + t:2 * CONV_PAD + t, :] = jnp.zeros((CONV_PAD, c), f32)
    half = (CONV_W - 1) // 2
    for r0 in range(0, t, CONV_ROWS):
        acc = None
        for j in range(CONV_W):
            start = CONV_PAD - half + j + r0
            term = w_ref[j:j + 1, :] * xp_ref[start:start + CONV_ROWS, :]
            acc = term if acc is None else acc + term
        if b_ref is not None:
            acc = acc + b_ref[...]
        out_ref[r0:r0 + CONV_ROWS, :] = _silu(acc)


def _unit_tri_inverses(xs):
    n = xs[0].shape[0]
    shape = xs[0].shape
    ii = lax.broadcasted_iota(jnp.int32, shape, 0)
    jj = lax.broadcasted_iota(jnp.int32, shape, 1)
    first_half = jj < n
    eye = jnp.where((jj == ii) | (jj == ii + n), 1.0, 0.0)

    def hi_lo(a):
        hi_f = a.astype(bf16).astype(f32)
        return hi_f, a - hi_f

    def left(hi_f, lo_f):
        return jnp.where(first_half, hi_f, lo_f).astype(bf16)

    levels = n.bit_length() - 2
    ps = [eye + x for x in xs]
    for j in range(levels + 1):
        first, last = j == 0, j == levels
        x_parts = [hi_lo(x) for x in xs]
        rhs = [jnp.concatenate([h.astype(bf16)] * 2 + [l.astype(bf16)] * 2, axis=0) for h, l in x_parts]
        lhs = []
        for p, xp in zip(ps, x_parts):
            blocks = []
            if not first:
                blocks.append(left(*hi_lo(p)))
            if not last:
                blocks.append(left(*xp))
            rows = blocks[0] if len(blocks) == 1 else jnp.concatenate(blocks, axis=0)
            lhs.append(jnp.concatenate([rows, rows], axis=1))
        rs = [_dot(l, r) for l, r in zip(lhs, rhs)]
        if not first:
            ps = [p + r[:n] for p, r in zip(ps, rs)]
        if not last:
            xs = [r[-n:] for r in rs]
    return ps


def _gdn_kernel(*refs, t, hb, cpi, has_s0, want_state, n_alias):
    it = iter(refs)
    q_ref, k_ref, v_ref, z_ref, sm_ref = (next(it) for _ in range(5))
    cwq_ref, cwk_ref, cwv_ref, gp_ref, ng_ref = (next(it) for _ in range(5))
    s0_ref = next(it) if has_s0 else None
    for _ in range(n_alias):
        next(it)
    o_ref = next(it)
    sfin_ref = next(it) if want_state else None
    xp_s, qn_s, kn_s, vn_s, beta_s, cum_s, cumt_s, wq_s, uu_s, kt_s, qk_s, eg_s, od_s, st_s = it

    ck = GDN_CHUNK
    nc = t // ck
    head0 = pl.program_id(1) * hb

    _conv_silu(q_ref, cwq_ref, None, xp_s, qn_s, t)
    _conv_silu(k_ref, cwk_ref, None, xp_s, kn_s, t)
    _conv_silu(v_ref, cwv_ref, None, xp_s, vn_s, t)
    for hh in range(hb):
        cols = slice(hh * GDN_DK, (hh + 1) * GDN_DK)
        qh = qn_s[:, cols]
        qn_s[:, cols] = qh * lax.rsqrt(jnp.sum(qh * qh, axis=-1, keepdims=True) + EPS) * (GDN_DK ** -0.5)
        kh = kn_s[:, cols]
        kn_s[:, cols] = kh * lax.rsqrt(jnp.sum(kh * kh, axis=-1, keepdims=True) + EPS)

    sm = sm_ref[...]
    beta_s[...] = _sigmoid(sm)
    g_all = -jnp.exp(gp_ref[0:1, :]) * _softplus(sm + gp_ref[1:2, :])

    ii = lax.broadcasted_iota(jnp.int32, (ck, LANES), 0)
    jj = lax.broadcasted_iota(jnp.int32, (ck, LANES), 1)
    jj = jnp.where(jj < ck, jj, jj - ck)
    incl = (ii >= jj, ii <= jj)
    strict = (ii > jj, ii < jj)
    tri = (incl[0][:, :ck].astype(bf16), incl[1][:, :ck].astype(bf16))

    for c in range(nc):
        rows = slice(c * ck, (c + 1) * ck)
        for d in range(2):
            cum = _tri_cumsum(tri[d], g_all[rows, :])
            cum_s[d, rows, :] = cum
            cumt_s[d, c] = jnp.concatenate([cum, cum], axis=0).T

    if has_s0:
        for d in range(2):
            for hh in range(hb):
                st_s[d * hb + hh] = s0_ref[0, 0, d, hh]
    else:
        st_s[...] = jnp.zeros(st_s.shape, f32)

    def phase1(it_idx, carry):
        chunks = [it_idx * cpi + e for e in range(cpi)]
        rows = [pl.ds(pl.multiple_of(c * ck, ck), ck) for c in chunks]
        beta_c = [beta_s[r, :] for r in rows]
        heads = [(e, hh) for e in range(cpi) for hh in range(hb)]
        kc, qc, vc, g_kk, g_qk = {}, {}, {}, {}, {}
        for e, hh in heads:
            cols = slice(hh * GDN_DK, (hh + 1) * GDN_DK)
            kc[e, hh] = kn_s[rows[e], cols]
            qc[e, hh] = qn_s[rows[e], cols]
            vc[e, hh] = vn_s[rows[e], cols]
        for key in heads:
            kcb = kc[key].astype(bf16)
            kc_dup = jnp.concatenate([kcb, kcb], axis=0)
            g_kk[key] = _dot_nt(kcb, kc_dup)
            g_qk[key] = _dot_nt(qc[key].astype(bf16), kc_dup)
        probs = [(e, hh, d) for e, hh in heads for d in range(2)]
        col, bcol, dec, gl, neg_a = {}, {}, {}, {}, []
        for e, hh, d in probs:
            key = (e, hh, d)
            lane_g = SM_A + d * GDN_HEADS + head0 + hh
            lane_b = SM_BETA + d * GDN_HEADS + head0 + hh
            cum_c = cum_s[d, rows[e], :]
            col[key] = _lane_col(cum_c, lane_g)
            bcol[key] = _lane_col(beta_c[e], lane_b)
            row = cumt_s[d, chunks[e], pl.ds(lane_g, 1), :]
            dec[key] = jnp.where(incl[d], jnp.exp(col[key] - row), 0.0)
            end = cum_c[ck - 1:ck, :] if d == 0 else cum_c[0:1, :]
            gl[key] = _lane_col(end, lane_g)
            neg_a.append(jnp.where(strict[d], -(bcol[key] * g_kk[e, hh] * dec[key]), 0.0))
        tms = _unit_tri_inverses(neg_a)
        ecol, rhs = {}, []
        for key in probs:
            e, hh, d = key
            ecol[key] = jnp.exp(col[key])
            rhs.append(jnp.concatenate([kc[e, hh] * (bcol[key] * ecol[key]), vc[e, hh] * bcol[key]],
                                       axis=-1).astype(bf16))
        wus = [_dot(tm[:, :ck].astype(bf16), r) for tm, r in zip(tms, rhs)]
        for key, wu in zip(probs, wus):
            e, hh, d = key
            idx = d * hb + hh
            r2 = pl.multiple_of(chunks[e] * 2 * ck, 2 * ck)
            wq_s[idx, pl.ds(r2, ck), :] = wu[:, :GDN_DK].astype(bf16)
            wq_s[idx, pl.ds(r2 + ck, ck), :] = (qc[e, hh] * ecol[key]).astype(bf16)
            uu_s[idx, rows[e], :] = wu[:, GDN_DK:]
            kt_s[idx, rows[e], :] = (kc[e, hh] * jnp.exp(gl[key] - col[key])).astype(bf16)
            qk_s[idx, rows[e], :] = jnp.where(incl[d], g_qk[e, hh] * dec[key], 0.0)[:, :ck].astype(bf16)
            eg_s[idx, pl.ds(chunks[e], 1), :] = jnp.broadcast_to(jnp.exp(gl[key]), (1, LANES))
        return carry

    lax.fori_loop(0, nc // cpi, phase1, 0)

    def phase2(i, carry):
        probs = [(d, hh) for d in range(2) for hh in range(hb)]
        chunk = {0: i, 1: nc - 1 - i}
        rows = {d: pl.ds(pl.multiple_of(chunk[d] * ck, ck), ck) for d in range(2)}
        rows2 = {d: pl.ds(pl.multiple_of(chunk[d] * 2 * ck, 2 * ck), 2 * ck) for d in range(2)}
        s = [st_s[d * hb + hh] for d, hh in probs]
        ws = [_dot(wq_s[d * hb + hh, rows2[d], :], sv.astype(bf16)) for (d, hh), sv in zip(probs, s)]
        vb = [(uu_s[d * hb + hh, rows[d], :] - w[:ck]).astype(bf16) for (d, hh), w in zip(probs, ws)]
        o = [w[ck:] + _dot(qk_s[d * hb + hh, rows[d], :], v) for (d, hh), w, v in zip(probs, ws, vb)]
        ds = [_dot_tn(kt_s[d * hb + hh, rows[d], :], v) for (d, hh), v in zip(probs, vb)]
        for (d, hh), sv, dv, ov in zip(probs, s, ds, o):
            idx = d * hb + hh
            st_s[idx] = sv * eg_s[idx, pl.ds(chunk[d], 1), :] + dv
            od_s[d, rows[d], hh * GDN_DV:(hh + 1) * GDN_DV] = ov
        return carry

    lax.fori_loop(0, nc, phase2, 0)

    for hh in range(hb):
        cols = slice(hh * GDN_DV, (hh + 1) * GDN_DV)
        o = od_s[0, :, cols] + od_s[1, :, cols]
        on = o * lax.rsqrt(jnp.mean(o * o, axis=-1, keepdims=True) + EPS) * ng_ref[...]
        o_ref[:, cols] = (on * _silu(z_ref[:, cols])).astype(o_ref.dtype)
    if want_state:
        for d in range(2):
            for hh in range(hb):
                sfin_ref[0, d, hh] = st_s[d * hb + hh]


def _gdn(u, wl, *, nseq, t, hb, cpi, s0, layer, want_state, prev=None):
    w = hb * GDN_DK
    nhb = GDN_HEADS // hb
    nc = t // GDN_CHUNK
    assert nc % cpi == 0
    gq, gk, gv = C_GQKV, C_GQKV + GDN_HEADS * GDN_DK, C_GQKV + 2 * GDN_HEADS * GDN_DK
    in_specs = [pl.BlockSpec((t, w), lambda i, j: (i, _blk(gq, w) + j)),
                pl.BlockSpec((t, w), lambda i, j: (i, _blk(gk, w) + j)),
                pl.BlockSpec((t, w), lambda i, j: (i, _blk(gv, w) + j)),
                pl.BlockSpec((t, w), lambda i, j: (i, _blk(C_GZ, w) + j)),
                pl.BlockSpec((t, LANES), lambda i, j: (i, _blk(C_SM, LANES))),
                _lspec((8, w), layer, lambda i, j: (0, j)),
                _lspec((8, w), layer, lambda i, j: (0, nhb + j)),
                _lspec((8, w), layer, lambda i, j: (0, 2 * nhb + j)),
                _lspec((8, LANES), layer, lambda i, j: (0, 0)),
                _lspec((1, GDN_DV), layer, lambda i, j: (0, 0))]
    args = [u, u, u, u, u, wl["gdn_cw"], wl["gdn_cw"], wl["gdn_cw"], wl["gdn_gp"], wl["gdn_ng"]]
    if s0 is not None:
        in_specs.append(pl.BlockSpec((1, 1, 2, hb, GDN_DK, GDN_DV), lambda i, j: (i, layer, 0, j, 0, 0)))
        args.append(s0)
    prev, alias_specs, aliases = _stacked(prev, len(in_specs))
    out_specs = [pl.BlockSpec((t, w), lambda i, j: (i, j))]
    out_shape = [jax.ShapeDtypeStruct((nseq * t, GDN_HEADS * GDN_DV), bf16)]
    if want_state:
        out_specs.append(pl.BlockSpec((1, None, 2, hb, GDN_DK, GDN_DV), lambda i, j: (i, layer, 0, j, 0, 0)))
        out_shape.append(jax.ShapeDtypeStruct((nseq, DEPTH, 2, GDN_HEADS, GDN_DK, GDN_DV), f32))
    scratch = [pltpu.VMEM((t + 2 * CONV_PAD, w), f32),
               pltpu.VMEM((t, w), f32), pltpu.VMEM((t, w), f32), pltpu.VMEM((t, w), f32),
               pltpu.VMEM((t, LANES), f32),
               pltpu.VMEM((2, t, LANES), f32),
               pltpu.VMEM((2, nc, LANES, LANES), f32),
               pltpu.VMEM((2 * hb, 2 * t, GDN_DK), bf16),
               pltpu.VMEM((2 * hb, t, GDN_DV), f32),
               pltpu.VMEM((2 * hb, t, GDN_DK), bf16),
               pltpu.VMEM((2 * hb, t, GDN_CHUNK), bf16),
               pltpu.VMEM((2 * hb, max(nc, 8), LANES), f32),
               pltpu.VMEM((2, t, w), f32),
               pltpu.VMEM((2 * hb, GDN_DK, GDN_DV), f32)]
    kern = functools.partial(_gdn_kernel, t=t, hb=hb, cpi=cpi, has_s0=s0 is not None, want_state=want_state,
                             n_alias=len(prev))
    return pl.pallas_call(
        kern,
        grid=(nseq, nhb),
        in_specs=in_specs + alias_specs,
        out_specs=out_specs,
        out_shape=out_shape,
        input_output_aliases=aliases,
        scratch_shapes=scratch,
        compiler_params=_cparams(("parallel", "parallel"), 48),
        name="gdn_" + ("lat" if s0 is not None else "ctx"),
    )(*args, *prev)


def _ssd_kernel(*refs, t, cpi, has_s0, want_state, n_alias):
    it = iter(refs)
    x_ref, bc_ref, z_ref, sm_ref = (next(it) for _ in range(4))
    cwx_ref, cwbc_ref, cbx_ref, cbbc_ref = (next(it) for _ in range(4))
    gp_ref, ex_ref, dv_ref, ng_ref = (next(it) for _ in range(4))
    s0_ref = next(it) if has_s0 else None
    for _ in range(n_alias):
        next(it)
    o_ref = next(it)
    sfin_ref = next(it) if want_state else None
    xp_s, xs_s, bcs_s, dt_s, da_s, y_s, ea_s, stc_s, cd_s, st_s = it

    ck = SSD_CHUNK
    nc = t // ck
    wg = HEAD_PER_GROUP * SSD_P
    nb = SSD_GROUPS * SSD_N

    _conv_silu(x_ref, cwx_ref, cbx_ref, xp_s, xs_s, t)
    _conv_silu(bc_ref, cwbc_ref, cbbc_ref, xp_s, bcs_s, t)

    dt_all = _softplus(sm_ref[...] + gp_ref[1:2, :])
    dt_s[...] = dt_all
    da_s[...] = dt_all * (-jnp.exp(gp_ref[0:1, :]))

    if has_s0:
        for d in range(2):
            s0 = jnp.concatenate([s0_ref[0, 0, d, h] for h in range(SSD_HEADS)], axis=0)
            st_s[d] = s0.T
    else:
        st_s[...] = jnp.zeros(st_s.shape, f32)

    ii = lax.broadcasted_iota(jnp.int32, (ck, ck), 0)
    jj = lax.broadcasted_iota(jnp.int32, (ck, ck), 1)
    incl = (ii >= jj, ii <= jj)
    tri = (incl[0].astype(bf16), incl[1].astype(bf16))
    lane = lax.broadcasted_iota(jnp.int32, (ck, LANES), 1)
    end_rows = 16
    dirs = (0, 1)

    def phase1(it_idx, carry):
        chunks = [it_idx * cpi + e for e in range(cpi)]
        rows = [pl.ds(pl.multiple_of(c * ck, ck), ck) for c in chunks]
        probs = [(e, d) for e in range(cpi) for d in dirs]
        dt_c = [dt_s[r, :] for r in rows]
        da_c = [da_s[r, :] for r in rows]
        cum = {(e, d): _tri_cumsum(tri[d], da_c[e]) for e, d in probs}
        cum_t = {key: cum[key].T for key in probs}
        dt_t = [v.T for v in dt_c]
        bcc = [bcs_s[r, :].astype(bf16) for r in rows]
        xc = [xs_s[r, :] for r in rows]
        xb = [v.astype(bf16) for v in xc]
        cb = {(e, g): _dot_nt(bcc[e][:, nb + g * SSD_N:nb + (g + 1) * SSD_N], bcc[e][:, g * SSD_N:(g + 1) * SSD_N])
              for e in range(cpi) for g in range(SSD_GROUPS)}
        heads = [(e, d, h) for e, d in probs for h in range(SSD_HEADS)]
        ms = []
        for e, d, h in heads:
            ln = SM_DT + d * SSD_HEADS + h
            col = _lane_col(cum[e, d], ln)
            row = cum_t[e, d][ln:ln + 1, :]
            dtrow = dt_t[e][ln:ln + 1, :]
            lm = jnp.where(incl[d], jnp.exp(col - row), 0.0)
            ms.append((cb[e, h // HEAD_PER_GROUP] * lm * dtrow).astype(bf16))
        ys = {key: _dot(m, xb[key[0]][:, (key[2] // 2) * LANES:(key[2] // 2 + 1) * LANES])
              for key, m in zip(heads, ms)}
        end = {(e, d): cum[e, d][ck - 1:ck, :] if d == 0 else cum[e, d][0:1, :] for e, d in probs}
        spread = {key: _select_dot(jnp.concatenate([jnp.exp(end[key] - cum[key]) * dt_c[key[0]], jnp.exp(cum[key]),
                                                    jnp.broadcast_to(jnp.exp(end[key]), (end_rows, LANES))],
                                                   axis=0), ex_ref[key[1]]) for key in probs}
        xsc = {key: (xc[key[0]] * spread[key][:ck]).astype(bf16) for key in probs}
        st_c = {(e, d, g): _dot_tn(bcc[e][:, g * SSD_N:(g + 1) * SSD_N], xsc[e, d][:, g * wg:(g + 1) * wg])
                for e, d in probs for g in range(SSD_GROUPS)}
        for e, d in probs:
            parts = [jnp.where(lane < SSD_P, ys[e, d, 2 * pr], ys[e, d, 2 * pr + 1]) for pr in range(SSD_HEADS // 2)]
            y_s[d, rows[e], :] = jnp.concatenate(parts, axis=-1)
            ea_s[d, rows[e], :] = spread[e, d][ck:2 * ck]
            cd_s[d, chunks[e]] = spread[e, d][2 * ck:2 * ck + 8]
            stc_s[d, chunks[e]] = jnp.concatenate([st_c[e, d, g] for g in range(SSD_GROUPS)], axis=-1)
        return carry

    lax.fori_loop(0, nc // cpi, phase1, 0)

    def phase2(i, carry):
        chunk = (i, nc - 1 - i)
        rows = [pl.ds(pl.multiple_of(c * ck, ck), ck) for c in chunk]
        st = [st_s[d] for d in dirs]
        stb = [v.astype(bf16) for v in st]
        y_off = {(d, g): _dot(bcs_s[rows[d], nb + g * SSD_N:nb + (g + 1) * SSD_N].astype(bf16),
                              stb[d][:, g * wg:(g + 1) * wg]) for d in dirs for g in range(SSD_GROUPS)}
        for d in dirs:
            off = jnp.concatenate([y_off[d, g] for g in range(SSD_GROUPS)], axis=-1)
            y_s[d, rows[d], :] = y_s[d, rows[d], :] + off * ea_s[d, rows[d], :]
            st_s[d] = st[d] * cd_s[d, chunk[d], 0:1, :] + stc_s[d, chunk[d]]
        return carry

    lax.fori_loop(0, nc, phase2, 0)

    y = y_s[0] + y_s[1] + xs_s[...] * dv_ref[...]
    y = y * _silu(z_ref[...])
    for g in range(SSD_GROUPS):
        cols = slice(g * wg, (g + 1) * wg)
        yg = y[:, cols]
        o_ref[:, cols] = (yg * lax.rsqrt(jnp.mean(yg * yg, axis=-1, keepdims=True) + EPS)
                          * ng_ref[:, cols]).astype(o_ref.dtype)
    if want_state:
        for d in range(2):
            stt = st_s[d].T
            for h in range(SSD_HEADS):
                sfin_ref[0, d, h] = stt[h * SSD_P:(h + 1) * SSD_P, :]


def _ssd(u, wl, *, nseq, t, cpi, s0, layer, want_state, prev=None):
    wi, wbc = SSD_INNER, 2 * SSD_GROUPS * SSD_N
    nc = t // SSD_CHUNK
    assert nc % cpi == 0 and wi == wbc
    in_specs = [pl.BlockSpec((t, wi), lambda i: (i, _blk(C_XBC, wi))),
                pl.BlockSpec((t, wbc), lambda i: (i, _blk(C_XBC + wi, wbc))),
                pl.BlockSpec((t, wi), lambda i: (i, _blk(C_SZ, wi))),
                pl.BlockSpec((t, LANES), lambda i: (i, _blk(C_SM, LANES))),
                _lspec((8, wi), layer, lambda i: (0, 0)),
                _lspec((8, wbc), layer, lambda i: (0, 1)),
                _lspec((1, wi), layer, lambda i: (0, 0)),
                _lspec((1, wbc), layer, lambda i: (0, 1)),
                _lspec((8, LANES), layer, lambda i: (0, 0)),
                pl.BlockSpec((2, LANES, wi), lambda i: (0, 0, 0)),
                _lspec((1, wi), layer, lambda i: (0, 0)),
                _lspec((1, wi), layer, lambda i: (0, 0))]
    args = [u, u, u, u, wl["ssd_cw"], wl["ssd_cw"], wl["ssd_cb"], wl["ssd_cb"],
            wl["ssd_gp"], wl["ssd_ex"], wl["ssd_dv"], wl["ssd_ng"]]
    if s0 is not None:
        in_specs.append(pl.BlockSpec((1, 1, 2, SSD_HEADS, SSD_P, SSD_N), lambda i: (i, layer, 0, 0, 0, 0)))
        args.append(s0)
    prev, alias_specs, aliases = _stacked(prev, len(in_specs))
    out_specs = [pl.BlockSpec((t, wi), lambda i: (i, 0))]
    out_shape = [jax.ShapeDtypeStruct((nseq * t, wi), bf16)]
    if want_state:
        out_specs.append(pl.BlockSpec((1, None, 2, SSD_HEADS, SSD_P, SSD_N), lambda i: (i, layer, 0, 0, 0, 0)))
        out_shape.append(jax.ShapeDtypeStruct((nseq, DEPTH, 2, SSD_HEADS, SSD_P, SSD_N), f32))
    scratch = [pltpu.VMEM((t + 2 * CONV_PAD, wi), f32),
               pltpu.VMEM((t, wi), f32), pltpu.VMEM((t, wbc), f32),
               pltpu.VMEM((t, LANES), f32), pltpu.VMEM((t, LANES), f32),
               pltpu.VMEM((2, t, wi), f32),
               pltpu.VMEM((2, t, wi), f32),
               pltpu.VMEM((2, nc, SSD_N, wi), f32),
               pltpu.VMEM((2, nc, 8, wi), f32),
               pltpu.VMEM((2, SSD_N, wi), f32)]
    kern = functools.partial(_ssd_kernel, t=t, cpi=cpi, has_s0=s0 is not None, want_state=want_state,
                             n_alias=len(prev))
    return pl.pallas_call(
        kern,
        grid=(nseq,),
        in_specs=in_specs + alias_specs,
        out_specs=out_specs,
        out_shape=out_shape,
        input_output_aliases=aliases,
        scratch_shapes=scratch,
        compiler_params=_cparams(("parallel",), 48),
        name="ssd_" + ("lat" if s0 is not None else "ctx"),
    )(*args, *prev)


def _prep_weights(p):
    w_in = p["w_in"]
    o_g = MLA_Q_RANK + MLA_KV_RANK + MLA_ROPE
    o_s = o_g + 2 * GDN_HEADS * GDN_DK + 2 * GDN_HEADS * GDN_DV + 4 * GDN_HEADS
    n_qkv = 2 * GDN_HEADS * GDN_DK + GDN_HEADS * GDN_DV
    n_gz = GDN_HEADS * GDN_DV
    sl = lambda a, b: w_in[:, :, a:b]
    zeros = lambda n: jnp.zeros(w_in.shape[:2] + (n,), w_in.dtype)
    n_xbc = SSD_INNER + 2 * SSD_GROUPS * SSD_N
    parts = [sl(0, MLA_Q_RANK),
             sl(MLA_Q_RANK + MLA_KV_RANK, o_g),
             sl(o_g + n_qkv + n_gz, o_g + n_qkv + n_gz + 4 * GDN_HEADS),
             sl(o_s + SSD_INNER + n_xbc, o_s + SSD_INNER + n_xbc + 2 * SSD_HEADS),
             zeros(C_XBC - (C_SM + MLA_ROPE + 4 * GDN_HEADS + 2 * SSD_HEADS)),
             sl(o_s + SSD_INNER, o_s + SSD_INNER + n_xbc),
             sl(MLA_Q_RANK, MLA_Q_RANK + MLA_KV_RANK),
             sl(o_g + n_qkv, o_g + n_qkv + n_gz),
             sl(o_g, o_g + n_qkv),
             sl(o_s, o_s + SSD_INNER)]
    w_in_p = jnp.concatenate(parts, axis=-1).astype(bf16)
    assert w_in_p.shape[-1] == N_IN

    w_uq = p["mla_w_uq"].reshape(DEPTH, MLA_Q_RANK, MLA_HEADS, MLA_QK)
    w_uq = jnp.pad(w_uq, ((0, 0), (0, 0), (0, 0), (0, MLA_QK_PAD - MLA_QK)))
    w_uq = w_uq.reshape(DEPTH, MLA_Q_RANK, MLA_HEADS * MLA_QK_PAD).astype(bf16)
    w_ukv = p["mla_w_ukv"].reshape(DEPTH, MLA_KV_RANK, MLA_HEADS, MLA_NOPE + MLA_V)
    w_uk = w_ukv[..., :MLA_NOPE].reshape(DEPTH, MLA_KV_RANK, MLA_HEADS * MLA_NOPE).astype(bf16)
    w_uv = w_ukv[..., MLA_NOPE:].reshape(DEPTH, MLA_KV_RANK, MLA_HEADS * MLA_V).astype(bf16)
    pad_g = lambda g: jnp.pad(g, ((0, 0), (0, MLA_QK_PAD - MLA_QK)))[:, None, :]

    def lane_rows(a_log, dt_bias, lane0):
        n = a_log.shape[1] * a_log.shape[2]
        rows = jnp.stack([a_log.reshape(DEPTH, n), dt_bias.reshape(DEPTH, n)], axis=1)
        return jnp.pad(rows.astype(f32), ((0, 0), (0, 6), (lane0, LANES - lane0 - n)))

    ex = np.zeros((2, LANES, SSD_INNER), np.float32)
    for d in range(2):
        for h in range(SSD_HEADS):
            ex[d, SM_DT + d * SSD_HEADS + h, h * SSD_P:(h + 1) * SSD_P] = 1.0

    pad_rows = lambda w: jnp.pad(w.astype(f32), ((0, 0), (0, 8 - CONV_W), (0, 0)))
    return dict(
        w_in=w_in_p, w_uq=w_uq, w_uk=w_uk, w_uv=w_uv,
        qn_g=p["mla_qnorm_g"][:, None, :], kvn_g=p["mla_kvnorm_g"][:, None, :],
        q_g=pad_g(p["mla_q_g"]), k_g=pad_g(p["mla_k_g"]),
        w_out=p["w_out"].astype(bf16), w_gu=p["ffn_w_gu"].astype(bf16), w_down=p["ffn_w_down"].astype(bf16),
        norm1_g=p["norm1_g"][:, None, :], norm2_g=p["norm2_g"][:, None, :],
        gdn_cw=pad_rows(p["gdn_conv_w"]), gdn_gp=lane_rows(p["gdn_a_log"], p["gdn_dt_bias"], SM_A),
        gdn_ng=p["gdn_norm_g"][:, None, :],
        ssd_cw=pad_rows(p["ssd_conv_w"]), ssd_cb=p["ssd_conv_b"][:, None, :],
        ssd_gp=lane_rows(p["ssd_a_log"], p["ssd_dt_bias"], SM_DT), ssd_ex=jnp.asarray(ex, dtype=bf16),
        ssd_dv=jnp.repeat(p["ssd_d"], SSD_P, axis=1)[:, None, :], ssd_ng=p["ssd_norm_g"][:, None, :],
    )


def _rope_tables(n_tokens):
    rows = n_tokens // GRID_W
    row = jnp.repeat(jnp.arange(rows, dtype=f32), GRID_W)
    col = jnp.tile(jnp.arange(GRID_W, dtype=f32), rows)
    inv = ROPE_BASE ** (-jnp.arange(ROPE_F, dtype=f32) / ROPE_F)
    ar, ac = row[:, None] * inv, col[:, None] * inv
    zero = jnp.zeros_like(ar)
    tail = jnp.zeros((n_tokens, LANES - MLA_ROPE), f32)
    cos = jnp.concatenate([jnp.cos(ar), jnp.cos(ar), jnp.cos(ac), jnp.cos(ac), tail], axis=-1)
    s_up = jnp.concatenate([-jnp.sin(ar), zero, -jnp.sin(ac), zero, tail], axis=-1)
    s_dn = jnp.concatenate([zero, jnp.sin(ar), zero, jnp.sin(ac), tail], axis=-1)
    return jnp.stack([cos, s_up, s_dn], axis=0)


FFN_TF = 512
DENSE_TM = 1024
IN_TN = 1024
OUT_TM = 512
DOWN_TN = 512


def _trunk_layer(x, mods, wl, l, *, nseq, t, latent, cache=None, stacked=None):
    cond = dict(layer=l, cond0=1 if latent else 0, ncond=nseq if latent else 1)
    u = _norm_mm(x, mods, wl["norm1_g"], wl["w_in"], shift_row=0, scale_row=1, swiglu=False,
                 tm=DENSE_TM, tn=IN_TN, out_dtype=f32, name="in_proj", **cond)
    if latent:
        o_mla = _mla_lat(u, cache["ckv"], cache["krope"], cache["rope"], wl, l)
        o_gdn, = _gdn(u, wl, nseq=nseq, t=t, hb=2, cpi=2, s0=cache["gdn"], layer=l, want_state=False)
        o_ssd, = _ssd(u, wl, nseq=nseq, t=t, cpi=2, s0=cache["ssd"], layer=l, want_state=False)
    else:
        prev = (lambda a, b: None) if stacked is None else (lambda a, b: stacked[a:b])
        o_mla, ckv_all, kr_all = _mla_ctx(u, wl, l, prev(0, 2))
        o_gdn, sg_all = _gdn(u, wl, nseq=nseq, t=t, hb=GDN_HEADS, cpi=2, s0=None, layer=l, want_state=True,
                             prev=prev(2, 3))
        o_ssd, ss_all = _ssd(u, wl, nseq=nseq, t=t, cpi=2, s0=None, layer=l, want_state=True, prev=prev(3, 4))
        stacked = (ckv_all, kr_all, sg_all, ss_all)
    x = _mm_res([o_mla, o_gdn, o_ssd], wl["w_out"], x, mods, gate_row=2, tm=OUT_TM, tn=D_MODEL,
                vmem_mb=48, name="out_proj", **cond)
    act = _norm_mm(x, mods, wl["norm2_g"], wl["w_gu"], shift_row=3, scale_row=4, swiglu=True,
                   tm=DENSE_TM, tn=FFN_TF, out_dtype=bf16, name="ffn_gu", **cond)
    x = _mm_res([act], wl["w_down"], x, mods, gate_row=5, tm=DENSE_TM, tn=DOWN_TN, vmem_mb=56,
                name="ffn_down", **cond)
    return x, stacked


def kernel(x_prompt, x_sample, cache_mla_ckv, cache_mla_krope, state_gdn, state_ssd, c, c_ctx, norm1_g, norm2_g, ada_w, ada_b, w_in, w_out, mla_qnorm_g, mla_w_uq, mla_kvnorm_g, mla_w_ukv, mla_q_g, mla_k_g, gdn_conv_w, gdn_a_log, gdn_dt_bias, gdn_norm_g, ssd_conv_w, ssd_conv_b, ssd_a_log, ssd_dt_bias, ssd_d, ssd_norm_g, ffn_w_gu, ffn_w_down):
    p = dict(norm1_g=norm1_g, norm2_g=norm2_g, w_in=w_in, w_out=w_out,
             mla_qnorm_g=mla_qnorm_g, mla_w_uq=mla_w_uq, mla_kvnorm_g=mla_kvnorm_g,
             mla_w_ukv=mla_w_ukv, mla_q_g=mla_q_g, mla_k_g=mla_k_g, gdn_conv_w=gdn_conv_w,
             gdn_a_log=gdn_a_log, gdn_dt_bias=gdn_dt_bias, gdn_norm_g=gdn_norm_g,
             ssd_conv_w=ssd_conv_w, ssd_conv_b=ssd_conv_b, ssd_a_log=ssd_a_log,
             ssd_dt_bias=ssd_dt_bias, ssd_d=ssd_d, ssd_norm_g=ssd_norm_g,
             ffn_w_gu=ffn_w_gu, ffn_w_down=ffn_w_down)
    w = _prep_weights(p)

    cvec = jnp.concatenate([c_ctx[None, :], c, jnp.zeros((8 - 1 - DEC_BATCH, D_MODEL), f32)], axis=0)
    mods = _mods(cvec, ada_w, ada_b).reshape(DEPTH, 8, 6, D_MODEL)

    cache = dict(ckv=cache_mla_ckv,
                 krope=jnp.pad(cache_mla_krope, ((0, 0), (0, 0), (0, 0), (0, LANES - MLA_ROPE))),
                 rope=_rope_tables(DEC_SEQ), gdn=state_gdn, ssd=state_ssd)

    xp = x_prompt.reshape(BATCH * SEQ, D_MODEL)
    xs = x_sample.reshape(DEC_BATCH * DEC_SEQ, D_MODEL)
    stacked = None
    for l in range(DEPTH):
        xp, stacked = _trunk_layer(xp, mods, w, l, nseq=BATCH, t=SEQ, latent=False, stacked=stacked)
        xs, _ = _trunk_layer(xs, mods, w, l, nseq=DEC_BATCH, t=DEC_SEQ, latent=True, cache=cache)
    ckv_all, kr_all, sg_all, ss_all = stacked
    return (xp.reshape(BATCH, SEQ, D_MODEL), xs.reshape(DEC_BATCH, DEC_SEQ, D_MODEL),
            ckv_all.reshape(BATCH, DEPTH, SEQ, MLA_KV_RANK), kr_all.reshape(BATCH, DEPTH, SEQ, MLA_ROPE),
            sg_all, ss_all)
```

```python
import functools

import numpy as np
import jax
import jax.numpy as jnp
from jax import lax
from jax.experimental import pallas as pl
from jax.experimental.pallas import tpu as pltpu

f32 = jnp.float32
bf16 = jnp.bfloat16

D_MODEL = 2048
BATCH = 32
SEQ = 256
DEPTH = 4
DEC_BATCH = 2
DEC_SEQ = 1024
PAST_LEN = 256
GRID_W = 64
MLA_HEADS = 8
MLA_NOPE = 128
MLA_ROPE = 64
MLA_QK = MLA_NOPE + MLA_ROPE
MLA_V = 128
MLA_Q_RANK = 768
MLA_KV_RANK = 512
ROPE_F = MLA_ROPE // 4
ROPE_BASE = 10000.0
GDN_HEADS = 4
GDN_DK = 128
GDN_DV = 128
GDN_CHUNK = 64
SSD_HEADS = 8
SSD_P = 64
SSD_GROUPS = 2
SSD_N = 128
SSD_CHUNK = 128
SSD_INNER = SSD_HEADS * SSD_P
CONV_W = 5
FF = -(-8 * D_MODEL // (3 * 256)) * 256
EPS = 1e-6

LANES = 128
MLA_QK_PAD = 256
HEAD_PER_GROUP = SSD_HEADS // SSD_GROUPS

C_CQ = 0
C_SM = 768
C_XBC = 1024
C_CKV = 2048
C_GZ = 2560
C_GQKV = 3072
C_SZ = 4608
N_IN = 5120
SM_BETA = 64
SM_A = 72
SM_DT = 80

VMEM_MB = 1024 * 1024


def _cparams(sem, vmem_mb):
    return pltpu.CompilerParams(dimension_semantics=sem, vmem_limit_bytes=vmem_mb * VMEM_MB)


def _blk(off, width):
    assert off % width == 0
    return off // width


def _sigmoid(x):
    return 1.0 / (1.0 + jnp.exp(-x))


def _silu(x):
    return x * _sigmoid(x)


def _softplus(x):
    return jnp.maximum(x, 0.0) + jnp.log(1.0 + jnp.exp(-jnp.abs(x)))


def _rms_scale(x, n):
    return lax.rsqrt(jnp.sum(x * x, axis=-1, keepdims=True) / n + EPS)


def _lane_col(a, lane_idx):
    lane = lax.broadcasted_iota(jnp.int32, a.shape, 1)
    return jnp.sum(jnp.where(lane == lane_idx, a, 0.0), axis=1, keepdims=True)


def _dot(a, b):
    return jnp.dot(a, b, preferred_element_type=f32)


def _dot_nt(a, b):
    return lax.dot_general(a, b, (((1,), (1,)), ((), ())), preferred_element_type=f32)


def _dot_tn(a, b):
    return lax.dot_general(a, b, (((0,), (0,)), ((), ())), preferred_element_type=f32)


def _tri_cumsum(tri_bf, a):
    hi = a.astype(bf16)
    r1 = a - hi.astype(f32)
    mid = r1.astype(bf16)
    lo = (r1 - mid.astype(f32)).astype(bf16)
    r = _dot(tri_bf, jnp.concatenate([hi, mid, lo], axis=1))
    return (r[:, 2 * LANES:] + r[:, LANES:2 * LANES]) + r[:, :LANES]


def _select_dot(a, sel_bf):
    n = a.shape[0]
    hi = a.astype(bf16)
    lo = (a - hi.astype(f32)).astype(bf16)
    r = _dot(jnp.concatenate([hi, lo], axis=0), sel_bf)
    return r[:n] + r[n:]


def _mods_kernel(c_ref, w_ref, b_ref, o_ref):
    s = _silu(c_ref[...]).astype(bf16)
    o_ref[0] = _dot(s, w_ref[0].astype(bf16)) + b_ref[0]


def _mods(cvec8, ada_w, ada_b):
    tn = 1024
    n = 6 * D_MODEL
    return pl.pallas_call(
        _mods_kernel,
        grid=(DEPTH, n // tn),
        in_specs=[pl.BlockSpec((8, D_MODEL), lambda l, j: (0, 0)),
                  pl.BlockSpec((1, D_MODEL, tn), lambda l, j: (l, 0, j)),
                  pl.BlockSpec((1, 1, tn), lambda l, j: (l, 0, j))],
        out_specs=pl.BlockSpec((1, 8, tn), lambda l, j: (l, 0, j)),
        out_shape=jax.ShapeDtypeStruct((DEPTH, 8, n), f32),
        compiler_params=_cparams(("parallel", "parallel"), 40),
        name="adaln_mods",
    )(cvec8, ada_w, ada_b.reshape(DEPTH, 1, n))


NORM_ROWS = 256


def _norm_mm_kernel(x_ref, m_ref, g_ref, *rest, shift_row, scale_row, swiglu):
    w_refs, (o_ref, h_ref) = rest[:-2], rest[-2:]
    tm = x_ref.shape[0]

    @pl.when(pl.program_id(1) == 0)
    def _():
        shift = m_ref[shift_row:shift_row + 1, :]
        scale1p = 1.0 + m_ref[scale_row:scale_row + 1, :]
        g = g_ref[...]

        def body(r, carry):
            sl = pl.ds(pl.multiple_of(r * NORM_ROWS, NORM_ROWS), NORM_ROWS)
            x = x_ref[sl, :]
            y = x * lax.rsqrt(jnp.mean(x * x, axis=-1, keepdims=True) + EPS) * g
            h_ref[sl, :] = (y * scale1p + shift).astype(bf16)
            return carry

        lax.fori_loop(0, tm // NORM_ROWS, body, 0)

    h = h_ref[...]
    if swiglu:
        gate, up = _dot(h, w_refs[0][...]), _dot(h, w_refs[1][...])
        o_ref[...] = (_silu(gate) * up).astype(o_ref.dtype)
    else:
        o_ref[...] = _dot(h, w_refs[0][...]).astype(o_ref.dtype)


def _lspec(block, layer, idx):
    return pl.BlockSpec((None,) + tuple(block), lambda *g: (layer,) + tuple(idx(*g)))


def _mods_spec(width, layer, cond0, rows_per_cond, tm, col):
    return pl.BlockSpec((None, None, 6, width),
                        lambda i, j: (layer, cond0 + (i * tm) // rows_per_cond, 0, j if col else 0))


def _norm_mm(x, mods, g, w, *, layer, cond0, ncond, shift_row, scale_row, swiglu, tm, tn, out_dtype, name):
    m_rows, k = x.shape
    n_out = w.shape[2] // (2 if swiglu else 1)
    n_tiles = n_out // tn
    w_specs = [_lspec((k, tn), layer, lambda i, j: (0, j))]
    if swiglu:
        w_specs.append(_lspec((k, tn), layer, lambda i, j: (0, n_tiles + j)))
    kern = functools.partial(_norm_mm_kernel, shift_row=shift_row, scale_row=scale_row, swiglu=swiglu)
    return pl.pallas_call(
        kern,
        grid=(m_rows // tm, n_tiles),
        in_specs=[pl.BlockSpec((tm, k), lambda i, j: (i, 0)),
                  _mods_spec(k, layer, cond0, m_rows // ncond, tm, False),
                  _lspec((1, k), layer, lambda i, j: (0, 0))] + w_specs,
        out_specs=pl.BlockSpec((tm, tn), lambda i, j: (i, j)),
        out_shape=jax.ShapeDtypeStruct((m_rows, n_out), out_dtype),
        scratch_shapes=[pltpu.VMEM((tm, k), bf16)],
        compiler_params=_cparams(("parallel", "arbitrary"), 48),
        name=name,
    )(x, mods, g, *([w] * len(w_specs)))


def _mm_res_kernel(*refs, n_in, gate_row):
    a_refs, w_refs = refs[:n_in], refs[n_in:2 * n_in]
    x_ref, m_ref, o_ref = refs[2 * n_in:]
    acc = _dot(a_refs[0][...], w_refs[0][...])
    for t in range(1, n_in):
        acc = acc + _dot(a_refs[t][...], w_refs[t][...])
    o_ref[...] = x_ref[...] + m_ref[gate_row:gate_row + 1, :] * acc


def _mm_res(acts, w, x, mods, *, layer, cond0, ncond, gate_row, tm, tn, vmem_mb, name):
    m_rows, n = x.shape
    n_in = len(acts)
    widths = [a.shape[1] for a in acts]
    offs = [sum(widths[:t]) for t in range(n_in)]
    in_specs = [pl.BlockSpec((tm, wd), lambda i, j: (i, 0)) for wd in widths]
    in_specs += [_lspec((wd, tn), layer, functools.partial(lambda i, j, rb: (rb, j), rb=_blk(off, wd)))
                 for wd, off in zip(widths, offs)]
    in_specs += [pl.BlockSpec((tm, tn), lambda i, j: (i, j)),
                 _mods_spec(tn, layer, cond0, m_rows // ncond, tm, True)]
    return pl.pallas_call(
        functools.partial(_mm_res_kernel, n_in=n_in, gate_row=gate_row),
        grid=(m_rows // tm, n // tn),
        in_specs=in_specs,
        out_specs=pl.BlockSpec((tm, tn), lambda i, j: (i, j)),
        out_shape=jax.ShapeDtypeStruct((m_rows, n), f32),
        compiler_params=_cparams(("parallel", "parallel"), vmem_mb),
        name=name,
    )(*acts, *([w] * n_in), x, mods)


def _rope(x, cos, s_up, s_dn):
    return x * cos + pltpu.roll(x, LANES - ROPE_F, 1) * s_up + pltpu.roll(x, ROPE_F, 1) * s_dn


def _mla_kv_rows(ckvn_bf, kr, wuk, wuv, kg, rope, k_scr, v_scr, r0):
    n = kr.shape[0]
    kn = _dot(ckvn_bf, wuk)
    v = _dot(ckvn_bf, wuv)
    krg = kr * kg[:, MLA_NOPE:]
    if rope is not None:
        krg = _rope(krg, *rope)
    kr_ss = jnp.sum(kr * kr, axis=-1, keepdims=True)
    for h in range(MLA_HEADS):
        knh = kn[:, h * MLA_NOPE:(h + 1) * MLA_NOPE]
        r = lax.rsqrt((jnp.sum(knh * knh, axis=-1, keepdims=True) + kr_ss) / MLA_QK + EPS)
        kh = jnp.concatenate([knh * r * kg[:, :MLA_NOPE], krg * r], axis=-1)
        k_scr[h, r0:r0 + n, :] = kh.astype(bf16)
        v_scr[h, r0:r0 + n, :] = v[:, h * MLA_V:(h + 1) * MLA_V].astype(bf16)


def _mla_attend(cq, wuq, qn_g, qg, rope, k_scr, v_scr, o_ref):
    cqn = (cq * _rms_scale(cq, MLA_Q_RANK) * qn_g).astype(bf16)
    q_all = _dot(cqn, wuq)
    scale = MLA_QK ** -0.5
    for h in range(MLA_HEADS):
        qh = q_all[:, h * MLA_QK_PAD:(h + 1) * MLA_QK_PAD]
        qh = qh * _rms_scale(qh, MLA_QK) * qg
        if rope is not None:
            qh = jnp.concatenate([qh[:, :MLA_NOPE], _rope(qh[:, MLA_NOPE:], *rope)], axis=-1)
        s = _dot_nt(qh.astype(bf16), k_scr[h]) * scale
        p = jnp.exp(s - jnp.max(s, axis=-1, keepdims=True))
        l = jnp.sum(p, axis=-1, keepdims=True)
        oh = _dot(p.astype(bf16), v_scr[h]) / l
        o_ref[:, h * MLA_V:(h + 1) * MLA_V] = oh.astype(o_ref.dtype)


def _krope_lanes(sm):
    lane = lax.broadcasted_iota(jnp.int32, sm.shape, 1)
    return jnp.where(lane < MLA_ROPE, sm, 0.0)


def _mla_ctx_kernel(cq_ref, ckv_ref, sm_ref, wuq_ref, wuk_ref, wuv_ref, qn_ref, kvn_ref, qg_ref, kg_ref,
                    *rest):
    o_ref, ckvn_ref, kr_ref, k_scr, v_scr = rest[-5:]
    ckv = ckv_ref[...]
    ckvn = ckv * _rms_scale(ckv, MLA_KV_RANK) * kvn_ref[...]
    ckvn_ref[...] = ckvn
    kr_ref[...] = sm_ref[:, 0:MLA_ROPE]
    _mla_kv_rows(ckvn.astype(bf16), _krope_lanes(sm_ref[...]), wuk_ref[...], wuv_ref[...], kg_ref[...],
                 None, k_scr, v_scr, 0)
    _mla_attend(cq_ref[...], wuq_ref[...], qn_ref[...], qg_ref[...], None, k_scr, v_scr, o_ref)


MLA_QB = 256


def _mla_lat_kernel(cq_ref, ckv_ref, sm_ref, cckv_ref, ckr_ref, rq_ref, rk_ref,
                    wuq_ref, wuk_ref, wuv_ref, qn_ref, kvn_ref, qg_ref, kg_ref, o_ref, k_scr, v_scr):
    @pl.when(pl.program_id(1) == 0)
    def _():
        kg = kg_ref[...]
        _mla_kv_rows(cckv_ref[0, 0].astype(bf16), ckr_ref[0, 0], wuk_ref[...], wuv_ref[...], kg,
                     None, k_scr, v_scr, 0)
        for c in range(DEC_SEQ // MLA_QB):
            rows = slice(c * MLA_QB, (c + 1) * MLA_QB)
            ckv = ckv_ref[rows, :]
            ckvn = ckv * _rms_scale(ckv, MLA_KV_RANK) * kvn_ref[...]
            rope = (rk_ref[0, rows, :], rk_ref[1, rows, :], rk_ref[2, rows, :])
            _mla_kv_rows(ckvn.astype(bf16), _krope_lanes(sm_ref[rows, :]), wuk_ref[...], wuv_ref[...], kg,
                         rope, k_scr, v_scr, PAST_LEN + c * MLA_QB)

    rope_q = (rq_ref[0], rq_ref[1], rq_ref[2])
    _mla_attend(cq_ref[...], wuq_ref[...], qn_ref[...], qg_ref[...], rope_q, k_scr, v_scr, o_ref)


def _mla_weight_specs(layer):
    zero = (lambda *a: (0, 0))
    return [_lspec((MLA_Q_RANK, MLA_HEADS * MLA_QK_PAD), layer, zero),
            _lspec((MLA_KV_RANK, MLA_HEADS * MLA_NOPE), layer, zero),
            _lspec((MLA_KV_RANK, MLA_HEADS * MLA_V), layer, zero),
            _lspec((1, MLA_Q_RANK), layer, zero),
            _lspec((1, MLA_KV_RANK), layer, zero),
            _lspec((1, MLA_QK_PAD), layer, zero),
            _lspec((1, MLA_QK_PAD), layer, zero)]


def _mla_weights(w):
    return [w["w_uq"], w["w_uk"], w["w_uv"], w["qn_g"], w["kvn_g"], w["q_g"], w["k_g"]]


def _stacked(prev, n_regular):
    prev = list(prev or [])
    specs = [pl.BlockSpec(memory_space=pl.ANY)] * len(prev)
    return prev, specs, {n_regular + t: 1 + t for t in range(len(prev))}


def _mla_ctx(u, w, layer, prev):
    t = SEQ
    in_specs = [pl.BlockSpec((t, MLA_Q_RANK), lambda i: (i, _blk(C_CQ, MLA_Q_RANK))),
                pl.BlockSpec((t, MLA_KV_RANK), lambda i: (i, _blk(C_CKV, MLA_KV_RANK))),
                pl.BlockSpec((t, LANES), lambda i: (i, _blk(C_SM, LANES)))] + _mla_weight_specs(layer)
    prev, alias_specs, aliases = _stacked(prev, len(in_specs))
    return pl.pallas_call(
        _mla_ctx_kernel,
        grid=(BATCH,),
        in_specs=in_specs + alias_specs,
        out_specs=[pl.BlockSpec((t, MLA_HEADS * MLA_V), lambda i: (i, 0)),
                   pl.BlockSpec((None, t, MLA_KV_RANK), lambda i: (i, layer, 0)),
                   pl.BlockSpec((None, t, MLA_ROPE), lambda i: (i, layer, 0))],
        out_shape=[jax.ShapeDtypeStruct((BATCH * t, MLA_HEADS * MLA_V), bf16),
                   jax.ShapeDtypeStruct((BATCH, DEPTH * t, MLA_KV_RANK), f32),
                   jax.ShapeDtypeStruct((BATCH, DEPTH * t, MLA_ROPE), f32)],
        input_output_aliases=aliases,
        scratch_shapes=[pltpu.VMEM((MLA_HEADS, t, MLA_QK_PAD), bf16),
                        pltpu.VMEM((MLA_HEADS, t, MLA_V), bf16)],
        compiler_params=_cparams(("parallel",), 40),
        name="mla_ctx",
    )(u, u, u, *_mla_weights(w), *prev)


def _mla_lat(u, cache_ckv, cache_kr, rope_tab, w, layer):
    t = DEC_SEQ
    nq = t // MLA_QB
    tk = PAST_LEN + t
    return pl.pallas_call(
        _mla_lat_kernel,
        grid=(DEC_BATCH, nq),
        in_specs=[pl.BlockSpec((MLA_QB, MLA_Q_RANK), lambda s, q: (s * nq + q, _blk(C_CQ, MLA_Q_RANK))),
                  pl.BlockSpec((t, MLA_KV_RANK), lambda s, q: (s, _blk(C_CKV, MLA_KV_RANK))),
                  pl.BlockSpec((t, LANES), lambda s, q: (s, _blk(C_SM, LANES))),
                  pl.BlockSpec((1, 1, PAST_LEN, MLA_KV_RANK), lambda s, q: (s, layer, 0, 0)),
                  pl.BlockSpec((1, 1, PAST_LEN, LANES), lambda s, q: (s, layer, 0, 0)),
                  pl.BlockSpec((3, MLA_QB, LANES), lambda s, q: (0, q, 0)),
                  pl.BlockSpec((3, t, LANES), lambda s, q: (0, 0, 0))] + _mla_weight_specs(layer),
        out_specs=pl.BlockSpec((MLA_QB, MLA_HEADS * MLA_V), lambda s, q: (s * nq + q, 0)),
        out_shape=jax.ShapeDtypeStruct((DEC_BATCH * t, MLA_HEADS * MLA_V), bf16),
        scratch_shapes=[pltpu.VMEM((MLA_HEADS, tk, MLA_QK_PAD), bf16),
                        pltpu.VMEM((MLA_HEADS, tk, MLA_V), bf16)],
        compiler_params=_cparams(("parallel", "arbitrary"), 48),
        name="mla_lat",
    )(u, u, u, cache_ckv, cache_kr, rope_tab, rope_tab, *_mla_weights(w))


CONV_PAD = 8
CONV_ROWS = 256
RECURRENCE_UNROLL = 4


def _conv_silu(x_ref, w_ref, b_ref, xp_ref, out_ref, t):
    c = x_ref.shape[1]
    xp_ref[0:CONV_PAD, :] = jnp.zeros((CONV_PAD, c), f32)
    xp_ref[CONV_PAD:CONV_PAD + t, :] = x_ref[...]
    xp_ref[CONV_PAD + t:2 * CONV_PAD + t, :] = jnp.zeros((CONV_PAD, c), f32)
    half = (CONV_W - 1) // 2
    for r0 in range(0, t, CONV_ROWS):
        acc = None
        for j in range(CONV_W):
            start = CONV_PAD - half + j + r0
            term = w_ref[j:j + 1, :] * xp_ref[start:start + CONV_ROWS, :]
            acc = term if acc is None else acc + term
        if b_ref is not None:
            acc = acc + b_ref[...]
        out_ref[r0:r0 + CONV_ROWS, :] = _silu(acc)


def _unit_tri_inverses(xs):
    n = xs[0].shape[0]
    shape = xs[0].shape
    ii = lax.broadcasted_iota(jnp.int32, shape, 0)
    jj = lax.broadcasted_iota(jnp.int32, shape, 1)
    first_half = jj < n
    eye = jnp.where((jj == ii) | (jj == ii + n), 1.0, 0.0)

    def hi_lo(a):
        hi = a.astype(bf16)
        return hi, (a - hi.astype(f32)).astype(bf16)

    def left(hi, lo):
        return jnp.where(first_half, hi, lo)

    levels = n.bit_length() - 2
    ps = [eye + x for x in xs]
    for j in range(levels + 1):
        first, last = j == 0, j == levels
        x_parts = [hi_lo(x) for x in xs]
        rhs = [jnp.concatenate([h, h, l, l], axis=0) for h, l in x_parts]
        lhs = []
        for p, xp in zip(ps, x_parts):
            blocks = []
            if not first:
                blocks.append(left(*hi_lo(p)))
            if not last:
                blocks.append(left(*xp))
            rows = blocks[0] if len(blocks) == 1 else jnp.concatenate(blocks, axis=0)
            lhs.append(jnp.concatenate([rows, rows], axis=1))
        rs = [_dot(l, r) for l, r in zip(lhs, rhs)]
        if not first:
            ps = [p + r[:n] for p, r in zip(ps, rs)]
        if not last:
            xs = [r[-n:] for r in rs]
    return ps


def _gdn_kernel(*refs, t, hb, cpi, has_s0, want_state, n_alias):
    it = iter(refs)
    q_ref, k_ref, v_ref, z_ref, sm_ref = (next(it) for _ in range(5))
    cwq_ref, cwk_ref, cwv_ref, gp_ref, ng_ref = (next(it) for _ in range(5))
    s0_ref = next(it) if has_s0 else None
    for _ in range(n_alias):
        next(it)
    o_ref = next(it)
    sfin_ref = next(it) if want_state else None
    xp_s, qn_s, kn_s, vn_s, beta_s, cum_s, cumt_s, wq_s, uu_s, kt_s, qk_s, eg_s, od_s, st_s = it

    ck = GDN_CHUNK
    nc = t // ck
    head0 = pl.program_id(1) * hb

    _conv_silu(q_ref, cwq_ref, None, xp_s, qn_s, t)
    _conv_silu(k_ref, cwk_ref, None, xp_s, kn_s, t)
    _conv_silu(v_ref, cwv_ref, None, xp_s, vn_s, t)
    for hh in range(hb):
        cols = slice(hh * GDN_DK, (hh + 1) * GDN_DK)
        qh = qn_s[:, cols]
        qn_s[:, cols] = qh * lax.rsqrt(jnp.sum(qh * qh, axis=-1, keepdims=True) + EPS) * (GDN_DK ** -0.5)
        kh = kn_s[:, cols]
        kn_s[:, cols] = kh * lax.rsqrt(jnp.sum(kh * kh, axis=-1, keepdims=True) + EPS)

    sm = sm_ref[...]
    beta_s[...] = _sigmoid(sm)
    g_all = -jnp.exp(gp_ref[0:1, :]) * _softplus(sm + gp_ref[1:2, :])

    ii = lax.broadcasted_iota(jnp.int32, (ck, LANES), 0)
    jj = lax.broadcasted_iota(jnp.int32, (ck, LANES), 1)
    jj = jnp.where(jj < ck, jj, jj - ck)
    incl = (ii >= jj, ii <= jj)
    strict = (ii > jj, ii < jj)
    tri = (incl[0][:, :ck].astype(bf16), incl[1][:, :ck].astype(bf16))

    for c in range(nc):
        rows = slice(c * ck, (c + 1) * ck)
        for d in range(2):
            cum = _tri_cumsum(tri[d], g_all[rows, :])
            cum_s[d, rows, :] = cum
            cumt_s[d, c] = jnp.concatenate([cum, cum], axis=0).T

    if has_s0:
        for d in range(2):
            for hh in range(hb):
                st_s[d * hb + hh] = s0_ref[0, 0, d, hh]
    else:
        st_s[...] = jnp.zeros(st_s.shape, f32)

    def phase1(it_idx, carry):
        chunks = [it_idx * cpi + e for e in range(cpi)]
        rows = [pl.ds(pl.multiple_of(c * ck, ck), ck) for c in chunks]
        beta_c = [beta_s[r, :] for r in rows]
        heads = [(e, hh) for e in range(cpi) for hh in range(hb)]
        kc, qc, vc, g_kk, g_qk = {}, {}, {}, {}, {}
        for e, hh in heads:
            cols = slice(hh * GDN_DK, (hh + 1) * GDN_DK)
            kc[e, hh] = kn_s[rows[e], cols]
            qc[e, hh] = qn_s[rows[e], cols]
            vc[e, hh] = vn_s[rows[e], cols]
        for key in heads:
            kcb = kc[key].astype(bf16)
            kc_dup = jnp.concatenate([kcb, kcb], axis=0)
            g_kk[key] = _dot_nt(kcb, kc_dup)
            g_qk[key] = _dot_nt(qc[key].astype(bf16), kc_dup)
        probs = [(e, hh, d) for e, hh in heads for d in range(2)]
        col, bcol, dec, gl, neg_a = {}, {}, {}, {}, []
        for e, hh, d in probs:
            key = (e, hh, d)
            lane_g = SM_A + d * GDN_HEADS + head0 + hh
            lane_b = SM_BETA + d * GDN_HEADS + head0 + hh
            cum_c = cum_s[d, rows[e], :]
            col[key] = _lane_col(cum_c, lane_g)
            bcol[key] = _lane_col(beta_c[e], lane_b)
            row = cumt_s[d, chunks[e], pl.ds(lane_g, 1), :]
            dec[key] = jnp.where(incl[d], jnp.exp(col[key] - row), 0.0)
            end = cum_c[ck - 1:ck, :] if d == 0 else cum_c[0:1, :]
            gl[key] = _lane_col(end, lane_g)
            neg_a.append(jnp.where(strict[d], -(bcol[key] * g_kk[e, hh] * dec[key]), 0.0))
        tms = _unit_tri_inverses(neg_a)
        ecol, rhs = {}, []
        for key in probs:
            e, hh, d = key
            ecol[key] = jnp.exp(col[key])
            rhs.append(jnp.concatenate([kc[e, hh] * (bcol[key] * ecol[key]), vc[e, hh] * bcol[key]],
                                       axis=-1).astype(bf16))
        wus = [_dot(tm[:, :ck].astype(bf16), r) for tm, r in zip(tms, rhs)]
        for key, wu in zip(probs, wus):
            e, hh, d = key
            idx = d * hb + hh
            r2 = pl.multiple_of(chunks[e] * 2 * ck, 2 * ck)
            wq_s[idx, pl.ds(r2, ck), :] = wu[:, :GDN_DK].astype(bf16)
            wq_s[idx, pl.ds(r2 + ck, ck), :] = (qc[e, hh] * ecol[key]).astype(bf16)
            uu_s[idx, rows[e], :] = wu[:, GDN_DK:]
            kt_s[idx, rows[e], :] = (kc[e, hh] * jnp.exp(gl[key] - col[key])).astype(bf16)
            qk_s[idx, rows[e], :] = jnp.where(incl[d], g_qk[e, hh] * dec[key], 0.0)[:, :ck].astype(bf16)
            eg_s[idx, pl.ds(chunks[e], 1), :] = jnp.broadcast_to(jnp.exp(gl[key]), (1, LANES))
        return carry

    lax.fori_loop(0, nc // cpi, phase1, 0, unroll=nc // cpi <= 2)

    def phase2(i, carry):
        probs = [(d, hh) for d in range(2) for hh in range(hb)]
        chunk = {0: i, 1: nc - 1 - i}
        rows = {d: pl.ds(pl.multiple_of(chunk[d] * ck, ck), ck) for d in range(2)}
        rows2 = {d: pl.ds(pl.multiple_of(chunk[d] * 2 * ck, 2 * ck), 2 * ck) for d in range(2)}
        s = [st_s[d * hb + hh] for d, hh in probs]
        ws = [_dot(wq_s[d * hb + hh, rows2[d], :], sv.astype(bf16)) for (d, hh), sv in zip(probs, s)]
        vb = [(uu_s[d * hb + hh, rows[d], :] - w[:ck]).astype(bf16) for (d, hh), w in zip(probs, ws)]
        o = [w[ck:] + _dot(qk_s[d * hb + hh, rows[d], :], v) for (d, hh), w, v in zip(probs, ws, vb)]
        ds = [_dot_tn(kt_s[d * hb + hh, rows[d], :], v) for (d, hh), v in zip(probs, vb)]
        for (d, hh), sv, dv, ov in zip(probs, s, ds, o):
            idx = d * hb + hh
            st_s[idx] = sv * eg_s[idx, pl.ds(chunk[d], 1), :] + dv
            od_s[d, rows[d], hh * GDN_DV:(hh + 1) * GDN_DV] = ov
        return carry

    lax.fori_loop(0, nc, phase2, 0, unroll=min(nc, RECURRENCE_UNROLL))

    for hh in range(hb):
        cols = slice(hh * GDN_DV, (hh + 1) * GDN_DV)
        o = od_s[0, :, cols] + od_s[1, :, cols]
        on = o * lax.rsqrt(jnp.mean(o * o, axis=-1, keepdims=True) + EPS) * ng_ref[...]
        o_ref[:, cols] = (on * _silu(z_ref[:, cols])).astype(o_ref.dtype)
    if want_state:
        for d in range(2):
            for hh in range(hb):
                sfin_ref[0, d, hh] = st_s[d * hb + hh]


def _gdn(u, wl, *, nseq, t, hb, cpi, s0, layer, want_state, prev=None):
    w = hb * GDN_DK
    nhb = GDN_HEADS // hb
    nc = t // GDN_CHUNK
    assert nc % cpi == 0
    gq, gk, gv = C_GQKV, C_GQKV + GDN_HEADS * GDN_DK, C_GQKV + 2 * GDN_HEADS * GDN_DK
    in_specs = [pl.BlockSpec((t, w), lambda i, j: (i, _blk(gq, w) + j)),
                pl.BlockSpec((t, w), lambda i, j: (i, _blk(gk, w) + j)),
                pl.BlockSpec((t, w), lambda i, j: (i, _blk(gv, w) + j)),
                pl.BlockSpec((t, w), lambda i, j: (i, _blk(C_GZ, w) + j)),
                pl.BlockSpec((t, LANES), lambda i, j: (i, _blk(C_SM, LANES))),
                _lspec((8, w), layer, lambda i, j: (0, j)),
                _lspec((8, w), layer, lambda i, j: (0, nhb + j)),
                _lspec((8, w), layer, lambda i, j: (0, 2 * nhb + j)),
                _lspec((8, LANES), layer, lambda i, j: (0, 0)),
                _lspec((1, GDN_DV), layer, lambda i, j: (0, 0))]
    args = [u, u, u, u, u, wl["gdn_cw"], wl["gdn_cw"], wl["gdn_cw"], wl["gdn_gp"], wl["gdn_ng"]]
    if s0 is not None:
        in_specs.append(pl.BlockSpec((1, 1, 2, hb, GDN_DK, GDN_DV), lambda i, j: (i, layer, 0, j, 0, 0)))
        args.append(s0)
    prev, alias_specs, aliases = _stacked(prev, len(in_specs))
    out_specs = [pl.BlockSpec((t, w), lambda i, j: (i, j))]
    out_shape = [jax.ShapeDtypeStruct((nseq * t, GDN_HEADS * GDN_DV), bf16)]
    if want_state:
        out_specs.append(pl.BlockSpec((1, None, 2, hb, GDN_DK, GDN_DV), lambda i, j: (i, layer, 0, j, 0, 0)))
        out_shape.append(jax.ShapeDtypeStruct((nseq, DEPTH, 2, GDN_HEADS, GDN_DK, GDN_DV), f32))
    scratch = [pltpu.VMEM((t + 2 * CONV_PAD, w), f32),
               pltpu.VMEM((t, w), f32), pltpu.VMEM((t, w), f32), pltpu.VMEM((t, w), f32),
               pltpu.VMEM((t, LANES), f32),
               pltpu.VMEM((2, t, LANES), f32),
               pltpu.VMEM((2, nc, LANES, LANES), f32),
               pltpu.VMEM((2 * hb, 2 * t, GDN_DK), bf16),
               pltpu.VMEM((2 * hb, t, GDN_DV), f32),
               pltpu.VMEM((2 * hb, t, GDN_DK), bf16),
               pltpu.VMEM((2 * hb, t, GDN_CHUNK), bf16),
               pltpu.VMEM((2 * hb, max(nc, 8), LANES), f32),
               pltpu.VMEM((2, t, w), f32),
               pltpu.VMEM((2 * hb, GDN_DK, GDN_DV), f32)]
    kern = functools.partial(_gdn_kernel, t=t, hb=hb, cpi=cpi, has_s0=s0 is not None, want_state=want_state,
                             n_alias=len(prev))
    return pl.pallas_call(
        kern,
        grid=(nseq, nhb),
        in_specs=in_specs + alias_specs,
        out_specs=out_specs,
        out_shape=out_shape,
        input_output_aliases=aliases,
        scratch_shapes=scratch,
        compiler_params=_cparams(("parallel", "parallel"), 48),
        name="gdn_" + ("lat" if s0 is not None else "ctx"),
    )(*args, *prev)


def _ssd_kernel(*refs, t, cpi, has_s0, want_state, n_alias):
    it = iter(refs)
    x_ref, bc_ref, z_ref, sm_ref = (next(it) for _ in range(4))
    cwx_ref, cwbc_ref, cbx_ref, cbbc_ref = (next(it) for _ in range(4))
    gp_ref, ex_ref, dv_ref, ng_ref = (next(it) for _ in range(4))
    s0_ref = next(it) if has_s0 else None
    for _ in range(n_alias):
        next(it)
    o_ref = next(it)
    sfin_ref = next(it) if want_state else None
    xp_s, xs_s, bcs_s, dt_s, da_s, y_s, ea_s, stc_s, cd_s, st_s = it

    ck = SSD_CHUNK
    nc = t // ck
    wg = HEAD_PER_GROUP * SSD_P
    nb = SSD_GROUPS * SSD_N

    _conv_silu(x_ref, cwx_ref, cbx_ref, xp_s, xs_s, t)
    _conv_silu(bc_ref, cwbc_ref, cbbc_ref, xp_s, bcs_s, t)

    dt_all = _softplus(sm_ref[...] + gp_ref[1:2, :])
    dt_s[...] = dt_all
    da_s[...] = dt_all * (-jnp.exp(gp_ref[0:1, :]))

    if has_s0:
        for d in range(2):
            s0 = jnp.concatenate([s0_ref[0, 0, d, h] for h in range(SSD_HEADS)], axis=0)
            st_s[d] = s0.T
    else:
        st_s[...] = jnp.zeros(st_s.shape, f32)

    ii = lax.broadcasted_iota(jnp.int32, (ck, ck), 0)
    jj = lax.broadcasted_iota(jnp.int32, (ck, ck), 1)
    incl = (ii >= jj, ii <= jj)
    tri = (incl[0].astype(bf16), incl[1].astype(bf16))
    lane = lax.broadcasted_iota(jnp.int32, (ck, LANES), 1)
    end_rows = 16
    dirs = (0, 1)

    def phase1(it_idx, carry):
        chunks = [it_idx * cpi + e for e in range(cpi)]
        rows = [pl.ds(pl.multiple_of(c * ck, ck), ck) for c in chunks]
        probs = [(e, d) for e in range(cpi) for d in dirs]
        dt_c = [dt_s[r, :] for r in rows]
        da_c = [da_s[r, :] for r in rows]
        cum = {(e, d): _tri_cumsum(tri[d], da_c[e]) for e, d in probs}
        cum_t = {key: cum[key].T for key in probs}
        dt_t = [v.T for v in dt_c]
        bcc = [bcs_s[r, :].astype(bf16) for r in rows]
        xc = [xs_s[r, :] for r in rows]
        xb = [v.astype(bf16) for v in xc]
        cb = {(e, g): _dot_nt(bcc[e][:, nb + g * SSD_N:nb + (g + 1) * SSD_N], bcc[e][:, g * SSD_N:(g + 1) * SSD_N])
              for e in range(cpi) for g in range(SSD_GROUPS)}
        heads = [(e, d, h) for e, d in probs for h in range(SSD_HEADS)]
        ms = []
        for e, d, h in heads:
            ln = SM_DT + d * SSD_HEADS + h
            col = _lane_col(cum[e, d], ln)
            row = cum_t[e, d][ln:ln + 1, :]
            dtrow = dt_t[e][ln:ln + 1, :]
            lm = jnp.where(incl[d], jnp.exp(col - row), 0.0)
            ms.append((cb[e, h // HEAD_PER_GROUP] * lm * dtrow).astype(bf16))
        ys = {key: _dot(m, xb[key[0]][:, (key[2] // 2) * LANES:(key[2] // 2 + 1) * LANES])
              for key, m in zip(heads, ms)}
        end = {(e, d): cum[e, d][ck - 1:ck, :] if d == 0 else cum[e, d][0:1, :] for e, d in probs}
        spread = {key: _select_dot(jnp.concatenate([jnp.exp(end[key] - cum[key]) * dt_c[key[0]], jnp.exp(cum[key]),
                                                    jnp.broadcast_to(jnp.exp(end[key]), (end_rows, LANES))],
                                                   axis=0), ex_ref[key[1]]) for key in probs}
        xsc = {key: (xc[key[0]] * spread[key][:ck]).astype(bf16) for key in probs}
        st_c = {(e, d, g): _dot_tn(bcc[e][:, g * SSD_N:(g + 1) * SSD_N], xsc[e, d][:, g * wg:(g + 1) * wg])
                for e, d in probs for g in range(SSD_GROUPS)}
        for e, d in probs:
            parts = [jnp.where(lane < SSD_P, ys[e, d, 2 * pr], ys[e, d, 2 * pr + 1]) for pr in range(SSD_HEADS // 2)]
            y_s[d, rows[e], :] = jnp.concatenate(parts, axis=-1)
            ea_s[d, rows[e], :] = spread[e, d][ck:2 * ck]
            cd_s[d, chunks[e]] = spread[e, d][2 * ck:2 * ck + 8]
            stc_s[d, chunks[e]] = jnp.concatenate([st_c[e, d, g] for g in range(SSD_GROUPS)], axis=-1)
        return carry

    lax.fori_loop(0, nc // cpi, phase1, 0)

    def phase2(i, carry):
        chunk = (i, nc - 1 - i)
        rows = [pl.ds(pl.multiple_of(c * ck, ck), ck) for c in chunk]
        st = [st_s[d] for d in dirs]
        stb = [v.astype(bf16) for v in st]
        y_off = {(d, g): _dot(bcs_s[rows[d], nb + g * SSD_N:nb + (g + 1) * SSD_N].astype(bf16),
                              stb[d][:, g * wg:(g + 1) * wg]) for d in dirs for g in range(SSD_GROUPS)}
        for d in dirs:
            off = jnp.concatenate([y_off[d, g] for g in range(SSD_GROUPS)], axis=-1)
            y_s[d, rows[d], :] = y_s[d, rows[d], :] + off * ea_s[d, rows[d], :]
            st_s[d] = st[d] * cd_s[d, chunk[d], 0:1, :] + stc_s[d, chunk[d]]
        return carry

    lax.fori_loop(0, nc, phase2, 0, unroll=min(nc, RECURRENCE_UNROLL))

    y = y_s[0] + y_s[1] + xs_s[...] * dv_ref[...]
    y = y * _silu(z_ref[...])
    for g in range(SSD_GROUPS):
        cols = slice(g * wg, (g + 1) * wg)
        yg = y[:, cols]
        o_ref[:, cols] = (yg * lax.rsqrt(jnp.mean(yg * yg, axis=-1, keepdims=True) + EPS)
                          * ng_ref[:, cols]).astype(o_ref.dtype)
    if want_state:
        for d in range(2):
            stt = st_s[d].T
            for h in range(SSD_HEADS):
                sfin_ref[0, d, h] = stt[h * SSD_P:(h + 1) * SSD_P, :]


def _ssd(u, wl, *, nseq, t, cpi, s0, layer, want_state, prev=None):
    wi, wbc = SSD_INNER, 2 * SSD_GROUPS * SSD_N
    nc = t // SSD_CHUNK
    assert nc % cpi == 0 and wi == wbc
    in_specs = [pl.BlockSpec((t, wi), lambda i: (i, _blk(C_XBC, wi))),
                pl.BlockSpec((t, wbc), lambda i: (i, _blk(C_XBC + wi, wbc))),
                pl.BlockSpec((t, wi), lambda i: (i, _blk(C_SZ, wi))),
                pl.BlockSpec((t, LANES), lambda i: (i, _blk(C_SM, LANES))),
                _lspec((8, wi), layer, lambda i: (0, 0)),
                _lspec((8, wbc), layer, lambda i: (0, 1)),
                _lspec((1, wi), layer, lambda i: (0, 0)),
                _lspec((1, wbc), layer, lambda i: (0, 1)),
                _lspec((8, LANES), layer, lambda i: (0, 0)),
                pl.BlockSpec((2, LANES, wi), lambda i: (0, 0, 0)),
                _lspec((1, wi), layer, lambda i: (0, 0)),
                _lspec((1, wi), layer, lambda i: (0, 0))]
    args = [u, u, u, u, wl["ssd_cw"], wl["ssd_cw"], wl["ssd_cb"], wl["ssd_cb"],
            wl["ssd_gp"], wl["ssd_ex"], wl["ssd_dv"], wl["ssd_ng"]]
    if s0 is not None:
        in_specs.append(pl.BlockSpec((1, 1, 2, SSD_HEADS, SSD_P, SSD_N), lambda i: (i, layer, 0, 0, 0, 0)))
        args.append(s0)
    prev, alias_specs, aliases = _stacked(prev, len(in_specs))
    out_specs = [pl.BlockSpec((t, wi), lambda i: (i, 0))]
    out_shape = [jax.ShapeDtypeStruct((nseq * t, wi), bf16)]
    if want_state:
        out_specs.append(pl.BlockSpec((1, None, 2, SSD_HEADS, SSD_P, SSD_N), lambda i: (i, layer, 0, 0, 0, 0)))
        out_shape.append(jax.ShapeDtypeStruct((nseq, DEPTH, 2, SSD_HEADS, SSD_P, SSD_N), f32))
    scratch = [pltpu.VMEM((t + 2 * CONV_PAD, wi), f32),
               pltpu.VMEM((t, wi), f32), pltpu.VMEM((t, wbc), f32),
               pltpu.VMEM((t, LANES), f32), pltpu.VMEM((t, LANES), f32),
               pltpu.VMEM((2, t, wi), f32),
               pltpu.VMEM((2, t, wi), f32),
               pltpu.VMEM((2, nc, SSD_N, wi), f32),
               pltpu.VMEM((2, nc, 8, wi), f32),
               pltpu.VMEM((2, SSD_N, wi), f32)]
    kern = functools.partial(_ssd_kernel, t=t, cpi=cpi, has_s0=s0 is not None, want_state=want_state,
                             n_alias=len(prev))
    return pl.pallas_call(
        kern,
        grid=(nseq,),
        in_specs=in_specs + alias_specs,
        out_specs=out_specs,
        out_shape=out_shape,
        input_output_aliases=aliases,
        scratch_shapes=scratch,
        compiler_params=_cparams(("parallel",), 48),
        name="ssd_" + ("lat" if s0 is not None else "ctx"),
    )(*args, *prev)


def _prep_weights(p):
    w_in = p["w_in"]
    o_g = MLA_Q_RANK + MLA_KV_RANK + MLA_ROPE
    o_s = o_g + 2 * GDN_HEADS * GDN_DK + 2 * GDN_HEADS * GDN_DV + 4 * GDN_HEADS
    n_qkv = 2 * GDN_HEADS * GDN_DK + GDN_HEADS * GDN_DV
    n_gz = GDN_HEADS * GDN_DV
    sl = lambda a, b: w_in[:, :, a:b]
    zeros = lambda n: jnp.zeros(w_in.shape[:2] + (n,), w_in.dtype)
    n_xbc = SSD_INNER + 2 * SSD_GROUPS * SSD_N
    parts = [sl(0, MLA_Q_RANK),
             sl(MLA_Q_RANK + MLA_KV_RANK, o_g),
             sl(o_g + n_qkv + n_gz, o_g + n_qkv + n_gz + 4 * GDN_HEADS),
             sl(o_s + SSD_INNER + n_xbc, o_s + SSD_INNER + n_xbc + 2 * SSD_HEADS),
             zeros(C_XBC - (C_SM + MLA_ROPE + 4 * GDN_HEADS + 2 * SSD_HEADS)),
             sl(o_s + SSD_INNER, o_s + SSD_INNER + n_xbc),
             sl(MLA_Q_RANK, MLA_Q_RANK + MLA_KV_RANK),
             sl(o_g + n_qkv, o_g + n_qkv + n_gz),
             sl(o_g, o_g + n_qkv),
             sl(o_s, o_s + SSD_INNER)]
    w_in_p = jnp.concatenate(parts, axis=-1).astype(bf16)
    assert w_in_p.shape[-1] == N_IN

    w_uq = p["mla_w_uq"].reshape(DEPTH, MLA_Q_RANK, MLA_HEADS, MLA_QK)
    w_uq = jnp.pad(w_uq, ((0, 0), (0, 0), (0, 0), (0, MLA_QK_PAD - MLA_QK)))
    w_uq = w_uq.reshape(DEPTH, MLA_Q_RANK, MLA_HEADS * MLA_QK_PAD).astype(bf16)
    w_ukv = p["mla_w_ukv"].reshape(DEPTH, MLA_KV_RANK, MLA_HEADS, MLA_NOPE + MLA_V)
    w_uk = w_ukv[..., :MLA_NOPE].reshape(DEPTH, MLA_KV_RANK, MLA_HEADS * MLA_NOPE).astype(bf16)
    w_uv = w_ukv[..., MLA_NOPE:].reshape(DEPTH, MLA_KV_RANK, MLA_HEADS * MLA_V).astype(bf16)
    pad_g = lambda g: jnp.pad(g, ((0, 0), (0, MLA_QK_PAD - MLA_QK)))[:, None, :]

    def lane_rows(a_log, dt_bias, lane0):
        n = a_log.shape[1] * a_log.shape[2]
        rows = jnp.stack([a_log.reshape(DEPTH, n), dt_bias.reshape(DEPTH, n)], axis=1)
        return jnp.pad(rows.astype(f32), ((0, 0), (0, 6), (lane0, LANES - lane0 - n)))

    ex = np.zeros((2, LANES, SSD_INNER), np.float32)
    for d in range(2):
        for h in range(SSD_HEADS):
            ex[d, SM_DT + d * SSD_HEADS + h, h * SSD_P:(h + 1) * SSD_P] = 1.0

    pad_rows = lambda w: jnp.pad(w.astype(f32), ((0, 0), (0, 8 - CONV_W), (0, 0)))
    return dict(
        w_in=w_in_p, w_uq=w_uq, w_uk=w_uk, w_uv=w_uv,
        qn_g=p["mla_qnorm_g"][:, None, :], kvn_g=p["mla_kvnorm_g"][:, None, :],
        q_g=pad_g(p["mla_q_g"]), k_g=pad_g(p["mla_k_g"]),
        w_out=p["w_out"].astype(bf16), w_gu=p["ffn_w_gu"].astype(bf16), w_down=p["ffn_w_down"].astype(bf16),
        norm1_g=p["norm1_g"][:, None, :], norm2_g=p["norm2_g"][:, None, :],
        gdn_cw=pad_rows(p["gdn_conv_w"]), gdn_gp=lane_rows(p["gdn_a_log"], p["gdn_dt_bias"], SM_A),
        gdn_ng=p["gdn_norm_g"][:, None, :],
        ssd_cw=pad_rows(p["ssd_conv_w"]), ssd_cb=p["ssd_conv_b"][:, None, :],
        ssd_gp=lane_rows(p["ssd_a_log"], p["ssd_dt_bias"], SM_DT), ssd_ex=jnp.asarray(ex, dtype=bf16),
        ssd_dv=jnp.repeat(p["ssd_d"], SSD_P, axis=1)[:, None, :], ssd_ng=p["ssd_norm_g"][:, None, :],
    )


def _rope_tables(n_tokens):
    rows = n_tokens // GRID_W
    row = jnp.repeat(jnp.arange(rows, dtype=f32), GRID_W)
    col = jnp.tile(jnp.arange(GRID_W, dtype=f32), rows)
    inv = ROPE_BASE ** (-jnp.arange(ROPE_F, dtype=f32) / ROPE_F)
    ar, ac = row[:, None] * inv, col[:, None] * inv
    zero = jnp.zeros_like(ar)
    tail = jnp.zeros((n_tokens, LANES - MLA_ROPE), f32)
    cos = jnp.concatenate([jnp.cos(ar), jnp.cos(ar), jnp.cos(ac), jnp.cos(ac), tail], axis=-1)
    s_up = jnp.concatenate([-jnp.sin(ar), zero, -jnp.sin(ac), zero, tail], axis=-1)
    s_dn = jnp.concatenate([zero, jnp.sin(ar), zero, jnp.sin(ac), tail], axis=-1)
    return jnp.stack([cos, s_up, s_dn], axis=0)


FFN_TF = 512
DENSE_TM = 1024
IN_TN = 1024
OUT_TM = 512
DOWN_TN = 512


def _trunk_layer(x, mods, wl, l, *, nseq, t, latent, cache=None, stacked=None):
    cond = dict(layer=l, cond0=1 if latent else 0, ncond=nseq if latent else 1)
    u = _norm_mm(x, mods, wl["norm1_g"], wl["w_in"], shift_row=0, scale_row=1, swiglu=False,
                 tm=DENSE_TM, tn=IN_TN, out_dtype=f32, name="in_proj", **cond)
    if latent:
        o_mla = _mla_lat(u, cache["ckv"], cache["krope"], cache["rope"], wl, l)
        o_gdn, = _gdn(u, wl, nseq=nseq, t=t, hb=2, cpi=2, s0=cache["gdn"], layer=l, want_state=False)
        o_ssd, = _ssd(u, wl, nseq=nseq, t=t, cpi=2, s0=cache["ssd"], layer=l, want_state=False)
    else:
        prev = (lambda a, b: None) if stacked is None else (lambda a, b: stacked[a:b])
        o_mla, ckv_all, kr_all = _mla_ctx(u, wl, l, prev(0, 2))
        o_gdn, sg_all = _gdn(u, wl, nseq=nseq, t=t, hb=GDN_HEADS, cpi=2, s0=None, layer=l, want_state=True,
                             prev=prev(2, 3))
        o_ssd, ss_all = _ssd(u, wl, nseq=nseq, t=t, cpi=2, s0=None, layer=l, want_state=True, prev=prev(3, 4))
        stacked = (ckv_all, kr_all, sg_all, ss_all)
    x = _mm_res([o_mla, o_gdn, o_ssd], wl["w_out"], x, mods, gate_row=2, tm=OUT_TM, tn=D_MODEL,
                vmem_mb=48, name="out_proj", **cond)
    act = _norm_mm(x, mods, wl["norm2_g"], wl["w_gu"], shift_row=3, scale_row=4, swiglu=True,
                   tm=DENSE_TM, tn=FFN_TF, out_dtype=bf16, name="ffn_gu", **cond)
    x = _mm_res([act], wl["w_down"], x, mods, gate_row=5, tm=DENSE_TM, tn=DOWN_TN, vmem_mb=56,
                name="ffn_down", **cond)
    return x, stacked


def kernel(x_prompt, x_sample, cache_mla_ckv, cache_mla_krope, state_gdn, state_ssd, c, c_ctx, norm1_g, norm2_g, ada_w, ada_b, w_in, w_out, mla_qnorm_g, mla_w_uq, mla_kvnorm_g, mla_w_ukv, mla_q_g, mla_k_g, gdn_conv_w, gdn_a_log, gdn_dt_bias, gdn_norm_g, ssd_conv_w, ssd_conv_b, ssd_a_log, ssd_dt_bias, ssd_d, ssd_norm_g, ffn_w_gu, ffn_w_down):
    p = dict(norm1_g=norm1_g, norm2_g=norm2_g, w_in=w_in, w_out=w_out,
             mla_qnorm_g=mla_qnorm_g, mla_w_uq=mla_w_uq, mla_kvnorm_g=mla_kvnorm_g,
             mla_w_ukv=mla_w_ukv, mla_q_g=mla_q_g, mla_k_g=mla_k_g, gdn_conv_w=gdn_conv_w,
             gdn_a_log=gdn_a_log, gdn_dt_bias=gdn_dt_bias, gdn_norm_g=gdn_norm_g,
             ssd_conv_w=ssd_conv_w, ssd_conv_b=ssd_conv_b, ssd_a_log=ssd_a_log,
             ssd_dt_bias=ssd_dt_bias, ssd_d=ssd_d, ssd_norm_g=ssd_norm_g,
             ffn_w_gu=ffn_w_gu, ffn_w_down=ffn_w_down)
    w = _prep_weights(p)

    cvec = jnp.concatenate([c_ctx[None, :], c, jnp.zeros((8 - 1 - DEC_BATCH, D_MODEL), f32)], axis=0)
    mods = _mods(cvec, ada_w, ada_b).reshape(DEPTH, 8, 6, D_MODEL)

    cache = dict(ckv=cache_mla_ckv,
                 krope=jnp.pad(cache_mla_krope, ((0, 0), (0, 0), (0, 0), (0, LANES - MLA_ROPE))),
                 rope=_rope_tables(DEC_SEQ), gdn=state_gdn, ssd=state_ssd)

    xp = x_prompt.reshape(BATCH * SEQ, D_MODEL)
    xs = x_sample.reshape(DEC_BATCH * DEC_SEQ, D_MODEL)
    stacked = None
    for l in range(DEPTH):
        xp, stacked = _trunk_layer(xp, mods, w, l, nseq=BATCH, t=SEQ, latent=False, stacked=stacked)
        xs, _ = _trunk_layer(xs, mods, w, l, nseq=DEC_BATCH, t=DEC_SEQ, latent=True, cache=cache)
    ckv_all, kr_all, sg_all, ss_all = stacked
    return (xp.reshape(BATCH, SEQ, D_MODEL), xs.reshape(DEC_BATCH, DEC_SEQ, D_MODEL),
            ckv_all.reshape(BATCH, DEPTH, SEQ, MLA_KV_RANK), kr_all.reshape(BATCH, DEPTH, SEQ, MLA_ROPE),
            sg_all, ss_all)
```

```python
import functools

import numpy as np
import jax
import jax.numpy as jnp
from jax import lax
from jax.experimental import pallas as pl
from jax.experimental.pallas import tpu as pltpu

f32 = jnp.float32
bf16 = jnp.bfloat16

D_MODEL = 2048
BATCH = 32
SEQ = 256
DEPTH = 4
DEC_BATCH = 2
DEC_SEQ = 1024
PAST_LEN = 256
GRID_W = 64
MLA_HEADS = 8
MLA_NOPE = 128
MLA_ROPE = 64
MLA_QK = MLA_NOPE + MLA_ROPE
MLA_V = 128
MLA_Q_RANK = 768
MLA_KV_RANK = 512
ROPE_F = MLA_ROPE // 4
ROPE_BASE = 10000.0
GDN_HEADS = 4
GDN_DK = 128
GDN_DV = 128
GDN_CHUNK = 64
SSD_HEADS = 8
SSD_P = 64
SSD_GROUPS = 2
SSD_N = 128
SSD_CHUNK = 128
SSD_INNER = SSD_HEADS * SSD_P
CONV_W = 5
EPS = 1e-6

LANES = 128
MLA_QK_PAD = 256
HEAD_PER_GROUP = SSD_HEADS // SSD_GROUPS

C_CQ = 0
C_SM = 768
C_XBC = 1024
C_CKV = 2048
C_GZ = 2560
C_GQKV = 3072
C_SZ = 4608
N_IN = 5120
SM_BETA = 64
SM_A = 72
SM_DT = 80

VMEM_MB = 1024 * 1024


def _cparams(sem, vmem_mb):
    return pltpu.CompilerParams(dimension_semantics=sem, vmem_limit_bytes=vmem_mb * VMEM_MB)


def _blk(off, width):
    assert off % width == 0
    return off // width


def _sigmoid(x):
    return 1.0 / (1.0 + jnp.exp(-x))


def _silu(x):
    return x * _sigmoid(x)


def _softplus(x):
    return jnp.maximum(x, 0.0) + jnp.log(1.0 + jnp.exp(-jnp.abs(x)))


def _rms_scale(x, n):
    return lax.rsqrt(jnp.sum(x * x, axis=-1, keepdims=True) / n + EPS)


def _lane_col(a, lane_idx):
    lane = lax.broadcasted_iota(jnp.int32, a.shape, 1)
    return jnp.sum(jnp.where(lane == lane_idx, a, 0.0), axis=1, keepdims=True)


def _dot(a, b):
    return jnp.dot(a, b, preferred_element_type=f32)


def _dot_nt(a, b):
    return lax.dot_general(a, b, (((1,), (1,)), ((), ())), preferred_element_type=f32)


def _dot_tn(a, b):
    return lax.dot_general(a, b, (((0,), (0,)), ((), ())), preferred_element_type=f32)


def _tri_cumsum(tri_bf, a):
    hi = a.astype(bf16)
    r1 = a - hi.astype(f32)
    mid = r1.astype(bf16)
    lo = (r1 - mid.astype(f32)).astype(bf16)
    r = _dot(tri_bf, jnp.concatenate([hi, mid, lo], axis=1))
    return (r[:, 2 * LANES:] + r[:, LANES:2 * LANES]) + r[:, :LANES]


def _select_dot(a, sel_bf):
    n = a.shape[0]
    hi = a.astype(bf16)
    lo = (a - hi.astype(f32)).astype(bf16)
    r = _dot(jnp.concatenate([hi, lo], axis=0), sel_bf)
    return r[:n] + r[n:]


def _mods_kernel(c_ref, w_ref, b_ref, o_ref):
    s = _silu(c_ref[...]).astype(bf16)
    o_ref[0] = _dot(s, w_ref[0].astype(bf16)) + b_ref[0]


def _mods(cvec8, ada_w, ada_b):
    tn = 1024
    n = 6 * D_MODEL
    return pl.pallas_call(
        _mods_kernel,
        grid=(DEPTH, n // tn),
        in_specs=[pl.BlockSpec((8, D_MODEL), lambda l, j: (0, 0)),
                  pl.BlockSpec((1, D_MODEL, tn), lambda l, j: (l, 0, j)),
                  pl.BlockSpec((1, 1, tn), lambda l, j: (l, 0, j))],
        out_specs=pl.BlockSpec((1, 8, tn), lambda l, j: (l, 0, j)),
        out_shape=jax.ShapeDtypeStruct((DEPTH, 8, n), f32),
        compiler_params=_cparams(("parallel", "parallel"), 40),
        name="adaln_mods",
    )(cvec8, ada_w, ada_b.reshape(DEPTH, 1, n))


NORM_ROWS = 256


def _norm_mm_kernel(x_ref, m_ref, g_ref, *rest, shift_row, scale_row, swiglu):
    w_refs, (o_ref, h_ref) = rest[:-2], rest[-2:]
    tm = x_ref.shape[0]

    @pl.when(pl.program_id(1) == 0)
    def _():
        shift = m_ref[shift_row:shift_row + 1, :]
        scale1p = 1.0 + m_ref[scale_row:scale_row + 1, :]
        g = g_ref[...]

        def body(r, carry):
            sl = pl.ds(pl.multiple_of(r * NORM_ROWS, NORM_ROWS), NORM_ROWS)
            x = x_ref[sl, :]
            y = x * lax.rsqrt(jnp.mean(x * x, axis=-1, keepdims=True) + EPS) * g
            h_ref[sl, :] = (y * scale1p + shift).astype(bf16)
            return carry

        lax.fori_loop(0, tm // NORM_ROWS, body, 0)

    h = h_ref[...]
    if swiglu:
        gate, up = _dot(h, w_refs[0][...]), _dot(h, w_refs[1][...])
        o_ref[...] = (_silu(gate) * up).astype(o_ref.dtype)
    else:
        o_ref[...] = _dot(h, w_refs[0][...]).astype(o_ref.dtype)


def _lspec(block, layer, idx):
    return pl.BlockSpec((None,) + tuple(block), lambda *g: (layer,) + tuple(idx(*g)))


def _mods_spec(width, layer, cond0, rows_per_cond, tm, col):
    return pl.BlockSpec((None, None, 6, width),
                        lambda i, j: (layer, cond0 + (i * tm) // rows_per_cond, 0, j if col else 0))


def _norm_mm(x, mods, g, w, *, layer, cond0, ncond, shift_row, scale_row, swiglu, tm, tn, out_dtype, name):
    m_rows, k = x.shape
    n_out = w.shape[2] // (2 if swiglu else 1)
    n_tiles = n_out // tn
    w_specs = [_lspec((k, tn), layer, lambda i, j: (0, j))]
    if swiglu:
        w_specs.append(_lspec((k, tn), layer, lambda i, j: (0, n_tiles + j)))
    kern = functools.partial(_norm_mm_kernel, shift_row=shift_row, scale_row=scale_row, swiglu=swiglu)
    return pl.pallas_call(
        kern,
        grid=(m_rows // tm, n_tiles),
        in_specs=[pl.BlockSpec((tm, k), lambda i, j: (i, 0)),
                  _mods_spec(k, layer, cond0, m_rows // ncond, tm, False),
                  _lspec((1, k), layer, lambda i, j: (0, 0))] + w_specs,
        out_specs=pl.BlockSpec((tm, tn), lambda i, j: (i, j)),
        out_shape=jax.ShapeDtypeStruct((m_rows, n_out), out_dtype),
        scratch_shapes=[pltpu.VMEM((tm, k), bf16)],
        compiler_params=_cparams(("parallel", "arbitrary"), 48),
        name=name,
    )(x, mods, g, *([w] * len(w_specs)))


def _mm_res_kernel(*refs, n_in, gate_row):
    a_refs, w_refs = refs[:n_in], refs[n_in:2 * n_in]
    x_ref, m_ref, o_ref = refs[2 * n_in:]
    acc = _dot(a_refs[0][...], w_refs[0][...])
    for t in range(1, n_in):
        acc = acc + _dot(a_refs[t][...], w_refs[t][...])
    o_ref[...] = x_ref[...] + m_ref[gate_row:gate_row + 1, :] * acc


def _mm_res(acts, w, x, mods, *, layer, cond0, ncond, gate_row, tm, tn, vmem_mb, name):
    m_rows, n = x.shape
    n_in = len(acts)
    widths = [a.shape[1] for a in acts]
    offs = [sum(widths[:t]) for t in range(n_in)]
    in_specs = [pl.BlockSpec((tm, wd), lambda i, j: (i, 0)) for wd in widths]
    in_specs += [_lspec((wd, tn), layer, functools.partial(lambda i, j, rb: (rb, j), rb=_blk(off, wd)))
                 for wd, off in zip(widths, offs)]
    in_specs += [pl.BlockSpec((tm, tn), lambda i, j: (i, j)),
                 _mods_spec(tn, layer, cond0, m_rows // ncond, tm, True)]
    return pl.pallas_call(
        functools.partial(_mm_res_kernel, n_in=n_in, gate_row=gate_row),
        grid=(m_rows // tm, n // tn),
        in_specs=in_specs,
        out_specs=pl.BlockSpec((tm, tn), lambda i, j: (i, j)),
        out_shape=jax.ShapeDtypeStruct((m_rows, n), f32),
        compiler_params=_cparams(("parallel", "parallel"), vmem_mb),
        name=name,
    )(*acts, *([w] * n_in), x, mods)


def _rope(x, cos, s_up, s_dn):
    return x * cos + pltpu.roll(x, LANES - ROPE_F, 1) * s_up + pltpu.roll(x, ROPE_F, 1) * s_dn


def _mla_kv_rows(ckvn_bf, kr, wuk, wuv, kg, rope, k_scr, v_scr, r0):
    n = kr.shape[0]
    kn = _dot(ckvn_bf, wuk)
    v = _dot(ckvn_bf, wuv)
    krg = kr * kg[:, MLA_NOPE:]
    if rope is not None:
        krg = _rope(krg, *rope)
    kr_ss = jnp.sum(kr * kr, axis=-1, keepdims=True)
    for h in range(MLA_HEADS):
        knh = kn[:, h * MLA_NOPE:(h + 1) * MLA_NOPE]
        r = lax.rsqrt((jnp.sum(knh * knh, axis=-1, keepdims=True) + kr_ss) / MLA_QK + EPS)
        kh = jnp.concatenate([knh * r * kg[:, :MLA_NOPE], krg * r], axis=-1)
        k_scr[h, r0:r0 + n, :] = kh.astype(bf16)
        v_scr[h, r0:r0 + n, :] = v[:, h * MLA_V:(h + 1) * MLA_V].astype(bf16)


def _mla_attend(cq, wuq, qn_g, qg, rope, k_scr, v_scr, o_ref):
    cqn = (cq * _rms_scale(cq, MLA_Q_RANK) * qn_g).astype(bf16)
    q_all = _dot(cqn, wuq)
    scale = MLA_QK ** -0.5
    for h in range(MLA_HEADS):
        qh = q_all[:, h * MLA_QK_PAD:(h + 1) * MLA_QK_PAD]
        qh = qh * _rms_scale(qh, MLA_QK) * qg
        if rope is not None:
            qh = jnp.concatenate([qh[:, :MLA_NOPE], _rope(qh[:, MLA_NOPE:], *rope)], axis=-1)
        s = _dot_nt(qh.astype(bf16), k_scr[h]) * scale
        p = jnp.exp(s - jnp.max(s, axis=-1, keepdims=True))
        l = jnp.sum(p, axis=-1, keepdims=True)
        oh = _dot(p.astype(bf16), v_scr[h]) / l
        o_ref[:, h * MLA_V:(h + 1) * MLA_V] = oh.astype(o_ref.dtype)


def _krope_lanes(sm):
    lane = lax.broadcasted_iota(jnp.int32, sm.shape, 1)
    return jnp.where(lane < MLA_ROPE, sm, 0.0)


def _mla_ctx_kernel(cq_ref, ckv_ref, sm_ref, wuq_ref, wuk_ref, wuv_ref, qn_ref, kvn_ref, qg_ref, kg_ref,
                    *rest):
    o_ref, ckvn_ref, kr_ref, k_scr, v_scr = rest[-5:]
    ckv = ckv_ref[...]
    ckvn = ckv * _rms_scale(ckv, MLA_KV_RANK) * kvn_ref[...]
    ckvn_ref[...] = ckvn
    kr_ref[...] = sm_ref[:, 0:MLA_ROPE]
    _mla_kv_rows(ckvn.astype(bf16), _krope_lanes(sm_ref[...]), wuk_ref[...], wuv_ref[...], kg_ref[...],
                 None, k_scr, v_scr, 0)
    _mla_attend(cq_ref[...], wuq_ref[...], qn_ref[...], qg_ref[...], None, k_scr, v_scr, o_ref)


MLA_QB = 256


def _mla_lat_kernel(cq_ref, ckv_ref, sm_ref, cckv_ref, ckr_ref, rq_ref, rk_ref,
                    wuq_ref, wuk_ref, wuv_ref, qn_ref, kvn_ref, qg_ref, kg_ref, o_ref, k_scr, v_scr):
    @pl.when(pl.program_id(1) == 0)
    def _():
        kg = kg_ref[...]
        _mla_kv_rows(cckv_ref[0, 0].astype(bf16), ckr_ref[0, 0], wuk_ref[...], wuv_ref[...], kg,
                     None, k_scr, v_scr, 0)
        for c in range(DEC_SEQ // MLA_QB):
            rows = slice(c * MLA_QB, (c + 1) * MLA_QB)
            ckv = ckv_ref[rows, :]
            ckvn = ckv * _rms_scale(ckv, MLA_KV_RANK) * kvn_ref[...]
            rope = (rk_ref[0, rows, :], rk_ref[1, rows, :], rk_ref[2, rows, :])
            _mla_kv_rows(ckvn.astype(bf16), _krope_lanes(sm_ref[rows, :]), wuk_ref[...], wuv_ref[...], kg,
                         rope, k_scr, v_scr, PAST_LEN + c * MLA_QB)

    rope_q = (rq_ref[0], rq_ref[1], rq_ref[2])
    _mla_attend(cq_ref[...], wuq_ref[...], qn_ref[...], qg_ref[...], rope_q, k_scr, v_scr, o_ref)


def _mla_weight_specs(layer):
    zero = (lambda *a: (0, 0))
    return [_lspec((MLA_Q_RANK, MLA_HEADS * MLA_QK_PAD), layer, zero),
            _lspec((MLA_KV_RANK, MLA_HEADS * MLA_NOPE), layer, zero),
            _lspec((MLA_KV_RANK, MLA_HEADS * MLA_V), layer, zero),
            _lspec((1, MLA_Q_RANK), layer, zero),
            _lspec((1, MLA_KV_RANK), layer, zero),
            _lspec((1, MLA_QK_PAD), layer, zero),
            _lspec((1, MLA_QK_PAD), layer, zero)]


def _mla_weights(w):
    return [w["w_uq"], w["w_uk"], w["w_uv"], w["qn_g"], w["kvn_g"], w["q_g"], w["k_g"]]


def _stacked(prev, n_regular):
    prev = list(prev or [])
    specs = [pl.BlockSpec(memory_space=pl.ANY)] * len(prev)
    return prev, specs, {n_regular + t: 1 + t for t in range(len(prev))}


def _mla_ctx(u, w, layer, prev):
    t = SEQ
    in_specs = [pl.BlockSpec((t, MLA_Q_RANK), lambda i: (i, _blk(C_CQ, MLA_Q_RANK))),
                pl.BlockSpec((t, MLA_KV_RANK), lambda i: (i, _blk(C_CKV, MLA_KV_RANK))),
                pl.BlockSpec((t, LANES), lambda i: (i, _blk(C_SM, LANES)))] + _mla_weight_specs(layer)
    prev, alias_specs, aliases = _stacked(prev, len(in_specs))
    return pl.pallas_call(
        _mla_ctx_kernel,
        grid=(BATCH,),
        in_specs=in_specs + alias_specs,
        out_specs=[pl.BlockSpec((t, MLA_HEADS * MLA_V), lambda i: (i, 0)),
                   pl.BlockSpec((None, t, MLA_KV_RANK), lambda i: (i, layer, 0)),
                   pl.BlockSpec((None, t, MLA_ROPE), lambda i: (i, layer, 0))],
        out_shape=[jax.ShapeDtypeStruct((BATCH * t, MLA_HEADS * MLA_V), bf16),
                   jax.ShapeDtypeStruct((BATCH, DEPTH * t, MLA_KV_RANK), f32),
                   jax.ShapeDtypeStruct((BATCH, DEPTH * t, MLA_ROPE), f32)],
        input_output_aliases=aliases,
        scratch_shapes=[pltpu.VMEM((MLA_HEADS, t, MLA_QK_PAD), bf16),
                        pltpu.VMEM((MLA_HEADS, t, MLA_V), bf16)],
        compiler_params=_cparams(("parallel",), 40),
        name="mla_ctx",
    )(u, u, u, *_mla_weights(w), *prev)


def _mla_lat(u, cache_ckv, cache_kr, rope_tab, w, layer):
    t = DEC_SEQ
    nq = t // MLA_QB
    tk = PAST_LEN + t
    return pl.pallas_call(
        _mla_lat_kernel,
        grid=(DEC_BATCH, nq),
        in_specs=[pl.BlockSpec((MLA_QB, MLA_Q_RANK), lambda s, q: (s * nq + q, _blk(C_CQ, MLA_Q_RANK))),
                  pl.BlockSpec((t, MLA_KV_RANK), lambda s, q: (s, _blk(C_CKV, MLA_KV_RANK))),
                  pl.BlockSpec((t, LANES), lambda s, q: (s, _blk(C_SM, LANES))),
                  pl.BlockSpec((1, 1, PAST_LEN, MLA_KV_RANK), lambda s, q: (s, layer, 0, 0)),
                  pl.BlockSpec((1, 1, PAST_LEN, LANES), lambda s, q: (s, layer, 0, 0)),
                  pl.BlockSpec((3, MLA_QB, LANES), lambda s, q: (0, q, 0)),
                  pl.BlockSpec((3, t, LANES), lambda s, q: (0, 0, 0))] + _mla_weight_specs(layer),
        out_specs=pl.BlockSpec((MLA_QB, MLA_HEADS * MLA_V), lambda s, q: (s * nq + q, 0)),
        out_shape=jax.ShapeDtypeStruct((DEC_BATCH * t, MLA_HEADS * MLA_V), bf16),
        scratch_shapes=[pltpu.VMEM((MLA_HEADS, tk, MLA_QK_PAD), bf16),
                        pltpu.VMEM((MLA_HEADS, tk, MLA_V), bf16)],
        compiler_params=_cparams(("parallel", "arbitrary"), 48),
        name="mla_lat",
    )(u, u, u, cache_ckv, cache_kr, rope_tab, rope_tab, *_mla_weights(w))


CONV_PAD = 8
CONV_ROWS = 256
RECURRENCE_UNROLL = 4


def _conv_silu(x_ref, w_ref, b_ref, xp_ref, out_ref, t):
    c = x_ref.shape[1]
    xp_ref[0:CONV_PAD, :] = jnp.zeros((CONV_PAD, c), f32)
    xp_ref[CONV_PAD:CONV_PAD + t, :] = x_ref[...]
    xp_ref[CONV_PAD + t:2 * CONV_PAD + t, :] = jnp.zeros((CONV_PAD, c), f32)
    half = (CONV_W - 1) // 2
    for r0 in range(0, t, CONV_ROWS):
        acc = None
        for j in range(CONV_W):
            start = CONV_PAD - half + j + r0
            term = w_ref[j:j + 1, :] * xp_ref[start:start + CONV_ROWS, :]
            acc = term if acc is None else acc + term
        if b_ref is not None:
            acc = acc + b_ref[...]
        out_ref[r0:r0 + CONV_ROWS, :] = _silu(acc)


def _unit_tri_inverses(xs):
    n = xs[0].shape[0]
    shape = xs[0].shape
    ii = lax.broadcasted_iota(jnp.int32, shape, 0)
    jj = lax.broadcasted_iota(jnp.int32, shape, 1)
    first_half = jj < n
    eye = jnp.where((jj == ii) | (jj == ii + n), 1.0, 0.0)

    def hi_lo(a):
        hi = a.astype(bf16)
        return hi, (a - hi.astype(f32)).astype(bf16)

    def left(hi, lo):
        return jnp.where(first_half, hi, lo)

    levels = n.bit_length() - 2
    ps = [eye + x for x in xs]
    for j in range(levels + 1):
        first, last = j == 0, j == levels
        x_parts = [hi_lo(x) for x in xs]
        rhs = [jnp.concatenate([h, h, l, l], axis=0) for h, l in x_parts]
        lhs = []
        for p, xp in zip(ps, x_parts):
            blocks = []
            if not first:
                blocks.append(left(*hi_lo(p)))
            if not last:
                blocks.append(left(*xp))
            rows = blocks[0] if len(blocks) == 1 else jnp.concatenate(blocks, axis=0)
            lhs.append(jnp.concatenate([rows, rows], axis=1))
        rs = [_dot(l, r) for l, r in zip(lhs, rhs)]
        if not first:
            ps = [p + r[:n] for p, r in zip(ps, rs)]
        if not last:
            xs = [r[-n:] for r in rs]
    return ps


def _gdn_kernel(*refs, t, hb, cpi, has_s0, want_state, n_alias):
    it = iter(refs)
    q_ref, k_ref, v_ref, z_ref, sm_ref = (next(it) for _ in range(5))
    cwq_ref, cwk_ref, cwv_ref, gp_ref, ng_ref = (next(it) for _ in range(5))
    s0_ref = next(it) if has_s0 else None
    for _ in range(n_alias):
        next(it)
    o_ref = next(it)
    sfin_ref = next(it) if want_state else None
    xp_s, qn_s, kn_s, vn_s, beta_s, cum_s, cumt_s, wq_s, uu_s, kt_s, qk_s, eg_s, od_s, st_s = it

    ck = GDN_CHUNK
    nc = t // ck
    head0 = pl.program_id(1) * hb

    _conv_silu(q_ref, cwq_ref, None, xp_s, qn_s, t)
    _conv_silu(k_ref, cwk_ref, None, xp_s, kn_s, t)
    _conv_silu(v_ref, cwv_ref, None, xp_s, vn_s, t)
    for hh in range(hb):
        cols = slice(hh * GDN_DK, (hh + 1) * GDN_DK)
        qh = qn_s[:, cols]
        qn_s[:, cols] = qh * lax.rsqrt(jnp.sum(qh * qh, axis=-1, keepdims=True) + EPS) * (GDN_DK ** -0.5)
        kh = kn_s[:, cols]
        kn_s[:, cols] = kh * lax.rsqrt(jnp.sum(kh * kh, axis=-1, keepdims=True) + EPS)

    sm = sm_ref[...]
    beta_s[...] = _sigmoid(sm)
    g_all = -jnp.exp(gp_ref[0:1, :]) * _softplus(sm + gp_ref[1:2, :])

    ii = lax.broadcasted_iota(jnp.int32, (ck, LANES), 0)
    jj = lax.broadcasted_iota(jnp.int32, (ck, LANES), 1)
    jj = jnp.where(jj < ck, jj, jj - ck)
    incl = (ii >= jj, ii <= jj)
    strict = (ii > jj, ii < jj)
    tri = (incl[0][:, :ck].astype(bf16), incl[1][:, :ck].astype(bf16))

    for c in range(nc):
        rows = slice(c * ck, (c + 1) * ck)
        for d in range(2):
            cum = _tri_cumsum(tri[d], g_all[rows, :])
            cum_s[d, rows, :] = cum
            cumt_s[d, c] = jnp.concatenate([cum, cum], axis=0).T

    if has_s0:
        for d in range(2):
            for hh in range(hb):
                st_s[d * hb + hh] = s0_ref[0, 0, d, hh]
    else:
        st_s[...] = jnp.zeros(st_s.shape, f32)

    def phase1(it_idx, carry):
        chunks = [it_idx * cpi + e for e in range(cpi)]
        rows = [pl.ds(pl.multiple_of(c * ck, ck), ck) for c in chunks]
        beta_c = [beta_s[r, :] for r in rows]
        heads = [(e, hh) for e in range(cpi) for hh in range(hb)]
        kc, qc, vc, g_kk, g_qk = {}, {}, {}, {}, {}
        for e, hh in heads:
            cols = slice(hh * GDN_DK, (hh + 1) * GDN_DK)
            kc[e, hh] = kn_s[rows[e], cols]
            qc[e, hh] = qn_s[rows[e], cols]
            vc[e, hh] = vn_s[rows[e], cols]
        for key in heads:
            kcb = kc[key].astype(bf16)
            kc_dup = jnp.concatenate([kcb, kcb], axis=0)
            g_kk[key] = _dot_nt(kcb, kc_dup)
            g_qk[key] = _dot_nt(qc[key].astype(bf16), kc_dup)
        probs = [(e, hh, d) for e, hh in heads for d in range(2)]
        col, bcol, dec, gl, neg_a = {}, {}, {}, {}, []
        for e, hh, d in probs:
            key = (e, hh, d)
            lane_g = SM_A + d * GDN_HEADS + head0 + hh
            lane_b = SM_BETA + d * GDN_HEADS + head0 + hh
            cum_c = cum_s[d, rows[e], :]
            col[key] = _lane_col(cum_c, lane_g)
            bcol[key] = _lane_col(beta_c[e], lane_b)
            row = cumt_s[d, chunks[e], pl.ds(lane_g, 1), :]
            dec[key] = jnp.where(incl[d], jnp.exp(col[key] - row), 0.0)
            end = cum_c[ck - 1:ck, :] if d == 0 else cum_c[0:1, :]
            gl[key] = _lane_col(end, lane_g)
            neg_a.append(jnp.where(strict[d], -(bcol[key] * g_kk[e, hh] * dec[key]), 0.0))
        tms = _unit_tri_inverses(neg_a)
        ecol, rhs = {}, []
        for key in probs:
            e, hh, d = key
            ecol[key] = jnp.exp(col[key])
            rhs.append(jnp.concatenate([kc[e, hh] * (bcol[key] * ecol[key]), vc[e, hh] * bcol[key]],
                                       axis=-1).astype(bf16))
        wus = [_dot(tm[:, :ck].astype(bf16), r) for tm, r in zip(tms, rhs)]
        for key, wu in zip(probs, wus):
            e, hh, d = key
            idx = d * hb + hh
            r2 = pl.multiple_of(chunks[e] * 2 * ck, 2 * ck)
            wq_s[idx, pl.ds(r2, ck), :] = wu[:, :GDN_DK].astype(bf16)
            wq_s[idx, pl.ds(r2 + ck, ck), :] = (qc[e, hh] * ecol[key]).astype(bf16)
            uu_s[idx, rows[e], :] = wu[:, GDN_DK:]
            kt_s[idx, rows[e], :] = (kc[e, hh] * jnp.exp(gl[key] - col[key])).astype(bf16)
            qk_s[idx, rows[e], :] = jnp.where(incl[d], g_qk[e, hh] * dec[key], 0.0)[:, :ck].astype(bf16)
            eg_s[idx, pl.ds(chunks[e], 1), :] = jnp.broadcast_to(jnp.exp(gl[key]), (1, LANES))
        return carry

    lax.fori_loop(0, nc // cpi, phase1, 0, unroll=nc // cpi <= 2)

    def phase2(i, carry):
        probs = [(d, hh) for d in range(2) for hh in range(hb)]
        chunk = {0: i, 1: nc - 1 - i}
        rows = {d: pl.ds(pl.multiple_of(chunk[d] * ck, ck), ck) for d in range(2)}
        rows2 = {d: pl.ds(pl.multiple_of(chunk[d] * 2 * ck, 2 * ck), 2 * ck) for d in range(2)}
        s = [st_s[d * hb + hh] for d, hh in probs]
        ws = [_dot(wq_s[d * hb + hh, rows2[d], :], sv.astype(bf16)) for (d, hh), sv in zip(probs, s)]
        vb = [(uu_s[d * hb + hh, rows[d], :] - w[:ck]).astype(bf16) for (d, hh), w in zip(probs, ws)]
        o = [w[ck:] + _dot(qk_s[d * hb + hh, rows[d], :], v) for (d, hh), w, v in zip(probs, ws, vb)]
        ds = [_dot_tn(kt_s[d * hb + hh, rows[d], :], v) for (d, hh), v in zip(probs, vb)]
        for (d, hh), sv, dv, ov in zip(probs, s, ds, o):
            idx = d * hb + hh
            st_s[idx] = sv * eg_s[idx, pl.ds(chunk[d], 1), :] + dv
            od_s[d, rows[d], hh * GDN_DV:(hh + 1) * GDN_DV] = ov
        return carry

    lax.fori_loop(0, nc, phase2, 0, unroll=min(nc, RECURRENCE_UNROLL))

    for hh in range(hb):
        cols = slice(hh * GDN_DV, (hh + 1) * GDN_DV)
        o = od_s[0, :, cols] + od_s[1, :, cols]
        on = o * lax.rsqrt(jnp.mean(o * o, axis=-1, keepdims=True) + EPS) * ng_ref[...]
        o_ref[:, cols] = (on * _silu(z_ref[:, cols])).astype(o_ref.dtype)
    if want_state:
        for d in range(2):
            for hh in range(hb):
                sfin_ref[0, d, hh] = st_s[d * hb + hh]


def _gdn(u, wl, *, nseq, t, hb, cpi, s0, layer, want_state, prev=None):
    w = hb * GDN_DK
    nhb = GDN_HEADS // hb
    nc = t // GDN_CHUNK
    assert nc % cpi == 0
    gq, gk, gv = C_GQKV, C_GQKV + GDN_HEADS * GDN_DK, C_GQKV + 2 * GDN_HEADS * GDN_DK
    in_specs = [pl.BlockSpec((t, w), lambda i, j: (i, _blk(gq, w) + j)),
                pl.BlockSpec((t, w), lambda i, j: (i, _blk(gk, w) + j)),
                pl.BlockSpec((t, w), lambda i, j: (i, _blk(gv, w) + j)),
                pl.BlockSpec((t, w), lambda i, j: (i, _blk(C_GZ, w) + j)),
                pl.BlockSpec((t, LANES), lambda i, j: (i, _blk(C_SM, LANES))),
                _lspec((8, w), layer, lambda i, j: (0, j)),
                _lspec((8, w), layer, lambda i, j: (0, nhb + j)),
                _lspec((8, w), layer, lambda i, j: (0, 2 * nhb + j)),
                _lspec((8, LANES), layer, lambda i, j: (0, 0)),
                _lspec((1, GDN_DV), layer, lambda i, j: (0, 0))]
    args = [u, u, u, u, u, wl["gdn_cw"], wl["gdn_cw"], wl["gdn_cw"], wl["gdn_gp"], wl["gdn_ng"]]
    if s0 is not None:
        in_specs.append(pl.BlockSpec((1, 1, 2, hb, GDN_DK, GDN_DV), lambda i, j: (i, layer, 0, j, 0, 0)))
        args.append(s0)
    prev, alias_specs, aliases = _stacked(prev, len(in_specs))
    out_specs = [pl.BlockSpec((t, w), lambda i, j: (i, j))]
    out_shape = [jax.ShapeDtypeStruct((nseq * t, GDN_HEADS * GDN_DV), bf16)]
    if want_state:
        out_specs.append(pl.BlockSpec((1, None, 2, hb, GDN_DK, GDN_DV), lambda i, j: (i, layer, 0, j, 0, 0)))
        out_shape.append(jax.ShapeDtypeStruct((nseq, DEPTH, 2, GDN_HEADS, GDN_DK, GDN_DV), f32))
    scratch = [pltpu.VMEM((t + 2 * CONV_PAD, w), f32),
               pltpu.VMEM((t, w), f32), pltpu.VMEM((t, w), f32), pltpu.VMEM((t, w), f32),
               pltpu.VMEM((t, LANES), f32),
               pltpu.VMEM((2, t, LANES), f32),
               pltpu.VMEM((2, nc, LANES, LANES), f32),
               pltpu.VMEM((2 * hb, 2 * t, GDN_DK), bf16),
               pltpu.VMEM((2 * hb, t, GDN_DV), f32),
               pltpu.VMEM((2 * hb, t, GDN_DK), bf16),
               pltpu.VMEM((2 * hb, t, GDN_CHUNK), bf16),
               pltpu.VMEM((2 * hb, max(nc, 8), LANES), f32),
               pltpu.VMEM((2, t, w), f32),
               pltpu.VMEM((2 * hb, GDN_DK, GDN_DV), f32)]
    kern = functools.partial(_gdn_kernel, t=t, hb=hb, cpi=cpi, has_s0=s0 is not None, want_state=want_state,
                             n_alias=len(prev))
    return pl.pallas_call(
        kern,
        grid=(nseq, nhb),
        in_specs=in_specs + alias_specs,
        out_specs=out_specs,
        out_shape=out_shape,
        input_output_aliases=aliases,
        scratch_shapes=scratch,
        compiler_params=_cparams(("parallel", "parallel"), 48),
        name="gdn_" + ("lat" if s0 is not None else "ctx"),
    )(*args, *prev)


def _ssd_kernel(*refs, t, cpi, has_s0, want_state, n_alias):
    it = iter(refs)
    x_ref, bc_ref, z_ref, sm_ref = (next(it) for _ in range(4))
    cwx_ref, cwbc_ref, cbx_ref, cbbc_ref = (next(it) for _ in range(4))
    gp_ref, ex_ref, dv_ref, ng_ref = (next(it) for _ in range(4))
    s0_ref = next(it) if has_s0 else None
    for _ in range(n_alias):
        next(it)
    o_ref = next(it)
    sfin_ref = next(it) if want_state else None
    xp_s, xs_s, bcs_s, dt_s, da_s, y_s, ea_s, stc_s, cd_s, st_s = it

    ck = SSD_CHUNK
    nc = t // ck
    wg = HEAD_PER_GROUP * SSD_P
    nb = SSD_GROUPS * SSD_N

    _conv_silu(x_ref, cwx_ref, cbx_ref, xp_s, xs_s, t)
    _conv_silu(bc_ref, cwbc_ref, cbbc_ref, xp_s, bcs_s, t)

    dt_all = _softplus(sm_ref[...] + gp_ref[1:2, :])
    dt_s[...] = dt_all
    da_s[...] = dt_all * (-jnp.exp(gp_ref[0:1, :]))

    if has_s0:
        for d in range(2):
            s0 = jnp.concatenate([s0_ref[0, 0, d, h] for h in range(SSD_HEADS)], axis=0)
            st_s[d] = s0.T
    else:
        st_s[...] = jnp.zeros(st_s.shape, f32)

    ii = lax.broadcasted_iota(jnp.int32, (ck, ck), 0)
    jj = lax.broadcasted_iota(jnp.int32, (ck, ck), 1)
    incl = (ii >= jj, ii <= jj)
    tri = (incl[0].astype(bf16), incl[1].astype(bf16))
    lane = lax.broadcasted_iota(jnp.int32, (ck, LANES), 1)
    end_rows = 16
    dirs = (0, 1)

    def phase1(it_idx, carry):
        chunks = [it_idx * cpi + e for e in range(cpi)]
        rows = [pl.ds(pl.multiple_of(c * ck, ck), ck) for c in chunks]
        probs = [(e, d) for e in range(cpi) for d in dirs]
        dt_c = [dt_s[r, :] for r in rows]
        da_c = [da_s[r, :] for r in rows]
        cum = {(e, d): _tri_cumsum(tri[d], da_c[e]) for e, d in probs}
        cum_t = {key: cum[key].T for key in probs}
        dt_t = [v.T for v in dt_c]
        bcc = [bcs_s[r, :].astype(bf16) for r in rows]
        xc = [xs_s[r, :] for r in rows]
        xb = [v.astype(bf16) for v in xc]
        cb = {(e, g): _dot_nt(bcc[e][:, nb + g * SSD_N:nb + (g + 1) * SSD_N], bcc[e][:, g * SSD_N:(g + 1) * SSD_N])
              for e in range(cpi) for g in range(SSD_GROUPS)}
        heads = [(e, d, h) for e, d in probs for h in range(SSD_HEADS)]
        ms = []
        for e, d, h in heads:
            ln = SM_DT + d * SSD_HEADS + h
            col = _lane_col(cum[e, d], ln)
            row = cum_t[e, d][ln:ln + 1, :]
            dtrow = dt_t[e][ln:ln + 1, :]
            lm = jnp.where(incl[d], jnp.exp(col - row), 0.0)
            ms.append((cb[e, h // HEAD_PER_GROUP] * lm * dtrow).astype(bf16))
        ys = {key: _dot(m, xb[key[0]][:, (key[2] // 2) * LANES:(key[2] // 2 + 1) * LANES])
              for key, m in zip(heads, ms)}
        end = {(e, d): cum[e, d][ck - 1:ck, :] if d == 0 else cum[e, d][0:1, :] for e, d in probs}
        spread = {key: _select_dot(jnp.concatenate([jnp.exp(end[key] - cum[key]) * dt_c[key[0]], jnp.exp(cum[key]),
                                                    jnp.broadcast_to(jnp.exp(end[key]), (end_rows, LANES))],
                                                   axis=0), ex_ref[key[1]]) for key in probs}
        xsc = {key: (xc[key[0]] * spread[key][:ck]).astype(bf16) for key in probs}
        st_c = {(e, d, g): _dot_tn(bcc[e][:, g * SSD_N:(g + 1) * SSD_N], xsc[e, d][:, g * wg:(g + 1) * wg])
                for e, d in probs for g in range(SSD_GROUPS)}
        for e, d in probs:
            parts = [jnp.where(lane < SSD_P, ys[e, d, 2 * pr], ys[e, d, 2 * pr + 1]) for pr in range(SSD_HEADS // 2)]
            y_s[d, rows[e], :] = jnp.concatenate(parts, axis=-1)
            ea_s[d, rows[e], :] = spread[e, d][ck:2 * ck]
            cd_s[d, chunks[e]] = spread[e, d][2 * ck:2 * ck + 8]
            stc_s[d, chunks[e]] = jnp.concatenate([st_c[e, d, g] for g in range(SSD_GROUPS)], axis=-1)
        return carry

    lax.fori_loop(0, nc // cpi, phase1, 0)

    def phase2(i, carry):
        chunk = (i, nc - 1 - i)
        rows = [pl.ds(pl.multiple_of(c * ck, ck), ck) for c in chunk]
        st = [st_s[d] for d in dirs]
        stb = [v.astype(bf16) for v in st]
        y_off = {(d, g): _dot(bcs_s[rows[d], nb + g * SSD_N:nb + (g + 1) * SSD_N].astype(bf16),
                              stb[d][:, g * wg:(g + 1) * wg]) for d in dirs for g in range(SSD_GROUPS)}
        for d in dirs:
            off = jnp.concatenate([y_off[d, g] for g in range(SSD_GROUPS)], axis=-1)
            y_s[d, rows[d], :] = y_s[d, rows[d], :] + off * ea_s[d, rows[d], :]
            st_s[d] = st[d] * cd_s[d, chunk[d], 0:1, :] + stc_s[d, chunk[d]]
        return carry

    lax.fori_loop(0, nc, phase2, 0, unroll=min(nc, RECURRENCE_UNROLL))

    y = y_s[0] + y_s[1] + xs_s[...] * dv_ref[...]
    y = y * _silu(z_ref[...])
    for g in range(SSD_GROUPS):
        cols = slice(g * wg, (g + 1) * wg)
        yg = y[:, cols]
        o_ref[:, cols] = (yg * lax.rsqrt(jnp.mean(yg * yg, axis=-1, keepdims=True) + EPS)
                          * ng_ref[:, cols]).astype(o_ref.dtype)
    if want_state:
        for d in range(2):
            stt = st_s[d].T
            for h in range(SSD_HEADS):
                sfin_ref[0, d, h] = stt[h * SSD_P:(h + 1) * SSD_P, :]


def _ssd(u, wl, *, nseq, t, cpi, s0, layer, want_state, prev=None):
    wi, wbc = SSD_INNER, 2 * SSD_GROUPS * SSD_N
    nc = t // SSD_CHUNK
    assert nc % cpi == 0 and wi == wbc
    in_specs = [pl.BlockSpec((t, wi), lambda i: (i, _blk(C_XBC, wi))),
                pl.BlockSpec((t, wbc), lambda i: (i, _blk(C_XBC + wi, wbc))),
                pl.BlockSpec((t, wi), lambda i: (i, _blk(C_SZ, wi))),
                pl.BlockSpec((t, LANES), lambda i: (i, _blk(C_SM, LANES))),
                _lspec((8, wi), layer, lambda i: (0, 0)),
                _lspec((8, wbc), layer, lambda i: (0, 1)),
                _lspec((1, wi), layer, lambda i: (0, 0)),
                _lspec((1, wbc), layer, lambda i: (0, 1)),
                _lspec((8, LANES), layer, lambda i: (0, 0)),
                pl.BlockSpec((2, LANES, wi), lambda i: (0, 0, 0)),
                _lspec((1, wi), layer, lambda i: (0, 0)),
                _lspec((1, wi), layer, lambda i: (0, 0))]
    args = [u, u, u, u, wl["ssd_cw"], wl["ssd_cw"], wl["ssd_cb"], wl["ssd_cb"],
            wl["ssd_gp"], wl["ssd_ex"], wl["ssd_dv"], wl["ssd_ng"]]
    if s0 is not None:
        in_specs.append(pl.BlockSpec((1, 1, 2, SSD_HEADS, SSD_P, SSD_N), lambda i: (i, layer, 0, 0, 0, 0)))
        args.append(s0)
    prev, alias_specs, aliases = _stacked(prev, len(in_specs))
    out_specs = [pl.BlockSpec((t, wi), lambda i: (i, 0))]
    out_shape = [jax.ShapeDtypeStruct((nseq * t, wi), bf16)]
    if want_state:
        out_specs.append(pl.BlockSpec((1, None, 2, SSD_HEADS, SSD_P, SSD_N), lambda i: (i, layer, 0, 0, 0, 0)))
        out_shape.append(jax.ShapeDtypeStruct((nseq, DEPTH, 2, SSD_HEADS, SSD_P, SSD_N), f32))
    scratch = [pltpu.VMEM((t + 2 * CONV_PAD, wi), f32),
               pltpu.VMEM((t, wi), f32), pltpu.VMEM((t, wbc), f32),
               pltpu.VMEM((t, LANES), f32), pltpu.VMEM((t, LANES), f32),
               pltpu.VMEM((2, t, wi), f32),
               pltpu.VMEM((2, t, wi), f32),
               pltpu.VMEM((2, nc, SSD_N, wi), f32),
               pltpu.VMEM((2, nc, 8, wi), f32),
               pltpu.VMEM((2, SSD_N, wi), f32)]
    kern = functools.partial(_ssd_kernel, t=t, cpi=cpi, has_s0=s0 is not None, want_state=want_state,
                             n_alias=len(prev))
    return pl.pallas_call(
        kern,
        grid=(nseq,),
        in_specs=in_specs + alias_specs,
        out_specs=out_specs,
        out_shape=out_shape,
        input_output_aliases=aliases,
        scratch_shapes=scratch,
        compiler_params=_cparams(("parallel",), 48),
        name="ssd_" + ("lat" if s0 is not None else "ctx"),
    )(*args, *prev)


def _prep_weights(p):
    w_in = p["w_in"]
    o_g = MLA_Q_RANK + MLA_KV_RANK + MLA_ROPE
    o_s = o_g + 2 * GDN_HEADS * GDN_DK + 2 * GDN_HEADS * GDN_DV + 4 * GDN_HEADS
    n_qkv = 2 * GDN_HEADS * GDN_DK + GDN_HEADS * GDN_DV
    n_gz = GDN_HEADS * GDN_DV
    sl = lambda a, b: w_in[:, :, a:b]
    zeros = lambda n: jnp.zeros(w_in.shape[:2] + (n,), w_in.dtype)
    n_xbc = SSD_INNER + 2 * SSD_GROUPS * SSD_N
    parts = [sl(0, MLA_Q_RANK),
             sl(MLA_Q_RANK + MLA_KV_RANK, o_g),
             sl(o_g + n_qkv + n_gz, o_g + n_qkv + n_gz + 4 * GDN_HEADS),
             sl(o_s + SSD_INNER + n_xbc, o_s + SSD_INNER + n_xbc + 2 * SSD_HEADS),
             zeros(C_XBC - (C_SM + MLA_ROPE + 4 * GDN_HEADS + 2 * SSD_HEADS)),
             sl(o_s + SSD_INNER, o_s + SSD_INNER + n_xbc),
             sl(MLA_Q_RANK, MLA_Q_RANK + MLA_KV_RANK),
             sl(o_g + n_qkv, o_g + n_qkv + n_gz),
             sl(o_g, o_g + n_qkv),
             sl(o_s, o_s + SSD_INNER)]
    w_in_p = jnp.concatenate(parts, axis=-1).astype(bf16)
    assert w_in_p.shape[-1] == N_IN

    w_uq = p["mla_w_uq"].reshape(DEPTH, MLA_Q_RANK, MLA_HEADS, MLA_QK)
    w_uq = jnp.pad(w_uq, ((0, 0), (0, 0), (0, 0), (0, MLA_QK_PAD - MLA_QK)))
    w_uq = w_uq.reshape(DEPTH, MLA_Q_RANK, MLA_HEADS * MLA_QK_PAD).astype(bf16)
    w_ukv = p["mla_w_ukv"].reshape(DEPTH, MLA_KV_RANK, MLA_HEADS, MLA_NOPE + MLA_V)
    w_uk = w_ukv[..., :MLA_NOPE].reshape(DEPTH, MLA_KV_RANK, MLA_HEADS * MLA_NOPE).astype(bf16)
    w_uv = w_ukv[..., MLA_NOPE:].reshape(DEPTH, MLA_KV_RANK, MLA_HEADS * MLA_V).astype(bf16)
    pad_g = lambda g: jnp.pad(g, ((0, 0), (0, MLA_QK_PAD - MLA_QK)))[:, None, :]

    def lane_rows(a_log, dt_bias, lane0):
        n = a_log.shape[1] * a_log.shape[2]
        rows = jnp.stack([a_log.reshape(DEPTH, n), dt_bias.reshape(DEPTH, n)], axis=1)
        return jnp.pad(rows.astype(f32), ((0, 0), (0, 6), (lane0, LANES - lane0 - n)))

    ex = np.zeros((2, LANES, SSD_INNER), np.float32)
    for d in range(2):
        for h in range(SSD_HEADS):
            ex[d, SM_DT + d * SSD_HEADS + h, h * SSD_P:(h + 1) * SSD_P] = 1.0

    pad_rows = lambda w: jnp.pad(w.astype(f32), ((0, 0), (0, 8 - CONV_W), (0, 0)))
    return dict(
        w_in=w_in_p, w_uq=w_uq, w_uk=w_uk, w_uv=w_uv,
        qn_g=p["mla_qnorm_g"][:, None, :], kvn_g=p["mla_kvnorm_g"][:, None, :],
        q_g=pad_g(p["mla_q_g"]), k_g=pad_g(p["mla_k_g"]),
        w_out=p["w_out"].astype(bf16), w_gu=p["ffn_w_gu"].astype(bf16), w_down=p["ffn_w_down"].astype(bf16),
        norm1_g=p["norm1_g"][:, None, :], norm2_g=p["norm2_g"][:, None, :],
        gdn_cw=pad_rows(p["gdn_conv_w"]), gdn_gp=lane_rows(p["gdn_a_log"], p["gdn_dt_bias"], SM_A),
        gdn_ng=p["gdn_norm_g"][:, None, :],
        ssd_cw=pad_rows(p["ssd_conv_w"]), ssd_cb=p["ssd_conv_b"][:, None, :],
        ssd_gp=lane_rows(p["ssd_a_log"], p["ssd_dt_bias"], SM_DT), ssd_ex=jnp.asarray(ex, dtype=bf16),
        ssd_dv=jnp.repeat(p["ssd_d"], SSD_P, axis=1)[:, None, :], ssd_ng=p["ssd_norm_g"][:, None, :],
    )


def _rope_tables(n_tokens):
    rows = n_tokens // GRID_W
    row = jnp.repeat(jnp.arange(rows, dtype=f32), GRID_W)
    col = jnp.tile(jnp.arange(GRID_W, dtype=f32), rows)
    inv = ROPE_BASE ** (-jnp.arange(ROPE_F, dtype=f32) / ROPE_F)
    ar, ac = row[:, None] * inv, col[:, None] * inv
    zero = jnp.zeros_like(ar)
    tail = jnp.zeros((n_tokens, LANES - MLA_ROPE), f32)
    cos = jnp.concatenate([jnp.cos(ar), jnp.cos(ar), jnp.cos(ac), jnp.cos(ac), tail], axis=-1)
    s_up = jnp.concatenate([-jnp.sin(ar), zero, -jnp.sin(ac), zero, tail], axis=-1)
    s_dn = jnp.concatenate([zero, jnp.sin(ar), zero, jnp.sin(ac), tail], axis=-1)
    return jnp.stack([cos, s_up, s_dn], axis=0)


FFN_TF = 512
DENSE_TM = 1024
IN_TN = 1024
OUT_TM = 512
DOWN_TN = 512


def _trunk_layer(x, mods, wl, l, *, nseq, t, latent, cache=None, stacked=None):
    cond = dict(layer=l, cond0=1 if latent else 0, ncond=nseq if latent else 1)
    u = _norm_mm(x, mods, wl["norm1_g"], wl["w_in"], shift_row=0, scale_row=1, swiglu=False,
                 tm=DENSE_TM, tn=IN_TN, out_dtype=f32, name="in_proj", **cond)
    if latent:
        o_mla = _mla_lat(u, cache["ckv"], cache["krope"], cache["rope"], wl, l)
        o_gdn, = _gdn(u, wl, nseq=nseq, t=t, hb=2, cpi=4, s0=cache["gdn"], layer=l, want_state=False)
        o_ssd, = _ssd(u, wl, nseq=nseq, t=t, cpi=2, s0=cache["ssd"], layer=l, want_state=False)
    else:
        prev = (lambda a, b: None) if stacked is None else (lambda a, b: stacked[a:b])
        o_mla, ckv_all, kr_all = _mla_ctx(u, wl, l, prev(0, 2))
        o_gdn, sg_all = _gdn(u, wl, nseq=nseq, t=t, hb=GDN_HEADS, cpi=2, s0=None, layer=l, want_state=True,
                             prev=prev(2, 3))
        o_ssd, ss_all = _ssd(u, wl, nseq=nseq, t=t, cpi=2, s0=None, layer=l, want_state=True, prev=prev(3, 4))
        stacked = (ckv_all, kr_all, sg_all, ss_all)
    x = _mm_res([o_mla, o_gdn, o_ssd], wl["w_out"], x, mods, gate_row=2, tm=OUT_TM, tn=D_MODEL,
                vmem_mb=48, name="out_proj", **cond)
    act = _norm_mm(x, mods, wl["norm2_g"], wl["w_gu"], shift_row=3, scale_row=4, swiglu=True,
                   tm=DENSE_TM, tn=FFN_TF, out_dtype=bf16, name="ffn_gu", **cond)
    x = _mm_res([act], wl["w_down"], x, mods, gate_row=5, tm=DENSE_TM, tn=DOWN_TN, vmem_mb=56,
                name="ffn_down", **cond)
    return x, stacked


def kernel(x_prompt, x_sample, cache_mla_ckv, cache_mla_krope, state_gdn, state_ssd, c, c_ctx, norm1_g, norm2_g, ada_w, ada_b, w_in, w_out, mla_qnorm_g, mla_w_uq, mla_kvnorm_g, mla_w_ukv, mla_q_g, mla_k_g, gdn_conv_w, gdn_a_log, gdn_dt_bias, gdn_norm_g, ssd_conv_w, ssd_conv_b, ssd_a_log, ssd_dt_bias, ssd_d, ssd_norm_g, ffn_w_gu, ffn_w_down):
    p = dict(norm1_g=norm1_g, norm2_g=norm2_g, w_in=w_in, w_out=w_out,
             mla_qnorm_g=mla_qnorm_g, mla_w_uq=mla_w_uq, mla_kvnorm_g=mla_kvnorm_g,
             mla_w_ukv=mla_w_ukv, mla_q_g=mla_q_g, mla_k_g=mla_k_g, gdn_conv_w=gdn_conv_w,
             gdn_a_log=gdn_a_log, gdn_dt_bias=gdn_dt_bias, gdn_norm_g=gdn_norm_g,
             ssd_conv_w=ssd_conv_w, ssd_conv_b=ssd_conv_b, ssd_a_log=ssd_a_log,
             ssd_dt_bias=ssd_dt_bias, ssd_d=ssd_d, ssd_norm_g=ssd_norm_g,
             ffn_w_gu=ffn_w_gu, ffn_w_down=ffn_w_down)
    w = _prep_weights(p)

    cvec = jnp.concatenate([c_ctx[None, :], c, jnp.zeros((8 - 1 - DEC_BATCH, D_MODEL), f32)], axis=0)
    mods = _mods(cvec, ada_w, ada_b).reshape(DEPTH, 8, 6, D_MODEL)

    cache = dict(ckv=cache_mla_ckv,
                 krope=jnp.pad(cache_mla_krope, ((0, 0), (0, 0), (0, 0), (0, LANES - MLA_ROPE))),
                 rope=_rope_tables(DEC_SEQ), gdn=state_gdn, ssd=state_ssd)

    xp = x_prompt.reshape(BATCH * SEQ, D_MODEL)
    xs = x_sample.reshape(DEC_BATCH * DEC_SEQ, D_MODEL)
    stacked = None
    for l in range(DEPTH):
        xp, stacked = _trunk_layer(xp, mods, w, l, nseq=BATCH, t=SEQ, latent=False, stacked=stacked)
        xs, _ = _trunk_layer(xs, mods, w, l, nseq=DEC_BATCH, t=DEC_SEQ, latent=True, cache=cache)
    ckv_all, kr_all, sg_all, ss_all = stacked
    return (xp.reshape(BATCH, SEQ, D_MODEL), xs.reshape(DEC_BATCH, DEC_SEQ, D_MODEL),
            ckv_all.reshape(BATCH, DEPTH, SEQ, MLA_KV_RANK), kr_all.reshape(BATCH, DEPTH, SEQ, MLA_ROPE),
            sg_all, ss_all)
```

```python
import functools

import numpy as np
import jax
import jax.numpy as jnp
from jax import lax
from jax.experimental import pallas as pl
from jax.experimental.pallas import tpu as pltpu

f32 = jnp.float32
bf16 = jnp.bfloat16

D_MODEL = 2048
BATCH = 32
SEQ = 256
DEPTH = 4
DEC_BATCH = 2
DEC_SEQ = 1024
PAST_LEN = 256
GRID_W = 64
MLA_HEADS = 8
MLA_NOPE = 128
MLA_ROPE = 64
MLA_QK = MLA_NOPE + MLA_ROPE
MLA_V = 128
MLA_Q_RANK = 768
MLA_KV_RANK = 512
ROPE_F = MLA_ROPE // 4
ROPE_BASE = 10000.0
GDN_HEADS = 4
GDN_DK = 128
GDN_DV = 128
GDN_CHUNK = 64
SSD_HEADS = 8
SSD_P = 64
SSD_GROUPS = 2
SSD_N = 128
SSD_CHUNK = 128
SSD_INNER = SSD_HEADS * SSD_P
CONV_W = 5
EPS = 1e-6

LANES = 128
MLA_QK_PAD = 256
HEAD_PER_GROUP = SSD_HEADS // SSD_GROUPS

C_CQ = 0
C_SM = 768
C_XBC = 1024
C_CKV = 2048
C_GZ = 2560
C_GQKV = 3072
C_SZ = 4608
N_IN = 5120
SM_BETA = 64
SM_A = 72
SM_DT = 80

VMEM_MB = 1024 * 1024


def _cparams(sem, vmem_mb):
    return pltpu.CompilerParams(dimension_semantics=sem, vmem_limit_bytes=vmem_mb * VMEM_MB)


def _blk(off, width):
    assert off % width == 0
    return off // width


def _sigmoid(x):
    return 1.0 / (1.0 + jnp.exp(-x))


def _silu(x):
    return x * _sigmoid(x)


def _softplus(x):
    return jnp.maximum(x, 0.0) + jnp.log(1.0 + jnp.exp(-jnp.abs(x)))


def _rms_scale(x, n):
    return lax.rsqrt(jnp.sum(x * x, axis=-1, keepdims=True) / n + EPS)


def _lane_col(a, lane_idx):
    lane = lax.broadcasted_iota(jnp.int32, a.shape, 1)
    return jnp.sum(jnp.where(lane == lane_idx, a, 0.0), axis=1, keepdims=True)


def _dot(a, b):
    return jnp.dot(a, b, preferred_element_type=f32)


def _dot_nt(a, b):
    return lax.dot_general(a, b, (((1,), (1,)), ((), ())), preferred_element_type=f32)


def _dot_tn(a, b):
    return lax.dot_general(a, b, (((0,), (0,)), ((), ())), preferred_element_type=f32)


def _tri_cumsum(tri_bf, a):
    hi = a.astype(bf16)
    r1 = a - hi.astype(f32)
    mid = r1.astype(bf16)
    lo = (r1 - mid.astype(f32)).astype(bf16)
    r = _dot(tri_bf, jnp.concatenate([hi, mid, lo], axis=1))
    return (r[:, 2 * LANES:] + r[:, LANES:2 * LANES]) + r[:, :LANES]


def _select_dot(a, sel_bf):
    n = a.shape[0]
    hi = a.astype(bf16)
    lo = (a - hi.astype(f32)).astype(bf16)
    r = _dot(jnp.concatenate([hi, lo], axis=0), sel_bf)
    return r[:n] + r[n:]


def _mods_kernel(c_ref, w_ref, b_ref, o_ref):
    s = _silu(c_ref[...]).astype(bf16)
    o_ref[0] = _dot(s, w_ref[0].astype(bf16)) + b_ref[0]


def _mods(cvec8, ada_w, ada_b):
    tn = 1024
    n = 6 * D_MODEL
    return pl.pallas_call(
        _mods_kernel,
        grid=(DEPTH, n // tn),
        in_specs=[pl.BlockSpec((8, D_MODEL), lambda l, j: (0, 0)),
                  pl.BlockSpec((1, D_MODEL, tn), lambda l, j: (l, 0, j)),
                  pl.BlockSpec((1, 1, tn), lambda l, j: (l, 0, j))],
        out_specs=pl.BlockSpec((1, 8, tn), lambda l, j: (l, 0, j)),
        out_shape=jax.ShapeDtypeStruct((DEPTH, 8, n), f32),
        compiler_params=_cparams(("parallel", "parallel"), 40),
        name="adaln_mods",
    )(cvec8, ada_w, ada_b.reshape(DEPTH, 1, n))


NORM_ROWS = 256


def _norm_mm_kernel(x_ref, m_ref, g_ref, *rest, shift_row, scale_row, swiglu):
    w_refs, (o_ref, h_ref) = rest[:-2], rest[-2:]
    tm = x_ref.shape[0]

    @pl.when(pl.program_id(1) == 0)
    def _():
        shift = m_ref[shift_row:shift_row + 1, :]
        scale1p = 1.0 + m_ref[scale_row:scale_row + 1, :]
        g = g_ref[...]

        def body(r, carry):
            sl = pl.ds(pl.multiple_of(r * NORM_ROWS, NORM_ROWS), NORM_ROWS)
            x = x_ref[sl, :]
            y = x * lax.rsqrt(jnp.mean(x * x, axis=-1, keepdims=True) + EPS) * g
            h_ref[sl, :] = (y * scale1p + shift).astype(bf16)
            return carry

        lax.fori_loop(0, tm // NORM_ROWS, body, 0)

    h = h_ref[...]
    if swiglu:
        gate, up = _dot(h, w_refs[0][...]), _dot(h, w_refs[1][...])
        o_ref[...] = (_silu(gate) * up).astype(o_ref.dtype)
    else:
        o_ref[...] = _dot(h, w_refs[0][...]).astype(o_ref.dtype)


def _lspec(block, layer, idx):
    return pl.BlockSpec((None,) + tuple(block), lambda *g: (layer,) + tuple(idx(*g)))


def _mods_spec(width, layer, cond0, rows_per_cond, tm, col):
    return pl.BlockSpec((None, None, 6, width),
                        lambda i, j: (layer, cond0 + (i * tm) // rows_per_cond, 0, j if col else 0))


def _norm_mm(x, mods, g, w, *, layer, cond0, ncond, shift_row, scale_row, swiglu, tm, tn, out_dtype, name):
    m_rows, k = x.shape
    n_out = w.shape[2] // (2 if swiglu else 1)
    n_tiles = n_out // tn
    w_specs = [_lspec((k, tn), layer, lambda i, j: (0, j))]
    if swiglu:
        w_specs.append(_lspec((k, tn), layer, lambda i, j: (0, n_tiles + j)))
    kern = functools.partial(_norm_mm_kernel, shift_row=shift_row, scale_row=scale_row, swiglu=swiglu)
    return pl.pallas_call(
        kern,
        grid=(m_rows // tm, n_tiles),
        in_specs=[pl.BlockSpec((tm, k), lambda i, j: (i, 0)),
                  _mods_spec(k, layer, cond0, m_rows // ncond, tm, False),
                  _lspec((1, k), layer, lambda i, j: (0, 0))] + w_specs,
        out_specs=pl.BlockSpec((tm, tn), lambda i, j: (i, j)),
        out_shape=jax.ShapeDtypeStruct((m_rows, n_out), out_dtype),
        scratch_shapes=[pltpu.VMEM((tm, k), bf16)],
        compiler_params=_cparams(("parallel", "arbitrary"), 48),
        name=name,
    )(x, mods, g, *([w] * len(w_specs)))


def _mm_res_kernel(*refs, n_in, gate_row):
    a_refs, w_refs = refs[:n_in], refs[n_in:2 * n_in]
    x_ref, m_ref, o_ref = refs[2 * n_in:]
    acc = _dot(a_refs[0][...], w_refs[0][...])
    for t in range(1, n_in):
        acc = acc + _dot(a_refs[t][...], w_refs[t][...])
    o_ref[...] = x_ref[...] + m_ref[gate_row:gate_row + 1, :] * acc


def _mm_res(acts, w, x, mods, *, layer, cond0, ncond, gate_row, tm, tn, vmem_mb, name):
    m_rows, n = x.shape
    n_in = len(acts)
    widths = [a.shape[1] for a in acts]
    offs = [sum(widths[:t]) for t in range(n_in)]
    in_specs = [pl.BlockSpec((tm, wd), lambda i, j: (i, 0)) for wd in widths]
    in_specs += [_lspec((wd, tn), layer, functools.partial(lambda i, j, rb: (rb, j), rb=_blk(off, wd)))
                 for wd, off in zip(widths, offs)]
    in_specs += [pl.BlockSpec((tm, tn), lambda i, j: (i, j)),
                 _mods_spec(tn, layer, cond0, m_rows // ncond, tm, True)]
    return pl.pallas_call(
        functools.partial(_mm_res_kernel, n_in=n_in, gate_row=gate_row),
        grid=(m_rows // tm, n // tn),
        in_specs=in_specs,
        out_specs=pl.BlockSpec((tm, tn), lambda i, j: (i, j)),
        out_shape=jax.ShapeDtypeStruct((m_rows, n), f32),
        compiler_params=_cparams(("parallel", "parallel"), vmem_mb),
        name=name,
    )(*acts, *([w] * n_in), x, mods)


def _rope(x, cos, s_up, s_dn):
    return x * cos + pltpu.roll(x, LANES - ROPE_F, 1) * s_up + pltpu.roll(x, ROPE_F, 1) * s_dn


def _mla_kv_rows(ckvn_bf, kr, wuk, wuv, kg, rope, k_scr, v_scr, r0):
    n = kr.shape[0]
    kn = _dot(ckvn_bf, wuk)
    v = _dot(ckvn_bf, wuv)
    krg = kr * kg[:, MLA_NOPE:]
    if rope is not None:
        krg = _rope(krg, *rope)
    kr_ss = jnp.sum(kr * kr, axis=-1, keepdims=True)
    for h in range(MLA_HEADS):
        knh = kn[:, h * MLA_NOPE:(h + 1) * MLA_NOPE]
        r = lax.rsqrt((jnp.sum(knh * knh, axis=-1, keepdims=True) + kr_ss) / MLA_QK + EPS)
        kh = jnp.concatenate([knh * r * kg[:, :MLA_NOPE], krg * r], axis=-1)
        k_scr[h, r0:r0 + n, :] = kh.astype(bf16)
        v_scr[h, r0:r0 + n, :] = v[:, h * MLA_V:(h + 1) * MLA_V].astype(bf16)


def _mla_attend(cq, wuq, qn_g, qg, rope, k_scr, v_scr, o_ref):
    cqn = (cq * _rms_scale(cq, MLA_Q_RANK) * qn_g).astype(bf16)
    q_all = _dot(cqn, wuq)
    scale = MLA_QK ** -0.5
    for h in range(MLA_HEADS):
        qh = q_all[:, h * MLA_QK_PAD:(h + 1) * MLA_QK_PAD]
        qh = qh * _rms_scale(qh, MLA_QK) * qg
        if rope is not None:
            qh = jnp.concatenate([qh[:, :MLA_NOPE], _rope(qh[:, MLA_NOPE:], *rope)], axis=-1)
        s = _dot_nt(qh.astype(bf16), k_scr[h]) * scale
        p = jnp.exp(s - jnp.max(s, axis=-1, keepdims=True))
        l = jnp.sum(p, axis=-1, keepdims=True)
        oh = _dot(p.astype(bf16), v_scr[h]) / l
        o_ref[:, h * MLA_V:(h + 1) * MLA_V] = oh.astype(o_ref.dtype)


def _krope_lanes(sm):
    lane = lax.broadcasted_iota(jnp.int32, sm.shape, 1)
    return jnp.where(lane < MLA_ROPE, sm, 0.0)


def _mla_ctx_kernel(cq_ref, ckv_ref, sm_ref, wuq_ref, wuk_ref, wuv_ref, qn_ref, kvn_ref, qg_ref, kg_ref,
                    *rest):
    o_ref, ckvn_ref, kr_ref, k_scr, v_scr = rest[-5:]
    ckv = ckv_ref[...]
    ckvn = ckv * _rms_scale(ckv, MLA_KV_RANK) * kvn_ref[...]
    ckvn_ref[...] = ckvn
    kr_ref[...] = sm_ref[:, 0:MLA_ROPE]
    _mla_kv_rows(ckvn.astype(bf16), _krope_lanes(sm_ref[...]), wuk_ref[...], wuv_ref[...], kg_ref[...],
                 None, k_scr, v_scr, 0)
    _mla_attend(cq_ref[...], wuq_ref[...], qn_ref[...], qg_ref[...], None, k_scr, v_scr, o_ref)


MLA_QB = 256


def _mla_lat_kernel(cq_ref, ckv_ref, sm_ref, cckv_ref, ckr_ref, rq_ref, rk_ref,
                    wuq_ref, wuk_ref, wuv_ref, qn_ref, kvn_ref, qg_ref, kg_ref, o_ref, k_scr, v_scr):
    @pl.when(pl.program_id(1) == 0)
    def _():
        kg = kg_ref[...]
        _mla_kv_rows(cckv_ref[0, 0].astype(bf16), ckr_ref[0, 0], wuk_ref[...], wuv_ref[...], kg,
                     None, k_scr, v_scr, 0)
        for c in range(DEC_SEQ // MLA_QB):
            rows = slice(c * MLA_QB, (c + 1) * MLA_QB)
            ckv = ckv_ref[rows, :]
            ckvn = ckv * _rms_scale(ckv, MLA_KV_RANK) * kvn_ref[...]
            rope = (rk_ref[0, rows, :], rk_ref[1, rows, :], rk_ref[2, rows, :])
            _mla_kv_rows(ckvn.astype(bf16), _krope_lanes(sm_ref[rows, :]), wuk_ref[...], wuv_ref[...], kg,
                         rope, k_scr, v_scr, PAST_LEN + c * MLA_QB)

    rope_q = (rq_ref[0], rq_ref[1], rq_ref[2])
    _mla_attend(cq_ref[...], wuq_ref[...], qn_ref[...], qg_ref[...], rope_q, k_scr, v_scr, o_ref)


def _mla_weight_specs(layer):
    zero = (lambda *a: (0, 0))
    return [_lspec((MLA_Q_RANK, MLA_HEADS * MLA_QK_PAD), layer, zero),
            _lspec((MLA_KV_RANK, MLA_HEADS * MLA_NOPE), layer, zero),
            _lspec((MLA_KV_RANK, MLA_HEADS * MLA_V), layer, zero),
            _lspec((1, MLA_Q_RANK), layer, zero),
            _lspec((1, MLA_KV_RANK), layer, zero),
            _lspec((1, MLA_QK_PAD), layer, zero),
            _lspec((1, MLA_QK_PAD), layer, zero)]


def _mla_weights(w):
    return [w["w_uq"], w["w_uk"], w["w_uv"], w["qn_g"], w["kvn_g"], w["q_g"], w["k_g"]]


def _stacked(prev, n_regular):
    prev = list(prev or [])
    specs = [pl.BlockSpec(memory_space=pl.ANY)] * len(prev)
    return prev, specs, {n_regular + t: 1 + t for t in range(len(prev))}


def _mla_ctx(u, w, layer, prev):
    t = SEQ
    in_specs = [pl.BlockSpec((t, MLA_Q_RANK), lambda i: (i, _blk(C_CQ, MLA_Q_RANK))),
                pl.BlockSpec((t, MLA_KV_RANK), lambda i: (i, _blk(C_CKV, MLA_KV_RANK))),
                pl.BlockSpec((t, LANES), lambda i: (i, _blk(C_SM, LANES)))] + _mla_weight_specs(layer)
    prev, alias_specs, aliases = _stacked(prev, len(in_specs))
    return pl.pallas_call(
        _mla_ctx_kernel,
        grid=(BATCH,),
        in_specs=in_specs + alias_specs,
        out_specs=[pl.BlockSpec((t, MLA_HEADS * MLA_V), lambda i: (i, 0)),
                   pl.BlockSpec((None, t, MLA_KV_RANK), lambda i: (i, layer, 0)),
                   pl.BlockSpec((None, t, MLA_ROPE), lambda i: (i, layer, 0))],
        out_shape=[jax.ShapeDtypeStruct((BATCH * t, MLA_HEADS * MLA_V), bf16),
                   jax.ShapeDtypeStruct((BATCH, DEPTH * t, MLA_KV_RANK), f32),
                   jax.ShapeDtypeStruct((BATCH, DEPTH * t, MLA_ROPE), f32)],
        input_output_aliases=aliases,
        scratch_shapes=[pltpu.VMEM((MLA_HEADS, t, MLA_QK_PAD), bf16),
                        pltpu.VMEM((MLA_HEADS, t, MLA_V), bf16)],
        compiler_params=_cparams(("parallel",), 40),
        name="mla_ctx",
    )(u, u, u, *_mla_weights(w), *prev)


def _mla_lat(u, cache_ckv, cache_kr, rope_tab, w, layer):
    t = DEC_SEQ
    nq = t // MLA_QB
    tk = PAST_LEN + t
    return pl.pallas_call(
        _mla_lat_kernel,
        grid=(DEC_BATCH, nq),
        in_specs=[pl.BlockSpec((MLA_QB, MLA_Q_RANK), lambda s, q: (s * nq + q, _blk(C_CQ, MLA_Q_RANK))),
                  pl.BlockSpec((t, MLA_KV_RANK), lambda s, q: (s, _blk(C_CKV, MLA_KV_RANK))),
                  pl.BlockSpec((t, LANES), lambda s, q: (s, _blk(C_SM, LANES))),
                  pl.BlockSpec((1, 1, PAST_LEN, MLA_KV_RANK), lambda s, q: (s, layer, 0, 0)),
                  pl.BlockSpec((1, 1, PAST_LEN, LANES), lambda s, q: (s, layer, 0, 0)),
                  pl.BlockSpec((3, MLA_QB, LANES), lambda s, q: (0, q, 0)),
                  pl.BlockSpec((3, t, LANES), lambda s, q: (0, 0, 0))] + _mla_weight_specs(layer),
        out_specs=pl.BlockSpec((MLA_QB, MLA_HEADS * MLA_V), lambda s, q: (s * nq + q, 0)),
        out_shape=jax.ShapeDtypeStruct((DEC_BATCH * t, MLA_HEADS * MLA_V), bf16),
        scratch_shapes=[pltpu.VMEM((MLA_HEADS, tk, MLA_QK_PAD), bf16),
                        pltpu.VMEM((MLA_HEADS, tk, MLA_V), bf16)],
        compiler_params=_cparams(("parallel", "arbitrary"), 48),
        name="mla_lat",
    )(u, u, u, cache_ckv, cache_kr, rope_tab, rope_tab, *_mla_weights(w))


CONV_PAD = 8
CONV_ROWS = 256
RECURRENCE_UNROLL = 4


def _conv_silu(x_ref, w_ref, b_ref, xp_ref, out_ref, t):
    c = x_ref.shape[1]
    xp_ref[0:CONV_PAD, :] = jnp.zeros((CONV_PAD, c), f32)
    xp_ref[CONV_PAD:CONV_PAD + t, :] = x_ref[...]
    xp_ref[CONV_PAD + t:2 * CONV_PAD + t, :] = jnp.zeros((CONV_PAD, c), f32)
    half = (CONV_W - 1) // 2
    for r0 in range(0, t, CONV_ROWS):
        acc = None
        for j in range(CONV_W):
            start = CONV_PAD - half + j + r0
            term = w_ref[j:j + 1, :] * xp_ref[start:start + CONV_ROWS, :]
            acc = term if acc is None else acc + term
        if b_ref is not None:
            acc = acc + b_ref[...]
        out_ref[r0:r0 + CONV_ROWS, :] = _silu(acc)


def _unit_tri_inverses(xs):
    n = xs[0].shape[0]
    shape = xs[0].shape
    ii = lax.broadcasted_iota(jnp.int32, shape, 0)
    jj = lax.broadcasted_iota(jnp.int32, shape, 1)
    first_half = jj < n
    eye = jnp.where((jj == ii) | (jj == ii + n), 1.0, 0.0)

    def hi_lo(a):
        hi = a.astype(bf16)
        return hi, (a - hi.astype(f32)).astype(bf16)

    def left(hi, lo):
        return jnp.where(first_half, hi, lo)

    levels = n.bit_length() - 2
    ps = [eye + x for x in xs]
    for j in range(levels + 1):
        first, last = j == 0, j == levels
        x_parts = [hi_lo(x) for x in xs]
        rhs = [jnp.concatenate([h, h, l, l], axis=0) for h, l in x_parts]
        lhs = []
        for p, xp in zip(ps, x_parts):
            blocks = []
            if not first:
                blocks.append(left(*hi_lo(p)))
            if not last:
                blocks.append(left(*xp))
            rows = blocks[0] if len(blocks) == 1 else jnp.concatenate(blocks, axis=0)
            lhs.append(jnp.concatenate([rows, rows], axis=1))
        rs = [_dot(l, r) for l, r in zip(lhs, rhs)]
        if not first:
            ps = [p + r[:n] for p, r in zip(ps, rs)]
        if not last:
            xs = [r[-n:] for r in rs]
    return ps


def _gdn_kernel(*refs, t, hb, cpi, has_s0, want_state, n_alias):
    it = iter(refs)
    q_ref, k_ref, v_ref, z_ref, sm_ref = (next(it) for _ in range(5))
    cwq_ref, cwk_ref, cwv_ref, gp_ref, ng_ref = (next(it) for _ in range(5))
    s0_ref = next(it) if has_s0 else None
    for _ in range(n_alias):
        next(it)
    o_ref = next(it)
    sfin_ref = next(it) if want_state else None
    xp_s, qn_s, kn_s, vn_s, beta_s, cum_s, cumt_s, wq_s, uu_s, kt_s, qk_s, eg_s, od_s, st_s = it

    ck = GDN_CHUNK
    nc = t // ck
    head0 = pl.program_id(1) * hb

    _conv_silu(q_ref, cwq_ref, None, xp_s, qn_s, t)
    _conv_silu(k_ref, cwk_ref, None, xp_s, kn_s, t)
    _conv_silu(v_ref, cwv_ref, None, xp_s, vn_s, t)
    for hh in range(hb):
        cols = slice(hh * GDN_DK, (hh + 1) * GDN_DK)
        qh = qn_s[:, cols]
        qn_s[:, cols] = qh * lax.rsqrt(jnp.sum(qh * qh, axis=-1, keepdims=True) + EPS) * (GDN_DK ** -0.5)
        kh = kn_s[:, cols]
        kn_s[:, cols] = kh * lax.rsqrt(jnp.sum(kh * kh, axis=-1, keepdims=True) + EPS)

    sm = sm_ref[...]
    beta_s[...] = _sigmoid(sm)
    g_all = -jnp.exp(gp_ref[0:1, :]) * _softplus(sm + gp_ref[1:2, :])

    ii = lax.broadcasted_iota(jnp.int32, (ck, LANES), 0)
    jj = lax.broadcasted_iota(jnp.int32, (ck, LANES), 1)
    jj = jnp.where(jj < ck, jj, jj - ck)
    incl = (ii >= jj, ii <= jj)
    strict = (ii > jj, ii < jj)
    tri = (incl[0][:, :ck].astype(bf16), incl[1][:, :ck].astype(bf16))

    for c in range(nc):
        rows = slice(c * ck, (c + 1) * ck)
        for d in range(2):
            cum = _tri_cumsum(tri[d], g_all[rows, :])
            cum_s[d, rows, :] = cum
            cumt_s[d, c] = jnp.concatenate([cum, cum], axis=0).T

    if has_s0:
        for d in range(2):
            for hh in range(hb):
                st_s[d * hb + hh] = s0_ref[0, 0, d, hh]
    else:
        st_s[...] = jnp.zeros(st_s.shape, f32)

    def phase1(it_idx, carry):
        chunks = [it_idx * cpi + e for e in range(cpi)]
        rows = [pl.ds(pl.multiple_of(c * ck, ck), ck) for c in chunks]
        beta_c = [beta_s[r, :] for r in rows]
        heads = [(e, hh) for e in range(cpi) for hh in range(hb)]
        kc, qc, vc, g_kk, g_qk = {}, {}, {}, {}, {}
        for e, hh in heads:
            cols = slice(hh * GDN_DK, (hh + 1) * GDN_DK)
            kc[e, hh] = kn_s[rows[e], cols]
            qc[e, hh] = qn_s[rows[e], cols]
            vc[e, hh] = vn_s[rows[e], cols]
        for key in heads:
            kcb = kc[key].astype(bf16)
            kc_dup = jnp.concatenate([kcb, kcb], axis=0)
            g_kk[key] = _dot_nt(kcb, kc_dup)
            g_qk[key] = _dot_nt(qc[key].astype(bf16), kc_dup)
        probs = [(e, hh, d) for e, hh in heads for d in range(2)]
        col, bcol, dec, gl, neg_a = {}, {}, {}, {}, []
        for e, hh, d in probs:
            key = (e, hh, d)
            lane_g = SM_A + d * GDN_HEADS + head0 + hh
            lane_b = SM_BETA + d * GDN_HEADS + head0 + hh
            cum_c = cum_s[d, rows[e], :]
            col[key] = _lane_col(cum_c, lane_g)
            bcol[key] = _lane_col(beta_c[e], lane_b)
            row = cumt_s[d, chunks[e], pl.ds(lane_g, 1), :]
            dec[key] = jnp.where(incl[d], jnp.exp(col[key] - row), 0.0)
            end = cum_c[ck - 1:ck, :] if d == 0 else cum_c[0:1, :]
            gl[key] = _lane_col(end, lane_g)
            neg_a.append(jnp.where(strict[d], -(bcol[key] * g_kk[e, hh] * dec[key]), 0.0))
        tms = _unit_tri_inverses(neg_a)
        ecol, rhs = {}, []
        for key in probs:
            e, hh, d = key
            ecol[key] = jnp.exp(col[key])
            rhs.append(jnp.concatenate([kc[e, hh] * (bcol[key] * ecol[key]), vc[e, hh] * bcol[key]],
                                       axis=-1).astype(bf16))
        wus = [_dot(tm[:, :ck].astype(bf16), r) for tm, r in zip(tms, rhs)]
        for key, wu in zip(probs, wus):
            e, hh, d = key
            idx = d * hb + hh
            r2 = pl.multiple_of(chunks[e] * 2 * ck, 2 * ck)
            wq_s[idx, pl.ds(r2, ck), :] = wu[:, :GDN_DK].astype(bf16)
            wq_s[idx, pl.ds(r2 + ck, ck), :] = (qc[e, hh] * ecol[key]).astype(bf16)
            uu_s[idx, rows[e], :] = wu[:, GDN_DK:]
            kt_s[idx, rows[e], :] = (kc[e, hh] * jnp.exp(gl[key] - col[key])).astype(bf16)
            qk_s[idx, rows[e], :] = jnp.where(incl[d], g_qk[e, hh] * dec[key], 0.0)[:, :ck].astype(bf16)
            eg_s[idx, pl.ds(chunks[e], 1), :] = jnp.broadcast_to(jnp.exp(gl[key]), (1, LANES))
        return carry

    lax.fori_loop(0, nc // cpi, phase1, 0, unroll=nc // cpi <= 2)

    def phase2(i, carry):
        probs = [(d, hh) for d in range(2) for hh in range(hb)]
        chunk = {0: i, 1: nc - 1 - i}
        rows = {d: pl.ds(pl.multiple_of(chunk[d] * ck, ck), ck) for d in range(2)}
        rows2 = {d: pl.ds(pl.multiple_of(chunk[d] * 2 * ck, 2 * ck), 2 * ck) for d in range(2)}
        s = [st_s[d * hb + hh] for d, hh in probs]
        ws = [_dot(wq_s[d * hb + hh, rows2[d], :], sv.astype(bf16)) for (d, hh), sv in zip(probs, s)]
        vb = [(uu_s[d * hb + hh, rows[d], :] - w[:ck]).astype(bf16) for (d, hh), w in zip(probs, ws)]
        o = [w[ck:] + _dot(qk_s[d * hb + hh, rows[d], :], v) for (d, hh), w, v in zip(probs, ws, vb)]
        ds = [_dot_tn(kt_s[d * hb + hh, rows[d], :], v) for (d, hh), v in zip(probs, vb)]
        for (d, hh), sv, dv, ov in zip(probs, s, ds, o):
            idx = d * hb + hh
            st_s[idx] = sv * eg_s[idx, pl.ds(chunk[d], 1), :] + dv
            od_s[d, rows[d], hh * GDN_DV:(hh + 1) * GDN_DV] = ov
        return carry

    lax.fori_loop(0, nc, phase2, 0, unroll=min(nc, RECURRENCE_UNROLL))

    for hh in range(hb):
        cols = slice(hh * GDN_DV, (hh + 1) * GDN_DV)
        o = od_s[0, :, cols] + od_s[1, :, cols]
        on = o * lax.rsqrt(jnp.mean(o * o, axis=-1, keepdims=True) + EPS) * ng_ref[...]
        o_ref[:, cols] = (on * _silu(z_ref[:, cols])).astype(o_ref.dtype)
    if want_state:
        for d in range(2):
            for hh in range(hb):
                sfin_ref[0, d, hh] = st_s[d * hb + hh]


def _gdn(u, wl, *, nseq, t, hb, cpi, s0, layer, want_state, prev=None):
    w = hb * GDN_DK
    nhb = GDN_HEADS // hb
    nc = t // GDN_CHUNK
    assert nc % cpi == 0
    gq, gk, gv = C_GQKV, C_GQKV + GDN_HEADS * GDN_DK, C_GQKV + 2 * GDN_HEADS * GDN_DK
    in_specs = [pl.BlockSpec((t, w), lambda i, j: (i, _blk(gq, w) + j)),
                pl.BlockSpec((t, w), lambda i, j: (i, _blk(gk, w) + j)),
                pl.BlockSpec((t, w), lambda i, j: (i, _blk(gv, w) + j)),
                pl.BlockSpec((t, w), lambda i, j: (i, _blk(C_GZ, w) + j)),
                pl.BlockSpec((t, LANES), lambda i, j: (i, _blk(C_SM, LANES))),
                _lspec((8, w), layer, lambda i, j: (0, j)),
                _lspec((8, w), layer, lambda i, j: (0, nhb + j)),
                _lspec((8, w), layer, lambda i, j: (0, 2 * nhb + j)),
                _lspec((8, LANES), layer, lambda i, j: (0, 0)),
                _lspec((1, GDN_DV), layer, lambda i, j: (0, 0))]
    args = [u, u, u, u, u, wl["gdn_cw"], wl["gdn_cw"], wl["gdn_cw"], wl["gdn_gp"], wl["gdn_ng"]]
    if s0 is not None:
        in_specs.append(pl.BlockSpec((1, 1, 2, hb, GDN_DK, GDN_DV), lambda i, j: (i, layer, 0, j, 0, 0)))
        args.append(s0)
    prev, alias_specs, aliases = _stacked(prev, len(in_specs))
    out_specs = [pl.BlockSpec((t, w), lambda i, j: (i, j))]
    out_shape = [jax.ShapeDtypeStruct((nseq * t, GDN_HEADS * GDN_DV), bf16)]
    if want_state:
        out_specs.append(pl.BlockSpec((1, None, 2, hb, GDN_DK, GDN_DV), lambda i, j: (i, layer, 0, j, 0, 0)))
        out_shape.append(jax.ShapeDtypeStruct((nseq, DEPTH, 2, GDN_HEADS, GDN_DK, GDN_DV), f32))
    scratch = [pltpu.VMEM((t + 2 * CONV_PAD, w), f32),
               pltpu.VMEM((t, w), f32), pltpu.VMEM((t, w), f32), pltpu.VMEM((t, w), f32),
               pltpu.VMEM((t, LANES), f32),
               pltpu.VMEM((2, t, LANES), f32),
               pltpu.VMEM((2, nc, LANES, LANES), f32),
               pltpu.VMEM((2 * hb, 2 * t, GDN_DK), bf16),
               pltpu.VMEM((2 * hb, t, GDN_DV), f32),
               pltpu.VMEM((2 * hb, t, GDN_DK), bf16),
               pltpu.VMEM((2 * hb, t, GDN_CHUNK), bf16),
               pltpu.VMEM((2 * hb, max(nc, 8), LANES), f32),
               pltpu.VMEM((2, t, w), f32),
               pltpu.VMEM((2 * hb, GDN_DK, GDN_DV), f32)]
    kern = functools.partial(_gdn_kernel, t=t, hb=hb, cpi=cpi, has_s0=s0 is not None, want_state=want_state,
                             n_alias=len(prev))
    return pl.pallas_call(
        kern,
        grid=(nseq, nhb),
        in_specs=in_specs + alias_specs,
        out_specs=out_specs,
        out_shape=out_shape,
        input_output_aliases=aliases,
        scratch_shapes=scratch,
        compiler_params=_cparams(("parallel", "parallel"), 48),
        name="gdn_" + ("lat" if s0 is not None else "ctx"),
    )(*args, *prev)


def _ssd_kernel(*refs, t, cpi, has_s0, want_state, n_alias):
    it = iter(refs)
    x_ref, bc_ref, z_ref, sm_ref = (next(it) for _ in range(4))
    cwx_ref, cwbc_ref, cbx_ref, cbbc_ref = (next(it) for _ in range(4))
    gp_ref, ex_ref, dv_ref, ng_ref = (next(it) for _ in range(4))
    s0_ref = next(it) if has_s0 else None
    for _ in range(n_alias):
        next(it)
    o_ref = next(it)
    sfin_ref = next(it) if want_state else None
    xp_s, xs_s, bcs_s, dt_s, da_s, y_s, ea_s, stc_s, cd_s, st_s = it

    ck = SSD_CHUNK
    nc = t // ck
    wg = HEAD_PER_GROUP * SSD_P
    nb = SSD_GROUPS * SSD_N

    _conv_silu(x_ref, cwx_ref, cbx_ref, xp_s, xs_s, t)
    _conv_silu(bc_ref, cwbc_ref, cbbc_ref, xp_s, bcs_s, t)

    dt_all = _softplus(sm_ref[...] + gp_ref[1:2, :])
    dt_s[...] = dt_all
    da_s[...] = dt_all * (-jnp.exp(gp_ref[0:1, :]))

    if has_s0:
        for d in range(2):
            s0 = jnp.concatenate([s0_ref[0, 0, d, h] for h in range(SSD_HEADS)], axis=0)
            st_s[d] = s0.T
    else:
        st_s[...] = jnp.zeros(st_s.shape, f32)

    ii = lax.broadcasted_iota(jnp.int32, (ck, ck), 0)
    jj = lax.broadcasted_iota(jnp.int32, (ck, ck), 1)
    incl = (ii >= jj, ii <= jj)
    tri = (incl[0].astype(bf16), incl[1].astype(bf16))
    lane = lax.broadcasted_iota(jnp.int32, (ck, LANES), 1)
    end_rows = 16
    dirs = (0, 1)

    def phase1(it_idx, carry):
        chunks = [it_idx * cpi + e for e in range(cpi)]
        rows = [pl.ds(pl.multiple_of(c * ck, ck), ck) for c in chunks]
        probs = [(e, d) for e in range(cpi) for d in dirs]
        dt_c = [dt_s[r, :] for r in rows]
        da_c = [da_s[r, :] for r in rows]
        cum = {(e, d): _tri_cumsum(tri[d], da_c[e]) for e, d in probs}
        cum_t = {key: cum[key].T for key in probs}
        dt_t = [v.T for v in dt_c]
        bcc = [bcs_s[r, :].astype(bf16) for r in rows]
        xc = [xs_s[r, :] for r in rows]
        xb = [v.astype(bf16) for v in xc]
        cb = {(e, g): _dot_nt(bcc[e][:, nb + g * SSD_N:nb + (g + 1) * SSD_N], bcc[e][:, g * SSD_N:(g + 1) * SSD_N])
              for e in range(cpi) for g in range(SSD_GROUPS)}
        heads = [(e, d, h) for e, d in probs for h in range(SSD_HEADS)]
        ms = []
        for e, d, h in heads:
            ln = SM_DT + d * SSD_HEADS + h
            col = _lane_col(cum[e, d], ln)
            row = cum_t[e, d][ln:ln + 1, :]
            dtrow = dt_t[e][ln:ln + 1, :]
            lm = jnp.where(incl[d], jnp.exp(col - row), 0.0)
            ms.append((cb[e, h // HEAD_PER_GROUP] * lm * dtrow).astype(bf16))
        ys = {key: _dot(m, xb[key[0]][:, (key[2] // 2) * LANES:(key[2] // 2 + 1) * LANES])
              for key, m in zip(heads, ms)}
        end = {(e, d): cum[e, d][ck - 1:ck, :] if d == 0 else cum[e, d][0:1, :] for e, d in probs}
        spread = {key: _select_dot(jnp.concatenate([jnp.exp(end[key] - cum[key]) * dt_c[key[0]], jnp.exp(cum[key]),
                                                    jnp.broadcast_to(jnp.exp(end[key]), (end_rows, LANES))],
                                                   axis=0), ex_ref[key[1]]) for key in probs}
        xsc = {key: (xc[key[0]] * spread[key][:ck]).astype(bf16) for key in probs}
        st_c = {(e, d, g): _dot_tn(bcc[e][:, g * SSD_N:(g + 1) * SSD_N], xsc[e, d][:, g * wg:(g + 1) * wg])
                for e, d in probs for g in range(SSD_GROUPS)}
        for e, d in probs:
            parts = [jnp.where(lane < SSD_P, ys[e, d, 2 * pr], ys[e, d, 2 * pr + 1]) for pr in range(SSD_HEADS // 2)]
            y_s[d, rows[e], :] = jnp.concatenate(parts, axis=-1)
            ea_s[d, rows[e], :] = spread[e, d][ck:2 * ck]
            cd_s[d, chunks[e]] = spread[e, d][2 * ck:2 * ck + 8]
            stc_s[d, chunks[e]] = jnp.concatenate([st_c[e, d, g] for g in range(SSD_GROUPS)], axis=-1)
        return carry

    lax.fori_loop(0, nc // cpi, phase1, 0)

    def phase2(i, carry):
        chunk = (i, nc - 1 - i)
        rows = [pl.ds(pl.multiple_of(c * ck, ck), ck) for c in chunk]
        st = [st_s[d] for d in dirs]
        stb = [v.astype(bf16) for v in st]
        y_off = {(d, g): _dot(bcs_s[rows[d], nb + g * SSD_N:nb + (g + 1) * SSD_N].astype(bf16),
                              stb[d][:, g * wg:(g + 1) * wg]) for d in dirs for g in range(SSD_GROUPS)}
        for d in dirs:
            off = jnp.concatenate([y_off[d, g] for g in range(SSD_GROUPS)], axis=-1)
            y_s[d, rows[d], :] = y_s[d, rows[d], :] + off * ea_s[d, rows[d], :]
            st_s[d] = st[d] * cd_s[d, chunk[d], 0:1, :] + stc_s[d, chunk[d]]
        return carry

    lax.fori_loop(0, nc, phase2, 0, unroll=min(nc, RECURRENCE_UNROLL))

    y = y_s[0] + y_s[1] + xs_s[...] * dv_ref[...]
    y = y * _silu(z_ref[...])
    for g in range(SSD_GROUPS):
        cols = slice(g * wg, (g + 1) * wg)
        yg = y[:, cols]
        o_ref[:, cols] = (yg * lax.rsqrt(jnp.mean(yg * yg, axis=-1, keepdims=True) + EPS)
                          * ng_ref[:, cols]).astype(o_ref.dtype)
    if want_state:
        for d in range(2):
            stt = st_s[d].T
            for h in range(SSD_HEADS):
                sfin_ref[0, d, h] = stt[h * SSD_P:(h + 1) * SSD_P, :]


def _ssd(u, wl, *, nseq, t, cpi, s0, layer, want_state, prev=None):
    wi, wbc = SSD_INNER, 2 * SSD_GROUPS * SSD_N
    nc = t // SSD_CHUNK
    assert nc % cpi == 0 and wi == wbc
    in_specs = [pl.BlockSpec((t, wi), lambda i: (i, _blk(C_XBC, wi))),
                pl.BlockSpec((t, wbc), lambda i: (i, _blk(C_XBC + wi, wbc))),
                pl.BlockSpec((t, wi), lambda i: (i, _blk(C_SZ, wi))),
                pl.BlockSpec((t, LANES), lambda i: (i, _blk(C_SM, LANES))),
                _lspec((8, wi), layer, lambda i: (0, 0)),
                _lspec((8, wbc), layer, lambda i: (0, 1)),
                _lspec((1, wi), layer, lambda i: (0, 0)),
                _lspec((1, wbc), layer, lambda i: (0, 1)),
                _lspec((8, LANES), layer, lambda i: (0, 0)),
                pl.BlockSpec((2, LANES, wi), lambda i: (0, 0, 0)),
                _lspec((1, wi), layer, lambda i: (0, 0)),
                _lspec((1, wi), layer, lambda i: (0, 0))]
    args = [u, u, u, u, wl["ssd_cw"], wl["ssd_cw"], wl["ssd_cb"], wl["ssd_cb"],
            wl["ssd_gp"], wl["ssd_ex"], wl["ssd_dv"], wl["ssd_ng"]]
    if s0 is not None:
        in_specs.append(pl.BlockSpec((1, 1, 2, SSD_HEADS, SSD_P, SSD_N), lambda i: (i, layer, 0, 0, 0, 0)))
        args.append(s0)
    prev, alias_specs, aliases = _stacked(prev, len(in_specs))
    out_specs = [pl.BlockSpec((t, wi), lambda i: (i, 0))]
    out_shape = [jax.ShapeDtypeStruct((nseq * t, wi), bf16)]
    if want_state:
        out_specs.append(pl.BlockSpec((1, None, 2, SSD_HEADS, SSD_P, SSD_N), lambda i: (i, layer, 0, 0, 0, 0)))
        out_shape.append(jax.ShapeDtypeStruct((nseq, DEPTH, 2, SSD_HEADS, SSD_P, SSD_N), f32))
    scratch = [pltpu.VMEM((t + 2 * CONV_PAD, wi), f32),
               pltpu.VMEM((t, wi), f32), pltpu.VMEM((t, wbc), f32),
               pltpu.VMEM((t, LANES), f32), pltpu.VMEM((t, LANES), f32),
               pltpu.VMEM((2, t, wi), f32),
               pltpu.VMEM((2, t, wi), f32),
               pltpu.VMEM((2, nc, SSD_N, wi), f32),
               pltpu.VMEM((2, nc, 8, wi), f32),
               pltpu.VMEM((2, SSD_N, wi), f32)]
    kern = functools.partial(_ssd_kernel, t=t, cpi=cpi, has_s0=s0 is not None, want_state=want_state,
                             n_alias=len(prev))
    return pl.pallas_call(
        kern,
        grid=(nseq,),
        in_specs=in_specs + alias_specs,
        out_specs=out_specs,
        out_shape=out_shape,
        input_output_aliases=aliases,
        scratch_shapes=scratch,
        compiler_params=_cparams(("parallel",), 48),
        name="ssd_" + ("lat" if s0 is not None else "ctx"),
    )(*args, *prev)


def _w_in_pieces():
    o_g = MLA_Q_RANK + MLA_KV_RANK + MLA_ROPE
    n_qkv = 2 * GDN_HEADS * GDN_DK + GDN_HEADS * GDN_DV
    n_gz = GDN_HEADS * GDN_DV
    o_s = o_g + n_qkv + n_gz + 4 * GDN_HEADS
    n_xbc = SSD_INNER + 2 * SSD_GROUPS * SSD_N
    return [(C_CQ, 0, MLA_Q_RANK),
            (C_SM, MLA_Q_RANK + MLA_KV_RANK, MLA_ROPE),
            (C_SM + SM_BETA, o_g + n_qkv + n_gz, 4 * GDN_HEADS),
            (C_SM + SM_DT, o_s + SSD_INNER + n_xbc, 2 * SSD_HEADS),
            (C_XBC, o_s + SSD_INNER, n_xbc),
            (C_CKV, MLA_Q_RANK, MLA_KV_RANK),
            (C_GZ, o_g + n_qkv, n_gz),
            (C_GQKV, o_g, n_qkv),
            (C_SZ, o_s, SSD_INNER)]


W_IN_ROWS = 256


def _w_in_permute_kernel(w_ref, o_ref):
    pad0 = C_SM + SM_DT + 2 * SSD_HEADS
    o_ref[0, :, pad0:C_XBC] = jnp.zeros((w_ref.shape[1], C_XBC - pad0), o_ref.dtype)
    for dst, src, width in _w_in_pieces():
        o_ref[0, :, dst:dst + width] = w_ref[0, :, src:src + width].astype(o_ref.dtype)


def _w_in_permute(w_in):
    depth, k, n = w_in.shape
    return pl.pallas_call(
        _w_in_permute_kernel,
        grid=(depth, k // W_IN_ROWS),
        in_specs=[pl.BlockSpec((1, W_IN_ROWS, n), lambda l, i: (l, i, 0))],
        out_specs=pl.BlockSpec((1, W_IN_ROWS, N_IN), lambda l, i: (l, i, 0)),
        out_shape=jax.ShapeDtypeStruct((depth, k, N_IN), bf16),
        compiler_params=_cparams(("parallel", "parallel"), 32),
        name="w_in_permute",
    )(w_in)


def _prep_weights(p):
    w_in_p = _w_in_permute(p["w_in"])

    w_uq = p["mla_w_uq"].reshape(DEPTH, MLA_Q_RANK, MLA_HEADS, MLA_QK)
    w_uq = jnp.pad(w_uq, ((0, 0), (0, 0), (0, 0), (0, MLA_QK_PAD - MLA_QK)))
    w_uq = w_uq.reshape(DEPTH, MLA_Q_RANK, MLA_HEADS * MLA_QK_PAD).astype(bf16)
    w_ukv = p["mla_w_ukv"].reshape(DEPTH, MLA_KV_RANK, MLA_HEADS, MLA_NOPE + MLA_V)
    w_uk = w_ukv[..., :MLA_NOPE].reshape(DEPTH, MLA_KV_RANK, MLA_HEADS * MLA_NOPE).astype(bf16)
    w_uv = w_ukv[..., MLA_NOPE:].reshape(DEPTH, MLA_KV_RANK, MLA_HEADS * MLA_V).astype(bf16)
    pad_g = lambda g: jnp.pad(g, ((0, 0), (0, MLA_QK_PAD - MLA_QK)))[:, None, :]

    def lane_rows(a_log, dt_bias, lane0):
        n = a_log.shape[1] * a_log.shape[2]
        rows = jnp.stack([a_log.reshape(DEPTH, n), dt_bias.reshape(DEPTH, n)], axis=1)
        return jnp.pad(rows.astype(f32), ((0, 0), (0, 6), (lane0, LANES - lane0 - n)))

    ex = np.zeros((2, LANES, SSD_INNER), np.float32)
    for d in range(2):
        for h in range(SSD_HEADS):
            ex[d, SM_DT + d * SSD_HEADS + h, h * SSD_P:(h + 1) * SSD_P] = 1.0

    pad_rows = lambda w: jnp.pad(w.astype(f32), ((0, 0), (0, 8 - CONV_W), (0, 0)))
    return dict(
        w_in=w_in_p, w_uq=w_uq, w_uk=w_uk, w_uv=w_uv,
        qn_g=p["mla_qnorm_g"][:, None, :], kvn_g=p["mla_kvnorm_g"][:, None, :],
        q_g=pad_g(p["mla_q_g"]), k_g=pad_g(p["mla_k_g"]),
        w_out=p["w_out"].astype(bf16), w_gu=p["ffn_w_gu"].astype(bf16), w_down=p["ffn_w_down"].astype(bf16),
        norm1_g=p["norm1_g"][:, None, :], norm2_g=p["norm2_g"][:, None, :],
        gdn_cw=pad_rows(p["gdn_conv_w"]), gdn_gp=lane_rows(p["gdn_a_log"], p["gdn_dt_bias"], SM_A),
        gdn_ng=p["gdn_norm_g"][:, None, :],
        ssd_cw=pad_rows(p["ssd_conv_w"]), ssd_cb=p["ssd_conv_b"][:, None, :],
        ssd_gp=lane_rows(p["ssd_a_log"], p["ssd_dt_bias"], SM_DT), ssd_ex=jnp.asarray(ex, dtype=bf16),
        ssd_dv=jnp.repeat(p["ssd_d"], SSD_P, axis=1)[:, None, :], ssd_ng=p["ssd_norm_g"][:, None, :],
    )


def _rope_tables(n_tokens):
    rows = n_tokens // GRID_W
    row = jnp.repeat(jnp.arange(rows, dtype=f32), GRID_W)
    col = jnp.tile(jnp.arange(GRID_W, dtype=f32), rows)
    inv = ROPE_BASE ** (-jnp.arange(ROPE_F, dtype=f32) / ROPE_F)
    ar, ac = row[:, None] * inv, col[:, None] * inv
    zero = jnp.zeros_like(ar)
    tail = jnp.zeros((n_tokens, LANES - MLA_ROPE), f32)
    cos = jnp.concatenate([jnp.cos(ar), jnp.cos(ar), jnp.cos(ac), jnp.cos(ac), tail], axis=-1)
    s_up = jnp.concatenate([-jnp.sin(ar), zero, -jnp.sin(ac), zero, tail], axis=-1)
    s_dn = jnp.concatenate([zero, jnp.sin(ar), zero, jnp.sin(ac), tail], axis=-1)
    return jnp.stack([cos, s_up, s_dn], axis=0)


FFN_TF = 512
DENSE_TM = 1024
IN_TN = 1024
OUT_TM = 512
DOWN_TN = 512


def _trunk_layer(x, mods, wl, l, *, nseq, t, latent, cache=None, stacked=None):
    cond = dict(layer=l, cond0=1 if latent else 0, ncond=nseq if latent else 1)
    u = _norm_mm(x, mods, wl["norm1_g"], wl["w_in"], shift_row=0, scale_row=1, swiglu=False,
                 tm=DENSE_TM, tn=IN_TN, out_dtype=f32, name="in_proj", **cond)
    if latent:
        o_mla = _mla_lat(u, cache["ckv"], cache["krope"], cache["rope"], wl, l)
        o_gdn, = _gdn(u, wl, nseq=nseq, t=t, hb=2, cpi=4, s0=cache["gdn"], layer=l, want_state=False)
        o_ssd, = _ssd(u, wl, nseq=nseq, t=t, cpi=2, s0=cache["ssd"], layer=l, want_state=False)
    else:
        prev = (lambda a, b: None) if stacked is None else (lambda a, b: stacked[a:b])
        o_mla, ckv_all, kr_all = _mla_ctx(u, wl, l, prev(0, 2))
        o_gdn, sg_all = _gdn(u, wl, nseq=nseq, t=t, hb=GDN_HEADS, cpi=2, s0=None, layer=l, want_state=True,
                             prev=prev(2, 3))
        o_ssd, ss_all = _ssd(u, wl, nseq=nseq, t=t, cpi=2, s0=None, layer=l, want_state=True, prev=prev(3, 4))
        stacked = (ckv_all, kr_all, sg_all, ss_all)
    x = _mm_res([o_mla, o_gdn, o_ssd], wl["w_out"], x, mods, gate_row=2, tm=OUT_TM, tn=D_MODEL,
                vmem_mb=48, name="out_proj", **cond)
    act = _norm_mm(x, mods, wl["norm2_g"], wl["w_gu"], shift_row=3, scale_row=4, swiglu=True,
                   tm=DENSE_TM, tn=FFN_TF, out_dtype=bf16, name="ffn_gu", **cond)
    x = _mm_res([act], wl["w_down"], x, mods, gate_row=5, tm=DENSE_TM, tn=DOWN_TN, vmem_mb=56,
                name="ffn_down", **cond)
    return x, stacked


def kernel(x_prompt, x_sample, cache_mla_ckv, cache_mla_krope, state_gdn, state_ssd, c, c_ctx, norm1_g, norm2_g, ada_w, ada_b, w_in, w_out, mla_qnorm_g, mla_w_uq, mla_kvnorm_g, mla_w_ukv, mla_q_g, mla_k_g, gdn_conv_w, gdn_a_log, gdn_dt_bias, gdn_norm_g, ssd_conv_w, ssd_conv_b, ssd_a_log, ssd_dt_bias, ssd_d, ssd_norm_g, ffn_w_gu, ffn_w_down):
    p = dict(norm1_g=norm1_g, norm2_g=norm2_g, w_in=w_in, w_out=w_out,
             mla_qnorm_g=mla_qnorm_g, mla_w_uq=mla_w_uq, mla_kvnorm_g=mla_kvnorm_g,
             mla_w_ukv=mla_w_ukv, mla_q_g=mla_q_g, mla_k_g=mla_k_g, gdn_conv_w=gdn_conv_w,
             gdn_a_log=gdn_a_log, gdn_dt_bias=gdn_dt_bias, gdn_norm_g=gdn_norm_g,
             ssd_conv_w=ssd_conv_w, ssd_conv_b=ssd_conv_b, ssd_a_log=ssd_a_log,
             ssd_dt_bias=ssd_dt_bias, ssd_d=ssd_d, ssd_norm_g=ssd_norm_g,
             ffn_w_gu=ffn_w_gu, ffn_w_down=ffn_w_down)
    w = _prep_weights(p)

    cvec = jnp.concatenate([c_ctx[None, :], c, jnp.zeros((8 - 1 - DEC_BATCH, D_MODEL), f32)], axis=0)
    mods = _mods(cvec, ada_w, ada_b).reshape(DEPTH, 8, 6, D_MODEL)

    cache = dict(ckv=cache_mla_ckv,
                 krope=jnp.pad(cache_mla_krope, ((0, 0), (0, 0), (0, 0), (0, LANES - MLA_ROPE))),
                 rope=_rope_tables(DEC_SEQ), gdn=state_gdn, ssd=state_ssd)

    xp = x_prompt.reshape(BATCH * SEQ, D_MODEL)
    xs = x_sample.reshape(DEC_BATCH * DEC_SEQ, D_MODEL)
    stacked = None
    for l in range(DEPTH):
        xp, stacked = _trunk_layer(xp, mods, w, l, nseq=BATCH, t=SEQ, latent=False, stacked=stacked)
        xs, _ = _trunk_layer(xs, mods, w, l, nseq=DEC_BATCH, t=DEC_SEQ, latent=True, cache=cache)
    ckv_all, kr_all, sg_all, ss_all = stacked
    return (xp.reshape(BATCH, SEQ, D_MODEL), xs.reshape(DEC_BATCH, DEC_SEQ, D_MODEL),
            ckv_all.reshape(BATCH, DEPTH, SEQ, MLA_KV_RANK), kr_all.reshape(BATCH, DEPTH, SEQ, MLA_ROPE),
            sg_all, ss_all)
```
